```python
import jax, jax.numpy as jnp
from jax import lax
import numpy as np

D_MODEL = 1024
BATCH = 8
SEQ = 4096
DEPTH = 4

HEAD_DIM = 64
NSA_HEADS = 8
NSA_KV_GROUPS = 2
NSA_HPG = NSA_HEADS // NSA_KV_GROUPS
CMP_LEN = 32
CMP_STRIDE = 16
CMP_HIDDEN = 256
SEL_BLOCK = 64
SEL_TOPK = 16
WINDOW = 512
NSA_QBLOCK = 64
FORCE_BONUS = 1e4
GLA_HEADS = 4
GLA_DK = 32
GLA_DV = 64
GLA_RANK = 16
GLA_TAU = 16.0
GLA_CHUNK = 64
SG_GROUPS = 4
SG_CH = 64
SG_CHUNK = 128
NSA_W = NSA_HEADS * HEAD_DIM
GLA_W = GLA_HEADS * GLA_DV
SG_W = SG_GROUPS * SG_CH
D_MIX = NSA_W + GLA_W + SG_W
KV_W = NSA_KV_GROUPS * HEAD_DIM
IN_SIZES = (NSA_W, KV_W, KV_W, KV_W, KV_W, KV_W, KV_W, NSA_HEADS * 3,
            GLA_HEADS * GLA_DK, GLA_HEADS * GLA_DK, GLA_W, GLA_RANK, GLA_W, 2 * SG_W)
D_IN = sum(IN_SIZES)
D_FF = -(-8 * D_MODEL // (3 * 256)) * 256
ROPE_THETA = 10000.0
NORM_EPS = 1e-6

kernel_name = "hybrid_nsa_gla_gmlp_trunk"


def rmsnorm(x, g):
    xf = x.astype(jnp.float32)
    y = xf * lax.rsqrt(jnp.mean(xf * xf, axis=-1, keepdims=True) + NORM_EPS)
    return (y * g.astype(jnp.float32)).astype(x.dtype)


def layernorm(x, g, b):
    xf = x.astype(jnp.float32)
    mu = jnp.mean(xf, axis=-1, keepdims=True)
    var = jnp.mean(jnp.square(xf - mu), axis=-1, keepdims=True)
    y = (xf - mu) * lax.rsqrt(var + NORM_EPS)
    return (y * g.astype(jnp.float32) + b.astype(jnp.float32)).astype(x.dtype)


def rope(x, pos):
    half = x.shape[-1] // 2
    inv = 1.0 / (ROPE_THETA ** (jnp.arange(half, dtype=jnp.float32) / half))
    ang = pos[:, None] * inv[None, :]
    cos = jnp.cos(ang)[None, :, None, :]
    sin = jnp.sin(ang)[None, :, None, :]
    xf = x.astype(jnp.float32)
    x1, x2 = xf[..., :half], xf[..., half:]
    return jnp.concatenate([x1 * cos - x2 * sin, x2 * cos + x1 * sin], axis=-1).astype(x.dtype)


def masked_softmax(s, mask):
    s = jnp.where(mask, s.astype(jnp.float32), -jnp.inf)
    m = jnp.max(s, axis=-1, keepdims=True)
    m = jnp.where(jnp.isfinite(m), m, 0.0)
    e = jnp.exp(s - m)
    d = jnp.sum(e, axis=-1, keepdims=True)
    return e / jnp.where(d > 0, d, 1.0)


def compress_blocks(k_raw, pos_emb, w1, w2):
    B, S, G, hd = k_raw.shape
    nc = (S - CMP_LEN) // CMP_STRIDE + 1
    idx = np.arange(nc)[:, None] * CMP_STRIDE + np.arange(CMP_LEN)[None, :]
    blk = k_raw[:, idx] + pos_emb[None, None, :, None, :]
    blk = blk.transpose(0, 1, 3, 2, 4).reshape(B, nc, G, CMP_LEN * hd)
    return jax.nn.gelu(blk @ w1) @ w2


def importance_map(nc, nsb):
    cs = np.arange(nc) * CMP_STRIDE
    ce = cs + CMP_LEN
    bs = np.arange(nsb) * SEL_BLOCK
    be = bs + SEL_BLOCK
    ov = np.clip(np.minimum(ce[:, None], be[None, :]) - np.maximum(cs[:, None], bs[None, :]), 0, None)
    return jnp.asarray(ov / CMP_LEN, jnp.float32)


def nsa_mixer(q, k_cmp_raw, v_cmp_raw, k_slc, v_slc, k_win, v_win, gates,
              cmp_pos_k, cmp_w1_k, cmp_w2_k, cmp_pos_v, cmp_w1_v, cmp_w2_v):
    B, S, _ = q.shape
    G, HPG, hd, QB = NSA_KV_GROUPS, NSA_HPG, HEAD_DIM, NSA_QBLOCK
    pos = jnp.arange(S, dtype=jnp.float32)
    q = rope(q.reshape(B, S, NSA_HEADS, hd), pos).reshape(B, S, G, HPG, hd) * (hd ** -0.5)
    nc = (S - CMP_LEN) // CMP_STRIDE + 1
    cmp_end_np = np.arange(nc) * CMP_STRIDE + CMP_LEN - 1
    kc = compress_blocks(k_cmp_raw.reshape(B, S, G, hd), cmp_pos_k, cmp_w1_k, cmp_w2_k)
    kc = rope(kc, jnp.asarray(cmp_end_np, jnp.float32))
    vc = compress_blocks(v_cmp_raw.reshape(B, S, G, hd), cmp_pos_v, cmp_w1_v, cmp_w2_v)
    cmp_end = jnp.asarray(cmp_end_np, jnp.int32)
    nsb = S // SEL_BLOCK
    topk = min(SEL_TOPK, nsb)
    ks = rope(k_slc.reshape(B, S, G, hd), pos)
    kb = ks.reshape(B, nsb, SEL_BLOCK, G, hd).transpose(0, 3, 1, 2, 4)
    vb = v_slc.reshape(B, nsb, SEL_BLOCK, G, hd).transpose(0, 3, 1, 2, 4)
    imp_map = importance_map(nc, nsb)
    blk_id = jnp.arange(nsb)
    bi = jnp.arange(B)[:, None, None, None]
    gi = jnp.arange(G)[None, :, None, None]
    kw = jnp.pad(rope(k_win.reshape(B, S, G, hd), pos), ((0, 0), (WINDOW, 0), (0, 0), (0, 0)))
    vw = jnp.pad(v_win.reshape(B, S, G, hd), ((0, 0), (WINDOW, 0), (0, 0), (0, 0)))
    gates = jax.nn.sigmoid(gates).reshape(B, S, G, HPG, 3)

    def query_block(i):
        s0 = i * QB
        t = s0 + jnp.arange(QB)
        qb = lax.dynamic_slice_in_dim(q, s0, QB, axis=1)
        gb = lax.dynamic_slice_in_dim(gates, s0, QB, axis=1)
        s = jnp.einsum('bqghd,bcgd->bghqc', qb, kc)
        p_cmp = masked_softmax(s, (cmp_end[None, :] <= t[:, None])[None, None, None])
        o_cmp = jnp.einsum('bghqc,bcgd->bqghd', p_cmp.astype(vc.dtype), vc)
        imp = jnp.einsum('bghqc,cn->bgqn', p_cmp, imp_map)
        tb = t // SEL_BLOCK
        valid = blk_id[None, :] * SEL_BLOCK <= t[:, None]
        forced = (blk_id[None, :] == 0) | (blk_id[None, :] == tb[:, None]) | (blk_id[None, :] == tb[:, None] - 1)
        score = jnp.where(valid, imp + jnp.where(forced, FORCE_BONUS, 0.0), -jnp.inf)
        _, idx = lax.top_k(score, topk)
        ksel = kb[bi, gi, idx].reshape(B, G, QB, topk * SEL_BLOCK, hd)
        vsel = vb[bi, gi, idx].reshape(B, G, QB, topk * SEL_BLOCK, hd)
        tok = (idx[..., None] * SEL_BLOCK + jnp.arange(SEL_BLOCK)).reshape(B, G, 1, QB, topk * SEL_BLOCK)
        s = jnp.einsum('bqghd,bgqmd->bghqm', qb, ksel)
        p = masked_softmax(s, tok <= t[None, None, None, :, None])
        o_slc = jnp.einsum('bghqm,bgqmd->bqghd', p.astype(vsel.dtype), vsel)
        kwb = lax.dynamic_slice_in_dim(kw, s0, WINDOW + QB, axis=1)
        vwb = lax.dynamic_slice_in_dim(vw, s0, WINDOW + QB, axis=1)
        kpos = s0 - WINDOW + jnp.arange(WINDOW + QB)
        wmask = (kpos[None, :] <= t[:, None]) & (kpos[None, :] > t[:, None] - WINDOW) & (kpos[None, :] >= 0)
        s = jnp.einsum('bqghd,bkgd->bghqk', qb, kwb)
        p = masked_softmax(s, wmask[None, None, None])
        o_win = jnp.einsum('bghqk,bkgd->bqghd', p.astype(vwb.dtype), vwb)
        return gb[..., 0:1] * o_cmp + gb[..., 1:2] * o_slc + gb[..., 2:3] * o_win

    out = lax.map(query_block, jnp.arange(S // QB))
    return out.transpose(1, 0, 2, 3, 4, 5).reshape(B, S, NSA_W)


def gla_mixer(q, k, v, a_lr, r, w_up, b_up, norm_g):
    B, S, _ = q.shape
    H, DK, DV, C = GLA_HEADS, GLA_DK, GLA_DV, GLA_CHUNK
    f32 = jnp.float32
    q = q.reshape(B, S, H, DK).astype(f32) * (DK ** -0.5)
    k = k.reshape(B, S, H, DK).astype(f32)
    v = v.reshape(B, S, H, DV).astype(f32)
    g = jax.nn.log_sigmoid((a_lr @ w_up + b_up).astype(f32)).reshape(B, S, H, DK) / GLA_TAU
    n = S // C

    def to_chunks(a):
        return a.reshape(B, n, C, H, a.shape[-1]).transpose(1, 0, 3, 2, 4)

    causal = jnp.tril(jnp.ones((C, C), dtype=bool))

    def step(state, inp):
        qc, kc, vc, gc = inp
        bcum = jnp.cumsum(gc, axis=2)
        o_inter = jnp.einsum('bhck,bhkv->bhcv', qc * jnp.exp(bcum), state)
        diff = bcum[:, :, :, None, :] - bcum[:, :, None, :, :]
        decay = jnp.exp(jnp.where(causal[None, None, :, :, None], diff, -jnp.inf))
        attn = jnp.einsum('bhik,bhjk,bhijk->bhij', qc, kc, decay)
        o_intra = jnp.einsum('bhij,bhjv->bhiv', attn, vc)
        blast = bcum[:, :, -1:, :]
        state = jnp.exp(blast[:, :, 0, :])[..., None] * state + jnp.einsum(
            'bhjk,bhjv->bhkv', kc * jnp.exp(blast - bcum), vc)
        return state, o_inter + o_intra

    state0 = jnp.zeros((B, H, DK, DV), f32)
    _, o = lax.scan(step, state0, (to_chunks(q), to_chunks(k), to_chunks(v), to_chunks(g)))
    o = o.transpose(1, 0, 3, 2, 4).reshape(B, S, H, DV)
    o = rmsnorm(o, norm_g) * jax.nn.silu(r.astype(f32)).reshape(B, S, H, DV)
    return o.reshape(B, S, GLA_W).astype(r.dtype)


def spatial_gating(uv, ln_g, ln_b, w_s, b_s):
    B, S, _ = uv.shape
    nch = S // SG_CHUNK
    u, v = jnp.split(jax.nn.gelu(uv), 2, axis=-1)
    v = layernorm(v, ln_g, ln_b).reshape(B, nch, SG_CHUNK, SG_GROUPS, SG_CH)
    w = jnp.tril(w_s)
    s = jnp.einsum('gij,bnjgc->bnigc', w, v) + b_s.T[None, None, :, :, None]
    out = u.reshape(B, nch, SG_CHUNK, SG_GROUPS, SG_CH) * s
    return out.reshape(B, S, SG_W)


def split_columns(z):
    offs = np.cumsum(IN_SIZES)[:-1]
    return jnp.split(z, [int(o) for o in offs], axis=-1)


def setup_inputs(seed: int = 0) -> dict:
    key = jax.random.key(seed)
    ks = jax.random.split(key, 32)
    L = DEPTH
    res = (2.0 * DEPTH) ** -0.5

    def nrm(k, shape, scale):
        return jax.random.normal(k, shape, jnp.float32) * scale

    def gain(k, shape):
        return 1.0 + nrm(k, shape, 0.02)

    return {
        "x": nrm(ks[0], (BATCH, SEQ, D_MODEL), 1.0),
        "attn_norm": gain(ks[1], (L, D_MODEL)),
        "w_in": nrm(ks[2], (L, D_MODEL, D_IN), D_MODEL ** -0.5),
        "cmp_pos_k": nrm(ks[3], (L, CMP_LEN, HEAD_DIM), 0.1),
        "cmp_w1_k": nrm(ks[4], (L, CMP_LEN * HEAD_DIM, CMP_HIDDEN), (CMP_LEN * HEAD_DIM) ** -0.5),
        "cmp_w2_k": nrm(ks[5], (L, CMP_HIDDEN, HEAD_DIM), CMP_HIDDEN ** -0.5),
        "cmp_pos_v": nrm(ks[6], (L, CMP_LEN, HEAD_DIM), 0.1),
        "cmp_w1_v": nrm(ks[7], (L, CMP_LEN * HEAD_DIM, CMP_HIDDEN), (CMP_LEN * HEAD_DIM) ** -0.5),
        "cmp_w2_v": nrm(ks[8], (L, CMP_HIDDEN, HEAD_DIM), CMP_HIDDEN ** -0.5),
        "gla_w_up": nrm(ks[9], (L, GLA_RANK, GLA_HEADS * GLA_DK), GLA_RANK ** -0.5),
        "gla_b_up": nrm(ks[10], (L, GLA_HEADS * GLA_DK), 0.1),
        "gla_norm": gain(ks[11], (L, GLA_DV)),
        "sg_ln_g": gain(ks[12], (L, SG_W)),
        "sg_ln_b": nrm(ks[13], (L, SG_W), 0.02),
        "sg_w": nrm(ks[14], (L, SG_GROUPS, SG_CHUNK, SG_CHUNK), SG_CHUNK ** -0.5),
        "sg_b": 1.0 + nrm(ks[15], (L, SG_GROUPS, SG_CHUNK), 0.1),
        "w_out": nrm(ks[16], (L, D_MIX, D_MODEL), D_MIX ** -0.5 * res),
        "ffn_norm": gain(ks[17], (L, D_MODEL)),
        "w_gate": nrm(ks[18], (L, D_MODEL, D_FF), D_MODEL ** -0.5),
        "w_up": nrm(ks[19], (L, D_MODEL, D_FF), D_MODEL ** -0.5),
        "w_down": nrm(ks[20], (L, D_FF, D_MODEL), D_FF ** -0.5 * res),
        "final_norm": gain(ks[21], (D_MODEL,)),
    }


def reference(x, attn_norm, w_in, cmp_pos_k, cmp_w1_k, cmp_w2_k, cmp_pos_v, cmp_w1_v, cmp_w2_v,
              gla_w_up, gla_b_up, gla_norm, sg_ln_g, sg_ln_b, sg_w, sg_b, w_out,
              ffn_norm, w_gate, w_up, w_down, final_norm):
    for l in range(DEPTH):
        h = rmsnorm(x, attn_norm[l])
        (q, kc, vc, ksl, vsl, kwn, vwn, gts, gq, gk, gv, ga, gr, uv) = split_columns(h @ w_in[l])
        o_nsa = nsa_mixer(q, kc, vc, ksl, vsl, kwn, vwn, gts,
                          cmp_pos_k[l], cmp_w1_k[l], cmp_w2_k[l], cmp_pos_v[l], cmp_w1_v[l], cmp_w2_v[l])
        o_gla = gla_mixer(gq, gk, gv, ga, gr, gla_w_up[l], gla_b_up[l], gla_norm[l])
        o_sg = spatial_gating(uv, sg_ln_g[l], sg_ln_b[l], sg_w[l], sg_b[l])
        x = x + jnp.concatenate([o_nsa, o_gla.astype(x.dtype), o_sg], axis=-1) @ w_out[l]
        h = rmsnorm(x, ffn_norm[l])
        x = x + (jax.nn.silu(h @ w_gate[l]) * (h @ w_up[l])) @ w_down[l]
    return rmsnorm(x, final_norm)
```

```python
import functools
import math

import numpy as np
import jax
import jax.numpy as jnp
from jax import lax
from jax.experimental import pallas as pl
from jax.experimental.pallas import tpu as pltpu

HEAD_DIM = 64
NSA_HEADS = 8
NSA_KV_GROUPS = 2
NSA_HPG = NSA_HEADS // NSA_KV_GROUPS
CMP_LEN = 32
CMP_STRIDE = 16
CMP_HIDDEN = 256
SEL_BLOCK = 64
SEL_TOPK = 16
WINDOW = 512
GLA_HEADS = 4
GLA_DK = 32
GLA_DV = 64
GLA_RANK = 16
GLA_TAU = 16.0
SG_GROUPS = 4
SG_CH = 64
SG_CHUNK = 128
NSA_W = NSA_HEADS * HEAD_DIM
GLA_W = GLA_HEADS * GLA_DV
SG_W = SG_GROUPS * SG_CH
KV_W = NSA_KV_GROUPS * HEAD_DIM
IN_SIZES = (NSA_W, KV_W, KV_W, KV_W, KV_W, KV_W, KV_W, NSA_HEADS * 3,
            GLA_HEADS * GLA_DK, GLA_HEADS * GLA_DK, GLA_W, GLA_RANK, GLA_W, 2 * SG_W)
ROPE_THETA = 10000.0
NORM_EPS = 1e-6

LANES = 128
NEG_BIG = -1e30
VMEM_LIMIT = 56 * 1024 * 1024

QPAD_W = NSA_HEADS * LANES
GEXP_W = 3 * NSA_W
GLA_QK_W = GLA_HEADS * GLA_DK
GLA_BLK = 256
GLA_SUB = 16
SEL_CH = 512
WIN_KEYS = WINDOW + 2 * SEL_BLOCK

HI = lax.Precision.HIGHEST
F32 = jnp.float32
BF16 = jnp.bfloat16


def _cparams(sem):
    return pltpu.CompilerParams(dimension_semantics=sem, vmem_limit_bytes=VMEM_LIMIT)


def _dot(a, b, precision=None):
    return jnp.dot(a, b, preferred_element_type=F32, precision=precision)


def _dot_nt(a, b, precision=None):
    return lax.dot_general(a, b, (((1,), (1,)), ((), ())),
                           preferred_element_type=F32, precision=precision)


def _dot_tn(a, b, precision=None):
    return lax.dot_general(a, b, (((0,), (0,)), ((), ())),
                           preferred_element_type=F32, precision=precision)


def _gelu_tanh(x):
    c = math.sqrt(2.0 / math.pi)
    return 0.5 * x * (1.0 + jnp.tanh(c * (x + 0.044715 * (x * x * x))))


def _rope_lanes(x, cos, sin_signed):
    n = x.shape[-1]
    lane = lax.broadcasted_iota(jnp.int32, x.shape, 1)
    first_half = (lane % HEAD_DIM) < (HEAD_DIM // 2)
    partner = jnp.where(first_half,
                        pltpu.roll(x, n - HEAD_DIM // 2, 1),
                        pltpu.roll(x, HEAD_DIM // 2, 1))
    return x * cos + partner * sin_signed


_INPROJ_OUT = (
    ("qpad", QPAD_W, F32), ("kc", KV_W, F32), ("vc", KV_W, F32),
    ("ksl", KV_W, BF16), ("vsl", KV_W, BF16), ("kwn", KV_W, BF16), ("vwn", KV_W, BF16),
    ("gexp", GEXP_W, F32), ("gq", GLA_QK_W, F32), ("gk", GLA_QK_W, F32),
    ("gv", GLA_W, F32), ("ga", LANES, F32), ("gr", GLA_W, F32), ("uv", 2 * SG_W, F32))
_INPROJ_W = sum(w for _, w, _ in _INPROJ_OUT)


def _inproj_kernel(x_ref, g_ref, w_ref, cos_ref, sin_ref, *out_refs):
    x = x_ref[...]
    hn = x * lax.rsqrt(jnp.mean(x * x, axis=-1, keepdims=True) + NORM_EPS) * g_ref[...]
    z = _dot(hn.astype(BF16), w_ref[...])
    cos = cos_ref[...]
    sin = sin_ref[...]
    off = 0
    for (name, width, dtype), o_ref in zip(_INPROJ_OUT, out_refs):
        if name in ("qpad", "ksl", "kwn"):
            for j in range(width // LANES):
                blk = _rope_lanes(z[:, off + j * LANES: off + (j + 1) * LANES], cos, sin)
                if name == "qpad":
                    blk = blk * (HEAD_DIM ** -0.5)
                o_ref[:, j * LANES:(j + 1) * LANES] = blk.astype(dtype)
        else:
            o_ref[...] = z[:, off:off + width].astype(dtype)
        off += width


def _inproj(x2, gain, w_p, cos_t, sin_t, seq, tm):
    T, D = x2.shape
    nper = seq // tm
    out_shape = [jax.ShapeDtypeStruct((T, w), dt) for _, w, dt in _INPROJ_OUT]
    out_specs = [pl.BlockSpec((tm, w), lambda i: (i, 0)) for _, w, _ in _INPROJ_OUT]
    return pl.pallas_call(
        _inproj_kernel,
        grid=(T // tm,),
        in_specs=[
            pl.BlockSpec((tm, D), lambda i: (i, 0)),
            pl.BlockSpec((1, D), lambda i: (0, 0)),
            pl.BlockSpec((D, _INPROJ_W), lambda i: (0, 0)),
            pl.BlockSpec((tm, LANES), lambda i: (i % nper, 0)),
            pl.BlockSpec((tm, LANES), lambda i: (i % nper, 0)),
        ],
        out_specs=out_specs,
        out_shape=out_shape,
        compiler_params=_cparams(("parallel",)),
        name="inproj",
    )(x2, gain, w_p, cos_t, sin_t)


def _prep_w_in(w_in):
    offs = np.cumsum((0,) + IN_SIZES)
    seg = [w_in[:, offs[k]:offs[k + 1]] for k in range(len(IN_SIZES))]
    (wq, wkc, wvc, wksl, wvsl, wkwn, wvwn, wg, wgq, wgk, wgv, wga, wgr, wuv) = seg
    D = w_in.shape[0]
    zeros64 = jnp.zeros((D, HEAD_DIM), w_in.dtype)
    qcols = []
    for h in range(NSA_HEADS):
        wh = wq[:, h * HEAD_DIM:(h + 1) * HEAD_DIM]
        qcols += [wh, zeros64] if h // NSA_HPG == 0 else [zeros64, wh]
    gcols = []
    for k in range(3):
        for h in range(NSA_HEADS):
            gcols.append(jnp.broadcast_to(wg[:, h * 3 + k][:, None], (D, HEAD_DIM)))
    wga_p = jnp.pad(wga, ((0, 0), (0, LANES - GLA_RANK)))
    return jnp.concatenate(qcols + [wkc, wvc, wksl, wvsl, wkwn, wvwn] + gcols
                           + [wgq, wgk, wgv, wga_p, wgr, wuv], axis=1).astype(BF16)


def _rope_tables(pos):
    half = HEAD_DIM // 2
    inv = 1.0 / (ROPE_THETA ** (jnp.arange(half, dtype=F32) / half))
    ang = pos.astype(F32)[:, None] * inv[None, :]
    cos = jnp.cos(ang)
    sin = jnp.sin(ang)
    cos_t = jnp.concatenate([cos, cos, cos, cos], axis=1)
    sin_t = jnp.concatenate([-sin, sin, -sin, sin], axis=1)
    return cos_t, sin_t


def _cmp_kernel(xk_ref, xv_ref, wek_ref, wev_ref, w1k_ref, w1v_ref, pk_ref, pv_ref,
                w2k_ref, w2v_ref, cos_ref, sin_ref, ok_ref, ov_ref):
    nrow = xk_ref.shape[1]
    row = lax.broadcasted_iota(jnp.int32, (nrow, LANES), 0)

    def compress(x_ref, we_ref, w1_ref, p_ref, w2_ref):
        h = _dot(x_ref[0].astype(BF16), we_ref[...])
        posb = _dot(p_ref[...].astype(BF16), w1_ref[...])[0:1, :]
        y = jnp.zeros((nrow, LANES), F32)
        for g in range(NSA_KV_GROUPS):
            a = h[:, g * CMP_HIDDEN:(g + 1) * CMP_HIDDEN]
            b = h[:, (NSA_KV_GROUPS + g) * CMP_HIDDEN:(NSA_KV_GROUPS + g + 1) * CMP_HIDDEN]
            hid = a + pltpu.roll(b, nrow - 1, 0) + posb
            y = y + _dot(_gelu_tanh(hid).astype(BF16), w2_ref[g])
        return y

    yk = _rope_lanes(compress(xk_ref, wek_ref, w1k_ref, pk_ref, w2k_ref), cos_ref[...], sin_ref[...])
    yv = compress(xv_ref, wev_ref, w1v_ref, pv_ref, w2v_ref)
    keep = row < nrow - 1
    ok_ref[0] = jnp.where(keep, yk, 0.0)
    ov_ref[0] = jnp.where(keep, yv, 0.0)


def _prep_cmp_weights(w1, w2, pos):
    half = CMP_LEN // CMP_STRIDE
    H = w1.shape[1]
    w1r = w1.reshape(half, CMP_STRIDE, HEAD_DIM, H)
    cols = []
    for a in range(half):
        for g in range(NSA_KV_GROUPS):
            blk = jnp.zeros((CMP_STRIDE, NSA_KV_GROUPS, HEAD_DIM, H), w1.dtype)
            blk = blk.at[:, g].set(w1r[a])
            cols.append(blk.reshape(CMP_STRIDE * KV_W, H))
    wexp = jnp.concatenate(cols, axis=1).astype(BF16)
    w2p = jnp.stack([jnp.pad(w2, ((0, 0), (g * HEAD_DIM, KV_W - (g + 1) * HEAD_DIM)))
                     for g in range(NSA_KV_GROUPS)]).astype(BF16)
    posf = jnp.pad(pos.reshape(1, CMP_LEN * HEAD_DIM), ((0, 7), (0, 0)))
    return wexp, w1.astype(BF16), posf, w2p


def _compress(kc, vc, wk, wv, cos_c, sin_c, batch, seq):
    nrow = seq // CMP_STRIDE
    xk = kc.reshape(batch, nrow, CMP_STRIDE * KV_W)
    xv = vc.reshape(batch, nrow, CMP_STRIDE * KV_W)
    wek, w1k, pk, w2k = wk
    wev, w1v, pv, w2v = wv
    full = lambda a: pl.BlockSpec(a.shape, lambda b: (0,) * a.ndim)
    xspec = pl.BlockSpec((1, nrow, CMP_STRIDE * KV_W), lambda b: (b, 0, 0))
    ospec = pl.BlockSpec((1, nrow, KV_W), lambda b: (b, 0, 0))
    return pl.pallas_call(
        _cmp_kernel,
        grid=(batch,),
        in_specs=[xspec, xspec, full(wek), full(wev), full(w1k), full(w1v), full(pk), full(pv),
                  full(w2k), full(w2v), full(cos_c), full(sin_c)],
        out_specs=[ospec, ospec],
        out_shape=[jax.ShapeDtypeStruct((batch, nrow, KV_W), F32)] * 2,
        compiler_params=_cparams(("parallel",)),
        name="compress",
    )(xk, xv, wek, wev, w1k, w1v, pk, pv, w2k, w2v, cos_c, sin_c)


def _unpad_heads(acc):
    qb = SEL_BLOCK
    lane = lax.broadcasted_iota(jnp.int32, (qb, LANES), 1)
    low = lane < HEAD_DIM
    outs = []
    for j in range(NSA_HEADS // 2):
        a = acc[(2 * j) * qb:(2 * j + 1) * qb]
        b = acc[(2 * j + 1) * qb:(2 * j + 2) * qb]
        if (2 * j) // NSA_HPG == 0:
            outs.append(jnp.where(low, a, pltpu.roll(b, HEAD_DIM, 1)))
        else:
            outs.append(jnp.where(low, pltpu.roll(a, HEAD_DIM, 1), b))
    return jnp.concatenate(outs, axis=1)


def _nsa_kernel(q_ref, g_ref, kc_ref, vc_ref, ksl_ref, vsl_ref, kwn_ref, vwn_ref,
                emat_ref, impt_ref, o_ref, imp_scr, m_scr, l_scr, acc_scr):
    qb = SEL_BLOCK
    rows = NSA_HEADS * qb
    i = pl.program_id(1)
    t0 = i * qb
    q = jnp.concatenate([q_ref[:, h * LANES:(h + 1) * LANES] for h in range(NSA_HEADS)], axis=0)
    qbf = q.astype(BF16)
    t = t0 + lax.broadcasted_iota(jnp.int32, (rows, 1), 0) % qb

    ncmp = kc_ref.shape[1]
    s = _dot_nt(q, kc_ref[0], HI)
    cend = lax.broadcasted_iota(jnp.int32, (1, ncmp), 1) * CMP_STRIDE + (CMP_LEN - 1)
    vis = cend <= t
    sm = jnp.where(vis, s, NEG_BIG)
    mx = jnp.max(sm, axis=1, keepdims=True)
    e = jnp.where(vis, jnp.exp(sm - mx), 0.0)
    d = jnp.sum(e, axis=1, keepdims=True)
    p = e / jnp.where(d > 0, d, 1.0)
    o_cmp = _dot(p.astype(BF16), vc_ref[0].astype(BF16))
    psum = jnp.concatenate(
        [sum(p[(g * NSA_HPG + h) * qb:(g * NSA_HPG + h + 1) * qb] for h in range(NSA_HPG))
         for g in range(NSA_KV_GROUPS)], axis=0)
    imp_scr[...] = _dot_nt(impt_ref[...], psum, HI)

    nsbp = imp_scr.shape[0]
    imp = imp_scr[...]
    n_id = lax.broadcasted_iota(jnp.int32, (nsbp, 1), 0)
    valid = n_id <= i
    forced = (n_id == 0) | (n_id == i) | (n_id == i - 1)
    nforced = 1 + (i >= 1).astype(jnp.int32) + (i >= 2).astype(jnp.int32)

    def rank_body(m, cnt):
        rowm = imp_scr[pl.ds(m, 1), :]
        beats = (rowm > imp) | ((rowm == imp) & (m < n_id))
        return cnt + beats.astype(F32)

    cnt = lax.fori_loop(1, jnp.maximum(i - 1, 1), rank_body, jnp.zeros(imp.shape, F32))
    budget = (SEL_TOPK - nforced).astype(F32)
    sel = valid & (forced | (cnt < budget))
    selb = jnp.where(sel, 1.0, 0.0)

    m_scr[...] = jnp.full(m_scr.shape, NEG_BIG, F32)
    l_scr[...] = jnp.zeros(l_scr.shape, F32)
    acc_scr[...] = jnp.zeros(acc_scr.shape, F32)

    def slc_body(c, carry):
        k0 = pl.multiple_of(c * SEL_CH, SEL_CH)
        k = ksl_ref[0, pl.ds(k0, SEL_CH), :]
        v = vsl_ref[0, pl.ds(k0, SEL_CH), :]
        sc = _dot_nt(qbf, k)
        bias = _dot_tn(selb, emat_ref[c])
        bias = jnp.concatenate([bias[g * qb:(g + 1) * qb]
                                for g in range(NSA_KV_GROUPS) for _ in range(NSA_HPG)], axis=0)
        kpos = k0 + lax.broadcasted_iota(jnp.int32, (1, SEL_CH), 1)
        ok = (bias > 0.5) & (kpos <= t)
        scm = jnp.where(ok, sc, NEG_BIG)
        m_old = m_scr[...]
        m_new = jnp.maximum(m_old, jnp.max(scm, axis=1, keepdims=True))
        alpha = jnp.exp(m_old - m_new)
        pr = jnp.where(ok, jnp.exp(scm - m_new), 0.0)
        l_scr[...] = alpha * l_scr[...] + jnp.sum(pr, axis=1, keepdims=True)
        acc_scr[...] = alpha * acc_scr[...] + _dot(pr.astype(BF16), v)
        m_scr[...] = m_new
        return carry

    lax.fori_loop(0, i // (SEL_CH // qb) + 1, slc_body, 0)
    lsum = l_scr[...]
    o_slc = acc_scr[...] / jnp.where(lsum > 0, lsum, 1.0)

    w0 = pl.multiple_of(jnp.maximum(t0 + qb - WIN_KEYS, 0), qb)
    kw = kwn_ref[0, pl.ds(w0, WIN_KEYS), :]
    vw = vwn_ref[0, pl.ds(w0, WIN_KEYS), :]
    sw = _dot_nt(qbf, kw)
    kpos = w0 + lax.broadcasted_iota(jnp.int32, (1, WIN_KEYS), 1)
    okw = (kpos <= t) & (kpos > t - WINDOW)
    swm = jnp.where(okw, sw, NEG_BIG)
    mw = jnp.max(swm, axis=1, keepdims=True)
    ew = jnp.where(okw, jnp.exp(swm - mw), 0.0)
    dw = jnp.sum(ew, axis=1, keepdims=True)
    o_win = _dot(ew.astype(BF16), vw) / jnp.where(dw > 0, dw, 1.0)

    gate = jax.nn.sigmoid(g_ref[...])
    o_ref[...] = (gate[:, 0:NSA_W] * _unpad_heads(o_cmp)
                  + gate[:, NSA_W:2 * NSA_W] * _unpad_heads(o_slc)
                  + gate[:, 2 * NSA_W:3 * NSA_W] * _unpad_heads(o_win))


def _nsa(qpad, gexp, kcmp, vcmp, ksl, vsl, kwn, vwn, emat, impt, batch, seq):
    qb = SEL_BLOCK
    nq = seq // qb
    rows = NSA_HEADS * qb
    kv3 = lambda a: a.reshape(batch, seq, KV_W)
    per_b = lambda n: pl.BlockSpec((1, n, KV_W), lambda b, i: (b, 0, 0))
    full = lambda a: pl.BlockSpec(a.shape, lambda b, i: (0,) * a.ndim)
    return pl.pallas_call(
        _nsa_kernel,
        grid=(batch, nq),
        in_specs=[
            pl.BlockSpec((qb, QPAD_W), lambda b, i: (b * nq + i, 0)),
            pl.BlockSpec((qb, GEXP_W), lambda b, i: (b * nq + i, 0)),
            per_b(kcmp.shape[1]), per_b(vcmp.shape[1]),
            per_b(seq), per_b(seq), per_b(seq), per_b(seq),
            full(emat), full(impt),
        ],
        out_specs=pl.BlockSpec((qb, NSA_W), lambda b, i: (b * nq + i, 0)),
        out_shape=jax.ShapeDtypeStruct((batch * seq, NSA_W), F32),
        scratch_shapes=[
            pltpu.VMEM((impt.shape[0], NSA_KV_GROUPS * qb), F32),
            pltpu.VMEM((rows, 1), F32),
            pltpu.VMEM((rows, 1), F32),
            pltpu.VMEM((rows, LANES), F32),
        ],
        compiler_params=_cparams(("parallel", "arbitrary")),
        name="nsa",
    )(qpad, gexp, kcmp, vcmp, kv3(ksl), kv3(vsl), kv3(kwn), kv3(vwn), emat, impt)


def _nsa_constants(seq):
    nc_pad = seq // CMP_STRIDE
    nsb = seq // SEL_BLOCK
    nsb_pad = max(nsb, 8)
    cs = np.arange(nc_pad) * CMP_STRIDE
    ce = cs + CMP_LEN
    bs = np.arange(nsb_pad) * SEL_BLOCK
    be = bs + SEL_BLOCK
    ov = np.clip(np.minimum(ce[None, :], be[:, None]) - np.maximum(cs[None, :], bs[:, None]), 0, None)
    impt = jnp.asarray(ov / CMP_LEN, F32)
    emat = np.arange(seq)[None, :] // SEL_BLOCK == np.arange(nsb_pad)[:, None]
    emat = jnp.asarray(emat.reshape(nsb_pad, seq // SEL_CH, SEL_CH).transpose(1, 0, 2), F32)
    return emat, impt


def _gla_kernel(q_ref, k_ref, v_ref, a_ref, r_ref, wup_ref, bup_ref, ng_ref, o_ref, st_ref):
    blk = q_ref.shape[0]
    nsub = blk // GLA_SUB

    @pl.when(pl.program_id(1) == 0)
    def _():
        st_ref[...] = jnp.zeros(st_ref.shape, F32)

    x = _dot(a_ref[...], wup_ref[...], HI) + bup_ref[...]
    g = (jnp.minimum(x, 0.0) - jnp.log1p(jnp.exp(-jnp.abs(x)))) / GLA_TAU
    ri = lax.broadcasted_iota(jnp.int32, (blk, blk), 0)
    ci = lax.broadcasted_iota(jnp.int32, (blk, blk), 1)
    b = _dot(jnp.where(ci <= ri, 1.0, 0.0), g, HI)
    blast = b[blk - 1:blk, :]
    q = q_ref[...] * (GLA_DK ** -0.5)
    k = k_ref[...]
    v = v_ref[...]
    vb = v.astype(BF16)

    st = st_ref[...]
    o = _dot_nt((q * jnp.exp(b)).astype(BF16), st.astype(BF16))
    khat = k * jnp.exp(blast - b)
    srow = lax.broadcasted_iota(jnp.int32, st.shape, 0) // GLA_DV
    scol = lax.broadcasted_iota(jnp.int32, st.shape, 1) // GLA_DK
    st_ref[...] = st * jnp.exp(blast) + jnp.where(srow == scol, _dot_tn(v, khat), 0.0)

    krow = lax.broadcasted_iota(jnp.int32, (blk, 1), 0)
    lane_qk = lax.broadcasted_iota(jnp.int32, (1, GLA_QK_W), 1) // GLA_DK
    lane_v = lax.broadcasted_iota(jnp.int32, (1, GLA_W), 1) // GLA_DV
    qrow = lax.broadcasted_iota(jnp.int32, (GLA_HEADS * GLA_SUB, 1), 0) % GLA_SUB
    kcol = lax.broadcasted_iota(jnp.int32, (1, blk), 1)
    for c in range(nsub):
        lo = c * GLA_SUB
        ref_b = b[lo:lo + 1, :]
        qt = q[lo:lo + GLA_SUB] * jnp.exp(b[lo:lo + GLA_SUB] - ref_b)
        kt = k * jnp.exp(jnp.where(krow < lo + GLA_SUB, ref_b - b, 0.0))
        qs = jnp.concatenate([jnp.where(lane_qk == h, qt, 0.0) for h in range(GLA_HEADS)], axis=0)
        a = _dot_nt(qs.astype(BF16), kt.astype(BF16))
        a = jnp.where(kcol <= lo + qrow, a, 0.0)
        r = _dot(a.astype(BF16), vb)
        oi = sum(jnp.where(lane_v == h, r[h * GLA_SUB:(h + 1) * GLA_SUB], 0.0) for h in range(GLA_HEADS))
        o_ref[lo:lo + GLA_SUB, :] = o[lo:lo + GLA_SUB] + oi

    o = o_ref[...]
    gi = lax.broadcasted_iota(jnp.int32, (GLA_W, GLA_W), 0) // GLA_DV
    gj = lax.broadcasted_iota(jnp.int32, (GLA_W, GLA_W), 1) // GLA_DV
    ms = _dot(o * o, jnp.where(gi == gj, 1.0 / GLA_DV, 0.0), HI)
    rr = r_ref[...]
    o_ref[...] = o * lax.rsqrt(ms + NORM_EPS) * ng_ref[...] * (rr * jax.nn.sigmoid(rr))


def _gla(gq, gk, gv, ga, gr, wup, bup, ng, batch, seq):
    blk = min(GLA_BLK, seq)
    nb = seq // blk
    tok = lambda w: pl.BlockSpec((blk, w), lambda b, j: (b * nb + j, 0))
    full = lambda a: pl.BlockSpec(a.shape, lambda b, j: (0,) * a.ndim)
    return pl.pallas_call(
        _gla_kernel,
        grid=(batch, nb),
        in_specs=[tok(GLA_QK_W), tok(GLA_QK_W), tok(GLA_W), tok(LANES), tok(GLA_W),
                  full(wup), full(bup), full(ng)],
        out_specs=tok(GLA_W),
        out_shape=jax.ShapeDtypeStruct((batch * seq, GLA_W), F32),
        scratch_shapes=[pltpu.VMEM((GLA_W, GLA_QK_W), F32)],
        compiler_params=_cparams(("parallel", "arbitrary")),
        name="gla",
    )(gq, gk, gv, ga, gr, wup, bup, ng)


def _sg_kernel(uv_ref, lg_ref, lb_ref, w_ref, bias_ref, o_ref):
    a = _gelu_tanh(uv_ref[...])
    u = a[:, :SG_W]
    v = a[:, SG_W:]
    mu = jnp.mean(v, axis=-1, keepdims=True)
    var = jnp.mean(jnp.square(v - mu), axis=-1, keepdims=True)
    vn = ((v - mu) * lax.rsqrt(var + NORM_EPS) * lg_ref[...] + lb_ref[...]).astype(BF16)
    n = w_ref.shape[1]
    ri = lax.broadcasted_iota(jnp.int32, (n, n), 0)
    ci = lax.broadcasted_iota(jnp.int32, (n, n), 1)
    lane_g = lax.broadcasted_iota(jnp.int32, (1, SG_W), 1) // SG_CH
    s = bias_ref[...]
    for g in range(SG_GROUPS):
        wg = jnp.where(ci <= ri, w_ref[g], 0.0).astype(BF16)
        s = s + jnp.where(lane_g == g, _dot(wg, vn), 0.0)
    o_ref[...] = u * s


def _spatial_gating(uv, ln_g, ln_b, w_s, bias_exp):
    T = uv.shape[0]
    full = lambda a: pl.BlockSpec(a.shape, lambda i: (0,) * a.ndim)
    return pl.pallas_call(
        _sg_kernel,
        grid=(T // SG_CHUNK,),
        in_specs=[pl.BlockSpec((SG_CHUNK, 2 * SG_W), lambda i: (i, 0)),
                  full(ln_g), full(ln_b), full(w_s), full(bias_exp)],
        out_specs=pl.BlockSpec((SG_CHUNK, SG_W), lambda i: (i, 0)),
        out_shape=jax.ShapeDtypeStruct((T, SG_W), F32),
        compiler_params=_cparams(("parallel",)),
        name="spatial_gating",
    )(uv, ln_g, ln_b, w_s, bias_exp)


def _outffn_kernel(x_ref, on_ref, og_ref, os_ref, wo_ref, fg_ref, wg_ref, wu_ref, wd_ref,
                   fin_ref, o_ref, *, final):
    x = x_ref[...]
    x = x + _dot(on_ref[...].astype(BF16), wo_ref[0:NSA_W, :])
    x = x + _dot(og_ref[...].astype(BF16), wo_ref[NSA_W:NSA_W + GLA_W, :])
    x = x + _dot(os_ref[...].astype(BF16), wo_ref[NSA_W + GLA_W:, :])
    hn = (x * lax.rsqrt(jnp.mean(x * x, axis=-1, keepdims=True) + NORM_EPS) * fg_ref[...]).astype(BF16)
    gt = _dot(hn, wg_ref[...])
    up = _dot(hn, wu_ref[...])
    y = x + _dot((gt * jax.nn.sigmoid(gt) * up).astype(BF16), wd_ref[...])
    if final:
        y = y * lax.rsqrt(jnp.mean(y * y, axis=-1, keepdims=True) + NORM_EPS) * fin_ref[...]
    o_ref[...] = y


def _outffn(x2, o_nsa, o_gla, o_sg, wo, fgain, wg, wu, wd, fin, tm, final):
    T, D = x2.shape
    row = lambda w: pl.BlockSpec((tm, w), lambda i: (i, 0))
    const = lambda a: pl.BlockSpec(a.shape, lambda i: (0,) * a.ndim)
    return pl.pallas_call(
        functools.partial(_outffn_kernel, final=final),
        grid=(T // tm,),
        in_specs=[row(D), row(NSA_W), row(GLA_W), row(SG_W), const(wo), const(fgain),
                  const(wg), const(wu), const(wd), const(fin)],
        out_specs=row(D),
        out_shape=jax.ShapeDtypeStruct((T, D), F32),
        compiler_params=_cparams(("parallel",)),
        name="outproj_ffn",
    )(x2, o_nsa, o_gla, o_sg, wo, fgain, wg, wu, wd, fin)


def kernel(x, attn_norm, w_in, cmp_pos_k, cmp_w1_k, cmp_w2_k, cmp_pos_v, cmp_w1_v, cmp_w2_v,
           gla_w_up, gla_b_up, gla_norm, sg_ln_g, sg_ln_b, sg_w, sg_b, w_out,
           ffn_norm, w_gate, w_up, w_down, final_norm):
    batch, seq, d_model = x.shape
    depth = w_in.shape[0]
    T = batch * seq
    assert seq % GLA_BLK == 0 and seq >= WIN_KEYS and seq % SEL_CH == 0
    tm_in = 256
    tm_ffn = 256

    cos_t, sin_t = _rope_tables(jnp.arange(seq))
    nrow = seq // CMP_STRIDE
    cos_c, sin_c = _rope_tables(jnp.arange(nrow) * CMP_STRIDE + CMP_LEN - 1)
    emat, impt = _nsa_constants(seq)

    x2 = x.reshape(T, d_model)
    for l in range(depth):
        (qpad, kc, vc, ksl, vsl, kwn, vwn, gexp, gq, gk, gv, ga, gr, uv) = _inproj(
            x2, attn_norm[l][None, :], _prep_w_in(w_in[l]), cos_t, sin_t, seq, tm_in)
        kcmp, vcmp = _compress(
            kc, vc, _prep_cmp_weights(cmp_w1_k[l], cmp_w2_k[l], cmp_pos_k[l]),
            _prep_cmp_weights(cmp_w1_v[l], cmp_w2_v[l], cmp_pos_v[l]), cos_c, sin_c, batch, seq)
        o_nsa = _nsa(qpad, gexp, kcmp, vcmp, ksl, vsl, kwn, vwn, emat, impt, batch, seq)
        wup = jnp.pad(gla_w_up[l], ((0, LANES - GLA_RANK), (0, 0)))
        o_gla = _gla(gq, gk, gv, ga, gr, wup, gla_b_up[l][None, :],
                     jnp.tile(gla_norm[l], GLA_HEADS)[None, :], batch, seq)
        bias_exp = jnp.repeat(sg_b[l].T, SG_CH, axis=1)
        o_sg = _spatial_gating(uv, sg_ln_g[l][None, :], sg_ln_b[l][None, :], sg_w[l], bias_exp)
        x2 = _outffn(x2, o_nsa, o_gla, o_sg, w_out[l].astype(BF16), ffn_norm[l][None, :],
                     w_gate[l].astype(BF16), w_up[l].astype(BF16), w_down[l].astype(BF16),
                     final_norm[None, :], tm_ffn, final=(l == depth - 1))
    return x2.reshape(batch, seq, d_model)
```

```python
import functools
import math

import numpy as np
import jax
import jax.numpy as jnp
from jax import lax
from jax.experimental import pallas as pl
from jax.experimental.pallas import tpu as pltpu

HEAD_DIM = 64
NSA_HEADS = 8
NSA_KV_GROUPS = 2
NSA_HPG = NSA_HEADS // NSA_KV_GROUPS
CMP_LEN = 32
CMP_STRIDE = 16
CMP_HIDDEN = 256
SEL_BLOCK = 64
SEL_TOPK = 16
WINDOW = 512
GLA_HEADS = 4
GLA_DK = 32
GLA_DV = 64
GLA_RANK = 16
GLA_TAU = 16.0
SG_GROUPS = 4
SG_CH = 64
SG_CHUNK = 128
NSA_W = NSA_HEADS * HEAD_DIM
GLA_W = GLA_HEADS * GLA_DV
SG_W = SG_GROUPS * SG_CH
KV_W = NSA_KV_GROUPS * HEAD_DIM
IN_SIZES = (NSA_W, KV_W, KV_W, KV_W, KV_W, KV_W, KV_W, NSA_HEADS * 3,
            GLA_HEADS * GLA_DK, GLA_HEADS * GLA_DK, GLA_W, GLA_RANK, GLA_W, 2 * SG_W)
ROPE_THETA = 10000.0
NORM_EPS = 1e-6

LANES = 128
NEG_BIG = -1e30
VMEM_LIMIT = 56 * 1024 * 1024

QPAD_W = NSA_HEADS * LANES
GEXP_W = 3 * NSA_W
GLA_QK_W = GLA_HEADS * GLA_DK
GLA_BLK = 256
GLA_SUB = 16
SEL_CH = 1024
WIN_KEYS = WINDOW + 2 * SEL_BLOCK
NSA_ROW_PARTS = 2

HI = lax.Precision.HIGHEST
F32 = jnp.float32
BF16 = jnp.bfloat16


def _cparams(sem):
    return pltpu.CompilerParams(dimension_semantics=sem, vmem_limit_bytes=VMEM_LIMIT)


def _dot(a, b, precision=None):
    return jnp.dot(a, b, preferred_element_type=F32, precision=precision)


def _dot_nt(a, b, precision=None):
    return lax.dot_general(a, b, (((1,), (1,)), ((), ())),
                           preferred_element_type=F32, precision=precision)


def _dot_tn(a, b, precision=None):
    return lax.dot_general(a, b, (((0,), (0,)), ((), ())),
                           preferred_element_type=F32, precision=precision)


def _gelu_tanh(x):
    c = math.sqrt(2.0 / math.pi)
    return 0.5 * x * (1.0 + jnp.tanh(c * (x + 0.044715 * (x * x * x))))


def _rope_lanes(x, cos, sin_signed):
    n = x.shape[-1]
    lane = lax.broadcasted_iota(jnp.int32, x.shape, 1)
    first_half = (lane % HEAD_DIM) < (HEAD_DIM // 2)
    partner = jnp.where(first_half,
                        pltpu.roll(x, n - HEAD_DIM // 2, 1),
                        pltpu.roll(x, HEAD_DIM // 2, 1))
    return x * cos + partner * sin_signed


_INPROJ_OUT = (
    ("qpad", QPAD_W, F32), ("kc", KV_W, F32), ("vc", KV_W, F32),
    ("ksl", KV_W, BF16), ("vsl", KV_W, BF16), ("kwn", KV_W, BF16), ("vwn", KV_W, BF16),
    ("gexp", GEXP_W, F32), ("gq", GLA_QK_W, F32), ("gk", GLA_QK_W, F32),
    ("gv", GLA_W, F32), ("ga", LANES, F32), ("gr", GLA_W, F32), ("uv", 2 * SG_W, F32))
_INPROJ_W = sum(w for _, w, _ in _INPROJ_OUT)


def _inproj_kernel(x_ref, g_ref, w_ref, cos_ref, sin_ref, *out_refs):
    x = x_ref[...]
    hn = x * lax.rsqrt(jnp.mean(x * x, axis=-1, keepdims=True) + NORM_EPS) * g_ref[...]
    z = _dot(hn.astype(BF16), w_ref[...])
    cos = cos_ref[...]
    sin = sin_ref[...]
    off = 0
    for (name, width, dtype), o_ref in zip(_INPROJ_OUT, out_refs):
        if name in ("qpad", "ksl", "kwn"):
            for j in range(width // LANES):
                blk = _rope_lanes(z[:, off + j * LANES: off + (j + 1) * LANES], cos, sin)
                if name == "qpad":
                    blk = blk * (HEAD_DIM ** -0.5)
                o_ref[:, j * LANES:(j + 1) * LANES] = blk.astype(dtype)
        else:
            o_ref[...] = z[:, off:off + width].astype(dtype)
        off += width


def _inproj(x2, gain, w_p, cos_t, sin_t, seq, tm):
    T, D = x2.shape
    nper = seq // tm
    out_shape = [jax.ShapeDtypeStruct((T, w), dt) for _, w, dt in _INPROJ_OUT]
    out_specs = [pl.BlockSpec((tm, w), lambda i: (i, 0)) for _, w, _ in _INPROJ_OUT]
    return pl.pallas_call(
        _inproj_kernel,
        grid=(T // tm,),
        in_specs=[
            pl.BlockSpec((tm, D), lambda i: (i, 0)),
            pl.BlockSpec((1, D), lambda i: (0, 0)),
            pl.BlockSpec((D, _INPROJ_W), lambda i: (0, 0)),
            pl.BlockSpec((tm, LANES), lambda i: (i % nper, 0)),
            pl.BlockSpec((tm, LANES), lambda i: (i % nper, 0)),
        ],
        out_specs=out_specs,
        out_shape=out_shape,
        compiler_params=_cparams(("parallel",)),
        name="inproj",
    )(x2, gain, w_p, cos_t, sin_t)


def _prep_w_in(w_in):
    offs = np.cumsum((0,) + IN_SIZES)
    seg = [w_in[:, offs[k]:offs[k + 1]] for k in range(len(IN_SIZES))]
    (wq, wkc, wvc, wksl, wvsl, wkwn, wvwn, wg, wgq, wgk, wgv, wga, wgr, wuv) = seg
    D = w_in.shape[0]
    zeros64 = jnp.zeros((D, HEAD_DIM), w_in.dtype)
    qcols = []
    for h in range(NSA_HEADS):
        wh = wq[:, h * HEAD_DIM:(h + 1) * HEAD_DIM]
        qcols += [wh, zeros64] if h // NSA_HPG == 0 else [zeros64, wh]
    gcols = []
    for k in range(3):
        for h in range(NSA_HEADS):
            gcols.append(jnp.broadcast_to(wg[:, h * 3 + k][:, None], (D, HEAD_DIM)))
    wga_p = jnp.pad(wga, ((0, 0), (0, LANES - GLA_RANK)))
    return jnp.concatenate(qcols + [wkc, wvc, wksl, wvsl, wkwn, wvwn] + gcols
                           + [wgq, wgk, wgv, wga_p, wgr, wuv], axis=1).astype(BF16)


def _rope_tables(pos):
    half = HEAD_DIM // 2
    inv = 1.0 / (ROPE_THETA ** (jnp.arange(half, dtype=F32) / half))
    ang = pos.astype(F32)[:, None] * inv[None, :]
    cos = jnp.cos(ang)
    sin = jnp.sin(ang)
    cos_t = jnp.concatenate([cos, cos, cos, cos], axis=1)
    sin_t = jnp.concatenate([-sin, sin, -sin, sin], axis=1)
    return cos_t, sin_t


def _cmp_kernel(xk_ref, xv_ref, wek_ref, wev_ref, w1k_ref, w1v_ref, pk_ref, pv_ref,
                w2k_ref, w2v_ref, cos_ref, sin_ref, ok_ref, ov_ref):
    nrow = xk_ref.shape[1]
    row = lax.broadcasted_iota(jnp.int32, (nrow, LANES), 0)

    def compress(x_ref, we_ref, w1_ref, p_ref, w2_ref):
        h = _dot(x_ref[0].astype(BF16), we_ref[...])
        posb = _dot(p_ref[...].astype(BF16), w1_ref[...])[0:1, :]
        y = jnp.zeros((nrow, LANES), F32)
        for g in range(NSA_KV_GROUPS):
            a = h[:, g * CMP_HIDDEN:(g + 1) * CMP_HIDDEN]
            b = h[:, (NSA_KV_GROUPS + g) * CMP_HIDDEN:(NSA_KV_GROUPS + g + 1) * CMP_HIDDEN]
            hid = a + pltpu.roll(b, nrow - 1, 0) + posb
            y = y + _dot(_gelu_tanh(hid).astype(BF16), w2_ref[g])
        return y

    yk = _rope_lanes(compress(xk_ref, wek_ref, w1k_ref, pk_ref, w2k_ref), cos_ref[...], sin_ref[...])
    yv = compress(xv_ref, wev_ref, w1v_ref, pv_ref, w2v_ref)
    keep = row < nrow - 1
    ok_ref[0] = jnp.where(keep, yk, 0.0)
    ov_ref[0] = jnp.where(keep, yv, 0.0)


def _prep_cmp_weights(w1, w2, pos):
    half = CMP_LEN // CMP_STRIDE
    H = w1.shape[1]
    w1r = w1.reshape(half, CMP_STRIDE, HEAD_DIM, H)
    cols = []
    for a in range(half):
        for g in range(NSA_KV_GROUPS):
            blk = jnp.zeros((CMP_STRIDE, NSA_KV_GROUPS, HEAD_DIM, H), w1.dtype)
            blk = blk.at[:, g].set(w1r[a])
            cols.append(blk.reshape(CMP_STRIDE * KV_W, H))
    wexp = jnp.concatenate(cols, axis=1).astype(BF16)
    w2p = jnp.stack([jnp.pad(w2, ((0, 0), (g * HEAD_DIM, KV_W - (g + 1) * HEAD_DIM)))
                     for g in range(NSA_KV_GROUPS)]).astype(BF16)
    posf = jnp.pad(pos.reshape(1, CMP_LEN * HEAD_DIM), ((0, 7), (0, 0)))
    return wexp, w1.astype(BF16), posf, w2p


def _compress(kc, vc, wk, wv, cos_c, sin_c, batch, seq):
    nrow = seq // CMP_STRIDE
    xk = kc.reshape(batch, nrow, CMP_STRIDE * KV_W)
    xv = vc.reshape(batch, nrow, CMP_STRIDE * KV_W)
    wek, w1k, pk, w2k = wk
    wev, w1v, pv, w2v = wv
    full = lambda a: pl.BlockSpec(a.shape, lambda b: (0,) * a.ndim)
    xspec = pl.BlockSpec((1, nrow, CMP_STRIDE * KV_W), lambda b: (b, 0, 0))
    ospec = pl.BlockSpec((1, nrow, KV_W), lambda b: (b, 0, 0))
    return pl.pallas_call(
        _cmp_kernel,
        grid=(batch,),
        in_specs=[xspec, xspec, full(wek), full(wev), full(w1k), full(w1v), full(pk), full(pv),
                  full(w2k), full(w2v), full(cos_c), full(sin_c)],
        out_specs=[ospec, ospec],
        out_shape=[jax.ShapeDtypeStruct((batch, nrow, KV_W), F32)] * 2,
        compiler_params=_cparams(("parallel",)),
        name="compress",
    )(xk, xv, wek, wev, w1k, w1v, pk, pv, w2k, w2v, cos_c, sin_c)


def _unpad_heads(acc):
    qb = SEL_BLOCK
    lane = lax.broadcasted_iota(jnp.int32, (qb, LANES), 1)
    low = lane < HEAD_DIM
    outs = []
    for j in range(NSA_HEADS // 2):
        a = acc[(2 * j) * qb:(2 * j + 1) * qb]
        b = acc[(2 * j + 1) * qb:(2 * j + 2) * qb]
        if (2 * j) // NSA_HPG == 0:
            outs.append(jnp.where(low, a, pltpu.roll(b, HEAD_DIM, 1)))
        else:
            outs.append(jnp.where(low, pltpu.roll(a, HEAD_DIM, 1), b))
    return jnp.concatenate(outs, axis=1)


def _nsa_kernel(q_ref, g_ref, kc_ref, vc_ref, ksl_ref, vsl_ref, kwn_ref, vwn_ref,
                ktab_ref, kdiag_ref, kwtab_ref, qc_ref, qw_ref, eye_ref, impt_ref, o_ref,
                imp_scr, m_scr, acc_scr, win_scr, kd_scr):
    qb = SEL_BLOCK
    rows = NSA_HEADS * qb
    i = pl.program_id(1)
    t0 = i * qb
    q = jnp.concatenate([q_ref[:, h * LANES:(h + 1) * LANES] for h in range(NSA_HEADS)], axis=0)
    qbf = q.astype(BF16)
    t = t0 + lax.broadcasted_iota(jnp.int32, (rows, 1), 0) % qb
    ones = jnp.ones((SEL_CH, LANES), BF16)

    ncmp = kc_ref.shape[1]
    s = _dot_nt(q, kc_ref[0], HI)
    cend = lax.broadcasted_iota(jnp.int32, (1, ncmp), 1) * CMP_STRIDE + (CMP_LEN - 1)
    vis = cend <= t
    sm = jnp.where(vis, s, NEG_BIG)
    mx = jnp.max(sm, axis=1, keepdims=True)
    e = jnp.where(vis, jnp.exp(sm - mx), 0.0)
    d = jnp.sum(e, axis=1, keepdims=True)
    p = e / jnp.where(d > 0, d, 1.0)
    o_cmp = _dot(p.astype(BF16), vc_ref[0].astype(BF16))
    psum = jnp.concatenate(
        [sum(p[(g * NSA_HPG + h) * qb:(g * NSA_HPG + h + 1) * qb] for h in range(NSA_HPG))
         for g in range(NSA_KV_GROUPS)], axis=0)
    imp_scr[...] = _dot_nt(impt_ref[...], psum, HI)

    nsbp = imp_scr.shape[0]
    imp = imp_scr[...]
    n_id = lax.broadcasted_iota(jnp.int32, (nsbp, 1), 0)
    valid = n_id <= i
    forced = (n_id == 0) | (n_id == i) | (n_id == i - 1)
    nforced = 1 + (i >= 1).astype(jnp.int32) + (i >= 2).astype(jnp.int32)

    def rank_body(m, cnt):
        rowm = imp_scr[pl.ds(m, 1), :]
        beats = (rowm > imp) | ((rowm == imp) & (m < n_id))
        return cnt + beats.astype(F32)

    cnt = lax.fori_loop(1, jnp.maximum(i - 1, 1), rank_body, jnp.zeros(imp.shape, F32))
    budget = (SEL_TOPK - nforced).astype(F32)
    sel = valid & (forced | (cnt < budget))
    notsel_t = jnp.where(sel, 0.0, 1.0)
    notsel = _dot_tn(notsel_t, eye_ref[...])
    notsel = jnp.concatenate([notsel[g * qb:(g + 1) * qb]
                              for g in range(NSA_KV_GROUPS) for _ in range(NSA_HPG)], axis=0)

    q_slc = jnp.concatenate([qbf, (notsel + qc_ref[...]).astype(BF16)], axis=1)
    m_scr[...] = jnp.full(m_scr.shape, NEG_BIG, F32)
    acc_scr[...] = jnp.zeros(acc_scr.shape, F32)

    def slc_chunk(k0, kmask):
        k = jnp.concatenate([ksl_ref[0, pl.ds(k0, SEL_CH), :], kmask], axis=1)
        v = jnp.concatenate([vsl_ref[0, pl.ds(k0, SEL_CH), :], ones], axis=1)
        for part in range(NSA_ROW_PARTS):
            rs = slice(part * rows // NSA_ROW_PARTS, (part + 1) * rows // NSA_ROW_PARTS)
            sc = _dot_nt(q_slc[rs], k)
            m_old = m_scr[rs]
            m_new = jnp.maximum(m_old, jnp.max(sc, axis=1, keepdims=True))
            alpha = jnp.exp(m_old - m_new)
            pr = jnp.exp(sc - m_new).astype(BF16)
            acc_scr[rs] = alpha * acc_scr[rs] + _dot(pr, v)
            m_scr[rs] = m_new

    def slc_body(c, carry):
        k0 = pl.multiple_of(c * SEL_CH, SEL_CH)
        slc_chunk(k0, ktab_ref[pl.ds(k0, SEL_CH), :])
        return carry

    nfull = i // (SEL_CH // qb)
    lax.fori_loop(0, nfull, slc_body, 0)
    k0 = pl.multiple_of(nfull * SEL_CH, SEL_CH)
    off = pl.multiple_of((i % (SEL_CH // qb)) * qb, qb)
    kd_scr[...] = ktab_ref[pl.ds(k0, SEL_CH), :]
    kd_scr[pl.ds(off, qb), :] = kd_scr[pl.ds(off, qb), :] + kdiag_ref[...]
    slc_chunk(k0, kd_scr[...])
    o_slc = acc_scr[:, 0:LANES] / acc_scr[:, LANES:2 * LANES]

    nblk_w = WIN_KEYS // qb

    @pl.when(i >= nblk_w - 1)
    def _():
        w0 = pl.multiple_of(t0 + qb - WIN_KEYS, qb)
        kw = jnp.concatenate([kwn_ref[0, pl.ds(w0, WIN_KEYS), :], kwtab_ref[...]], axis=1)
        vw = jnp.concatenate([vwn_ref[0, pl.ds(w0, WIN_KEYS), :],
                              jnp.ones((WIN_KEYS, LANES), BF16)], axis=1)
        sw = _dot_nt(jnp.concatenate([qbf, qw_ref[...]], axis=1), kw)
        ew = jnp.exp(sw - jnp.max(sw, axis=1, keepdims=True)).astype(BF16)
        ow = _dot(ew, vw)
        win_scr[...] = ow[:, 0:LANES] / ow[:, LANES:2 * LANES]

    @pl.when(i < nblk_w - 1)
    def _():
        kw = kwn_ref[0, 0:WIN_KEYS, :]
        vw = vwn_ref[0, 0:WIN_KEYS, :]
        sw = _dot_nt(qbf, kw)
        kpos = lax.broadcasted_iota(jnp.int32, (1, WIN_KEYS), 1)
        okw = (kpos <= t) & (kpos > t - WINDOW)
        swm = jnp.where(okw, sw, NEG_BIG)
        ew = jnp.where(okw, jnp.exp(swm - jnp.max(swm, axis=1, keepdims=True)), 0.0)
        dw = jnp.sum(ew, axis=1, keepdims=True)
        win_scr[...] = _dot(ew.astype(BF16), vw) / dw

    gate = jax.nn.sigmoid(g_ref[...])
    o_ref[...] = (gate[:, 0:NSA_W] * _unpad_heads(o_cmp)
                  + gate[:, NSA_W:2 * NSA_W] * _unpad_heads(o_slc)
                  + gate[:, 2 * NSA_W:3 * NSA_W] * _unpad_heads(win_scr[...]))


def _nsa(qpad, gexp, kcmp, vcmp, ksl, vsl, kwn, vwn, consts, batch, seq):
    qb = SEL_BLOCK
    nq = seq // qb
    rows = NSA_HEADS * qb
    kv3 = lambda a: a.reshape(batch, seq, KV_W)
    per_b = lambda n: pl.BlockSpec((1, n, KV_W), lambda b, i: (b, 0, 0))
    full = lambda a: pl.BlockSpec(a.shape, lambda b, i: (0,) * a.ndim)
    return pl.pallas_call(
        _nsa_kernel,
        grid=(batch, nq),
        in_specs=[
            pl.BlockSpec((qb, QPAD_W), lambda b, i: (b * nq + i, 0)),
            pl.BlockSpec((qb, GEXP_W), lambda b, i: (b * nq + i, 0)),
            per_b(kcmp.shape[1]), per_b(vcmp.shape[1]),
            per_b(seq), per_b(seq), per_b(seq), per_b(seq),
        ] + [full(c) for c in consts],
        out_specs=pl.BlockSpec((qb, NSA_W), lambda b, i: (b * nq + i, 0)),
        out_shape=jax.ShapeDtypeStruct((batch * seq, NSA_W), F32),
        scratch_shapes=[
            pltpu.VMEM((consts[-1].shape[0], NSA_KV_GROUPS * qb), F32),
            pltpu.VMEM((rows, 1), F32),
            pltpu.VMEM((rows, 2 * LANES), F32),
            pltpu.VMEM((rows, LANES), F32),
            pltpu.VMEM((SEL_CH, LANES), BF16),
        ],
        compiler_params=_cparams(("parallel", "arbitrary")),
        name="nsa",
    )(qpad, gexp, kcmp, vcmp, kv3(ksl), kv3(vsl), kv3(kwn), kv3(vwn), *consts)


def _nsa_constants(seq):
    qb = SEL_BLOCK
    nc_pad = seq // CMP_STRIDE
    nsb = seq // qb
    assert nsb <= qb
    nsb_pad = max(nsb, 8)
    cs = np.arange(nc_pad) * CMP_STRIDE
    ce = cs + CMP_LEN
    bs = np.arange(nsb_pad) * qb
    be = bs + qb
    ov = np.clip(np.minimum(ce[None, :], be[:, None]) - np.maximum(cs[None, :], bs[:, None]), 0, None)
    impt = ov / CMP_LEN
    pos = np.arange(seq)
    lane = np.arange(LANES)[None, :]
    ktab = np.where(lane == pos[:, None] // qb, NEG_BIG, 0.0)
    kdiag = np.where(lane == qb + np.arange(qb)[:, None], NEG_BIG, 0.0)
    x = np.arange(WIN_KEYS)[:, None]
    kb, j = x // qb, x % qb
    last = WIN_KEYS // qb - 1
    kwtab = np.where(((kb == 0) & (lane == 0)) | ((kb == 1) & (lane == j)) | ((kb == last) & (lane == qb + j)),
                     NEG_BIG, 0.0)
    r = (np.arange(NSA_HEADS * qb) % qb)[:, None]
    upper = (lane >= qb) & (lane - qb > r)
    qc = np.where(upper, 1.0, 0.0)
    qw = np.where(upper | ((lane < qb) & (lane <= r)), 1.0, 0.0)
    eye = np.eye(nsb_pad, LANES)
    return (jnp.asarray(ktab, BF16), jnp.asarray(kdiag, BF16), jnp.asarray(kwtab, BF16),
            jnp.asarray(qc, F32), jnp.asarray(qw, BF16),
            jnp.asarray(eye, F32), jnp.asarray(impt, F32))


def _gla_kernel(q_ref, k_ref, v_ref, a_ref, r_ref, wup_ref, bup_ref, ng_ref, o_ref, st_ref):
    blk = q_ref.shape[0]
    nsub = blk // GLA_SUB

    @pl.when(pl.program_id(1) == 0)
    def _():
        st_ref[...] = jnp.zeros(st_ref.shape, F32)

    x = _dot(a_ref[...], wup_ref[...], HI) + bup_ref[...]
    g = (jnp.minimum(x, 0.0) - jnp.log1p(jnp.exp(-jnp.abs(x)))) / GLA_TAU
    ri = lax.broadcasted_iota(jnp.int32, (blk, blk), 0)
    ci = lax.broadcasted_iota(jnp.int32, (blk, blk), 1)
    b = _dot(jnp.where(ci <= ri, 1.0, 0.0), g, HI)
    blast = b[blk - 1:blk, :]
    q = q_ref[...] * (GLA_DK ** -0.5)
    k = k_ref[...]
    v = v_ref[...]
    vb = v.astype(BF16)

    st = st_ref[...]
    o = _dot_nt((q * jnp.exp(b)).astype(BF16), st.astype(BF16))
    khat = k * jnp.exp(blast - b)
    srow = lax.broadcasted_iota(jnp.int32, st.shape, 0) // GLA_DV
    scol = lax.broadcasted_iota(jnp.int32, st.shape, 1) // GLA_DK
    st_ref[...] = st * jnp.exp(blast) + jnp.where(srow == scol, _dot_tn(v, khat), 0.0)

    krow = lax.broadcasted_iota(jnp.int32, (blk, 1), 0)
    lane_qk = lax.broadcasted_iota(jnp.int32, (1, GLA_QK_W), 1) // GLA_DK
    lane_v = lax.broadcasted_iota(jnp.int32, (1, GLA_W), 1) // GLA_DV
    qrow = lax.broadcasted_iota(jnp.int32, (GLA_HEADS * GLA_SUB, 1), 0) % GLA_SUB
    kcol = lax.broadcasted_iota(jnp.int32, (1, blk), 1)
    for c in range(nsub):
        lo = c * GLA_SUB
        ref_b = b[lo:lo + 1, :]
        qt = q[lo:lo + GLA_SUB] * jnp.exp(b[lo:lo + GLA_SUB] - ref_b)
        kt = k * jnp.exp(jnp.where(krow < lo + GLA_SUB, ref_b - b, 0.0))
        qs = jnp.concatenate([jnp.where(lane_qk == h, qt, 0.0) for h in range(GLA_HEADS)], axis=0)
        a = _dot_nt(qs.astype(BF16), kt.astype(BF16))
        a = jnp.where(kcol <= lo + qrow, a, 0.0)
        r = _dot(a.astype(BF16), vb)
        oi = sum(jnp.where(lane_v == h, r[h * GLA_SUB:(h + 1) * GLA_SUB], 0.0) for h in range(GLA_HEADS))
        o_ref[lo:lo + GLA_SUB, :] = o[lo:lo + GLA_SUB] + oi

    o = o_ref[...]
    gi = lax.broadcasted_iota(jnp.int32, (GLA_W, GLA_W), 0) // GLA_DV
    gj = lax.broadcasted_iota(jnp.int32, (GLA_W, GLA_W), 1) // GLA_DV
    ms = _dot(o * o, jnp.where(gi == gj, 1.0 / GLA_DV, 0.0), HI)
    rr = r_ref[...]
    o_ref[...] = o * lax.rsqrt(ms + NORM_EPS) * ng_ref[...] * (rr * jax.nn.sigmoid(rr))


def _gla(gq, gk, gv, ga, gr, wup, bup, ng, batch, seq):
    blk = min(GLA_BLK, seq)
    nb = seq // blk
    tok = lambda w: pl.BlockSpec((blk, w), lambda b, j: (b * nb + j, 0))
    full = lambda a: pl.BlockSpec(a.shape, lambda b, j: (0,) * a.ndim)
    return pl.pallas_call(
        _gla_kernel,
        grid=(batch, nb),
        in_specs=[tok(GLA_QK_W), tok(GLA_QK_W), tok(GLA_W), tok(LANES), tok(GLA_W),
                  full(wup), full(bup), full(ng)],
        out_specs=tok(GLA_W),
        out_shape=jax.ShapeDtypeStruct((batch * seq, GLA_W), F32),
        scratch_shapes=[pltpu.VMEM((GLA_W, GLA_QK_W), F32)],
        compiler_params=_cparams(("parallel", "arbitrary")),
        name="gla",
    )(gq, gk, gv, ga, gr, wup, bup, ng)


def _sg_kernel(uv_ref, lg_ref, lb_ref, w_ref, bias_ref, o_ref):
    a = _gelu_tanh(uv_ref[...])
    u = a[:, :SG_W]
    v = a[:, SG_W:]
    mu = jnp.mean(v, axis=-1, keepdims=True)
    var = jnp.mean(jnp.square(v - mu), axis=-1, keepdims=True)
    vn = ((v - mu) * lax.rsqrt(var + NORM_EPS) * lg_ref[...] + lb_ref[...]).astype(BF16)
    n = w_ref.shape[1]
    ri = lax.broadcasted_iota(jnp.int32, (n, n), 0)
    ci = lax.broadcasted_iota(jnp.int32, (n, n), 1)
    lane_g = lax.broadcasted_iota(jnp.int32, (1, SG_W), 1) // SG_CH
    s = bias_ref[...]
    for g in range(SG_GROUPS):
        wg = jnp.where(ci <= ri, w_ref[g], 0.0).astype(BF16)
        s = s + jnp.where(lane_g == g, _dot(wg, vn), 0.0)
    o_ref[...] = u * s


def _spatial_gating(uv, ln_g, ln_b, w_s, bias_exp):
    T = uv.shape[0]
    full = lambda a: pl.BlockSpec(a.shape, lambda i: (0,) * a.ndim)
    return pl.pallas_call(
        _sg_kernel,
        grid=(T // SG_CHUNK,),
        in_specs=[pl.BlockSpec((SG_CHUNK, 2 * SG_W), lambda i: (i, 0)),
                  full(ln_g), full(ln_b), full(w_s), full(bias_exp)],
        out_specs=pl.BlockSpec((SG_CHUNK, SG_W), lambda i: (i, 0)),
        out_shape=jax.ShapeDtypeStruct((T, SG_W), F32),
        compiler_params=_cparams(("parallel",)),
        name="spatial_gating",
    )(uv, ln_g, ln_b, w_s, bias_exp)


def _outffn_kernel(x_ref, on_ref, og_ref, os_ref, wo_ref, fg_ref, wg_ref, wu_ref, wd_ref,
                   fin_ref, o_ref, *, final):
    x = x_ref[...]
    x = x + _dot(on_ref[...].astype(BF16), wo_ref[0:NSA_W, :])
    x = x + _dot(og_ref[...].astype(BF16), wo_ref[NSA_W:NSA_W + GLA_W, :])
    x = x + _dot(os_ref[...].astype(BF16), wo_ref[NSA_W + GLA_W:, :])
    hn = (x * lax.rsqrt(jnp.mean(x * x, axis=-1, keepdims=True) + NORM_EPS) * fg_ref[...]).astype(BF16)
    gt = _dot(hn, wg_ref[...])
    up = _dot(hn, wu_ref[...])
    y = x + _dot((gt * jax.nn.sigmoid(gt) * up).astype(BF16), wd_ref[...])
    if final:
        y = y * lax.rsqrt(jnp.mean(y * y, axis=-1, keepdims=True) + NORM_EPS) * fin_ref[...]
    o_ref[...] = y


def _outffn(x2, o_nsa, o_gla, o_sg, wo, fgain, wg, wu, wd, fin, tm, final):
    T, D = x2.shape
    row = lambda w: pl.BlockSpec((tm, w), lambda i: (i, 0))
    const = lambda a: pl.BlockSpec(a.shape, lambda i: (0,) * a.ndim)
    return pl.pallas_call(
        functools.partial(_outffn_kernel, final=final),
        grid=(T // tm,),
        in_specs=[row(D), row(NSA_W), row(GLA_W), row(SG_W), const(wo), const(fgain),
                  const(wg), const(wu), const(wd), const(fin)],
        out_specs=row(D),
        out_shape=jax.ShapeDtypeStruct((T, D), F32),
        compiler_params=_cparams(("parallel",)),
        name="outproj_ffn",
    )(x2, o_nsa, o_gla, o_sg, wo, fgain, wg, wu, wd, fin)


def kernel(x, attn_norm, w_in, cmp_pos_k, cmp_w1_k, cmp_w2_k, cmp_pos_v, cmp_w1_v, cmp_w2_v,
           gla_w_up, gla_b_up, gla_norm, sg_ln_g, sg_ln_b, sg_w, sg_b, w_out,
           ffn_norm, w_gate, w_up, w_down, final_norm):
    batch, seq, d_model = x.shape
    depth = w_in.shape[0]
    T = batch * seq
    assert seq % GLA_BLK == 0 and seq >= WIN_KEYS and seq % SEL_CH == 0
    tm_in = 256
    tm_ffn = 256

    cos_t, sin_t = _rope_tables(jnp.arange(seq))
    nrow = seq // CMP_STRIDE
    cos_c, sin_c = _rope_tables(jnp.arange(nrow) * CMP_STRIDE + CMP_LEN - 1)
    nsa_consts = _nsa_constants(seq)

    x2 = x.reshape(T, d_model)
    for l in range(depth):
        (qpad, kc, vc, ksl, vsl, kwn, vwn, gexp, gq, gk, gv, ga, gr, uv) = _inproj(
            x2, attn_norm[l][None, :], _prep_w_in(w_in[l]), cos_t, sin_t, seq, tm_in)
        kcmp, vcmp = _compress(
            kc, vc, _prep_cmp_weights(cmp_w1_k[l], cmp_w2_k[l], cmp_pos_k[l]),
            _prep_cmp_weights(cmp_w1_v[l], cmp_w2_v[l], cmp_pos_v[l]), cos_c, sin_c, batch, seq)
        o_nsa = _nsa(qpad, gexp, kcmp, vcmp, ksl, vsl, kwn, vwn, nsa_consts, batch, seq)
        wup = jnp.pad(gla_w_up[l], ((0, LANES - GLA_RANK), (0, 0)))
        o_gla = _gla(gq, gk, gv, ga, gr, wup, gla_b_up[l][None, :],
                     jnp.tile(gla_norm[l], GLA_HEADS)[None, :], batch, seq)
        bias_exp = jnp.repeat(sg_b[l].T, SG_CH, axis=1)
        o_sg = _spatial_gating(uv, sg_ln_g[l][None, :], sg_ln_b[l][None, :], sg_w[l], bias_exp)
        x2 = _outffn(x2, o_nsa, o_gla, o_sg, w_out[l].astype(BF16), ffn_norm[l][None, :],
                     w_gate[l].astype(BF16), w_up[l].astype(BF16), w_down[l].astype(BF16),
                     final_norm[None, :], tm_ffn, final=(l == depth - 1))
    return x2.reshape(batch, seq, d_model)
```

```python
import functools
import math

import numpy as np
import jax
import jax.numpy as jnp
from jax import lax
from jax.experimental import pallas as pl
from jax.experimental.pallas import tpu as pltpu

HEAD_DIM = 64
NSA_HEADS = 8
NSA_KV_GROUPS = 2
NSA_HPG = NSA_HEADS // NSA_KV_GROUPS
CMP_LEN = 32
CMP_STRIDE = 16
CMP_HIDDEN = 256
SEL_BLOCK = 64
SEL_TOPK = 16
WINDOW = 512
GLA_HEADS = 4
GLA_DK = 32
GLA_DV = 64
GLA_RANK = 16
GLA_TAU = 16.0
SG_GROUPS = 4
SG_CH = 64
SG_CHUNK = 128
NSA_W = NSA_HEADS * HEAD_DIM
GLA_W = GLA_HEADS * GLA_DV
SG_W = SG_GROUPS * SG_CH
KV_W = NSA_KV_GROUPS * HEAD_DIM
IN_SIZES = (NSA_W, KV_W, KV_W, KV_W, KV_W, KV_W, KV_W, NSA_HEADS * 3,
            GLA_HEADS * GLA_DK, GLA_HEADS * GLA_DK, GLA_W, GLA_RANK, GLA_W, 2 * SG_W)
ROPE_THETA = 10000.0
NORM_EPS = 1e-6

LANES = 128
NEG_BIG = -1e30
VMEM_LIMIT = 56 * 1024 * 1024

QPAD_W = NSA_HEADS * LANES
GEXP_W = 3 * NSA_W
GLA_QK_W = GLA_HEADS * GLA_DK
GLA_BLK = 256
GLA_SUB = 16
SEL_CH = 1024
WIN_KEYS = WINDOW + 2 * SEL_BLOCK
NSA_ROW_PARTS = 2

HI = lax.Precision.HIGHEST
F32 = jnp.float32
BF16 = jnp.bfloat16


def _cparams(sem):
    return pltpu.CompilerParams(dimension_semantics=sem, vmem_limit_bytes=VMEM_LIMIT)


def _dot(a, b, precision=None):
    return jnp.dot(a, b, preferred_element_type=F32, precision=precision)


def _dot_nt(a, b, precision=None):
    return lax.dot_general(a, b, (((1,), (1,)), ((), ())),
                           preferred_element_type=F32, precision=precision)


def _dot_tn(a, b, precision=None):
    return lax.dot_general(a, b, (((0,), (0,)), ((), ())),
                           preferred_element_type=F32, precision=precision)


def _gelu_tanh(x):
    c = math.sqrt(2.0 / math.pi)
    return 0.5 * x * (1.0 + jnp.tanh(c * (x + 0.044715 * (x * x * x))))


def _rope_lanes(x, cos, sin_signed):
    n = x.shape[-1]
    lane = lax.broadcasted_iota(jnp.int32, x.shape, 1)
    first_half = (lane % HEAD_DIM) < (HEAD_DIM // 2)
    partner = jnp.where(first_half,
                        pltpu.roll(x, n - HEAD_DIM // 2, 1),
                        pltpu.roll(x, HEAD_DIM // 2, 1))
    return x * cos + partner * sin_signed


_INPROJ_OUT = (
    ("qpad", QPAD_W, BF16), ("kc", KV_W, F32), ("vc", KV_W, F32),
    ("ksl", KV_W, BF16), ("vsl", KV_W, BF16), ("kwn", KV_W, BF16), ("vwn", KV_W, BF16),
    ("gexp", GEXP_W, BF16), ("gq", GLA_QK_W, F32), ("gk", GLA_QK_W, F32),
    ("gv", GLA_W, F32), ("ga", LANES, F32), ("gr", GLA_W, F32), ("uv", 2 * SG_W, F32))
_INPROJ_W = sum(w for _, w, _ in _INPROJ_OUT)


def _inproj_kernel(x_ref, g_ref, w_ref, cos_ref, sin_ref, *out_refs):
    x = x_ref[...]
    hn = x * lax.rsqrt(jnp.mean(x * x, axis=-1, keepdims=True) + NORM_EPS) * g_ref[...]
    z = _dot(hn.astype(BF16), w_ref[...])
    cos = cos_ref[...]
    sin = sin_ref[...]
    off = 0
    for (name, width, dtype), o_ref in zip(_INPROJ_OUT, out_refs):
        if name in ("qpad", "ksl", "kwn"):
            for j in range(width // LANES):
                blk = _rope_lanes(z[:, off + j * LANES: off + (j + 1) * LANES], cos, sin)
                if name == "qpad":
                    blk = blk * (HEAD_DIM ** -0.5)
                o_ref[:, j * LANES:(j + 1) * LANES] = blk.astype(dtype)
        else:
            o_ref[...] = z[:, off:off + width].astype(dtype)
        off += width


def _inproj(x2, gain, w_p, cos_t, sin_t, seq, tm):
    T, D = x2.shape
    nper = seq // tm
    out_shape = [jax.ShapeDtypeStruct((T, w), dt) for _, w, dt in _INPROJ_OUT]
    out_specs = [pl.BlockSpec((tm, w), lambda i: (i, 0)) for _, w, _ in _INPROJ_OUT]
    return pl.pallas_call(
        _inproj_kernel,
        grid=(T // tm,),
        in_specs=[
            pl.BlockSpec((tm, D), lambda i: (i, 0)),
            pl.BlockSpec((1, D), lambda i: (0, 0)),
            pl.BlockSpec((D, _INPROJ_W), lambda i: (0, 0)),
            pl.BlockSpec((tm, LANES), lambda i: (i % nper, 0)),
            pl.BlockSpec((tm, LANES), lambda i: (i % nper, 0)),
        ],
        out_specs=out_specs,
        out_shape=out_shape,
        compiler_params=_cparams(("parallel",)),
        name="inproj",
    )(x2, gain, w_p, cos_t, sin_t)


def _prep_w_in(w_in):
    offs = np.cumsum((0,) + IN_SIZES)
    seg = [w_in[:, offs[k]:offs[k + 1]] for k in range(len(IN_SIZES))]
    (wq, wkc, wvc, wksl, wvsl, wkwn, wvwn, wg, wgq, wgk, wgv, wga, wgr, wuv) = seg
    D = w_in.shape[0]
    zeros64 = jnp.zeros((D, HEAD_DIM), w_in.dtype)
    qcols = []
    for h in range(NSA_HEADS):
        wh = wq[:, h * HEAD_DIM:(h + 1) * HEAD_DIM]
        qcols += [wh, zeros64] if h // NSA_HPG == 0 else [zeros64, wh]
    gcols = []
    for k in range(3):
        for h in range(NSA_HEADS):
            gcols.append(jnp.broadcast_to(wg[:, h * 3 + k][:, None], (D, HEAD_DIM)))
    wga_p = jnp.pad(wga, ((0, 0), (0, LANES - GLA_RANK)))
    return jnp.concatenate(qcols + [wkc, wvc, wksl, wvsl, wkwn, wvwn] + gcols
                           + [wgq, wgk, wgv, wga_p, wgr, wuv], axis=1).astype(BF16)


def _rope_tables(pos):
    half = HEAD_DIM // 2
    inv = 1.0 / (ROPE_THETA ** (jnp.arange(half, dtype=F32) / half))
    ang = pos.astype(F32)[:, None] * inv[None, :]
    cos = jnp.cos(ang)
    sin = jnp.sin(ang)
    cos_t = jnp.concatenate([cos, cos, cos, cos], axis=1)
    sin_t = jnp.concatenate([-sin, sin, -sin, sin], axis=1)
    return cos_t, sin_t


def _cmp_kernel(xk_ref, xv_ref, wek_ref, wev_ref, w1k_ref, w1v_ref, pk_ref, pv_ref,
                w2k_ref, w2v_ref, cos_ref, sin_ref, ok_ref, ov_ref):
    nrow = xk_ref.shape[1]
    row = lax.broadcasted_iota(jnp.int32, (nrow, LANES), 0)

    def compress(x_ref, we_ref, w1_ref, p_ref, w2_ref):
        h = _dot(x_ref[0].astype(BF16), we_ref[...])
        posb = _dot(p_ref[...].astype(BF16), w1_ref[...])[0:1, :]
        y = jnp.zeros((nrow, LANES), F32)
        for g in range(NSA_KV_GROUPS):
            a = h[:, g * CMP_HIDDEN:(g + 1) * CMP_HIDDEN]
            b = h[:, (NSA_KV_GROUPS + g) * CMP_HIDDEN:(NSA_KV_GROUPS + g + 1) * CMP_HIDDEN]
            hid = a + pltpu.roll(b, nrow - 1, 0) + posb
            y = y + _dot(_gelu_tanh(hid).astype(BF16), w2_ref[g])
        return y

    yk = _rope_lanes(compress(xk_ref, wek_ref, w1k_ref, pk_ref, w2k_ref), cos_ref[...], sin_ref[...])
    yv = compress(xv_ref, wev_ref, w1v_ref, pv_ref, w2v_ref)
    keep = row < nrow - 1
    ok_ref[0] = jnp.where(keep, yk, 0.0)
    ov_ref[0] = jnp.where(keep, yv, 0.0)


def _prep_cmp_weights(w1, w2, pos):
    half = CMP_LEN // CMP_STRIDE
    H = w1.shape[1]
    w1r = w1.reshape(half, CMP_STRIDE, HEAD_DIM, H)
    cols = []
    for a in range(half):
        for g in range(NSA_KV_GROUPS):
            blk = jnp.zeros((CMP_STRIDE, NSA_KV_GROUPS, HEAD_DIM, H), w1.dtype)
            blk = blk.at[:, g].set(w1r[a])
            cols.append(blk.reshape(CMP_STRIDE * KV_W, H))
    wexp = jnp.concatenate(cols, axis=1).astype(BF16)
    w2p = jnp.stack([jnp.pad(w2, ((0, 0), (g * HEAD_DIM, KV_W - (g + 1) * HEAD_DIM)))
                     for g in range(NSA_KV_GROUPS)]).astype(BF16)
    posf = jnp.pad(pos.reshape(1, CMP_LEN * HEAD_DIM), ((0, 7), (0, 0)))
    return wexp, w1.astype(BF16), posf, w2p


def _compress(kc, vc, wk, wv, cos_c, sin_c, batch, seq):
    nrow = seq // CMP_STRIDE
    xk = kc.reshape(batch, nrow, CMP_STRIDE * KV_W)
    xv = vc.reshape(batch, nrow, CMP_STRIDE * KV_W)
    wek, w1k, pk, w2k = wk
    wev, w1v, pv, w2v = wv
    full = lambda a: pl.BlockSpec(a.shape, lambda b: (0,) * a.ndim)
    xspec = pl.BlockSpec((1, nrow, CMP_STRIDE * KV_W), lambda b: (b, 0, 0))
    ospec = pl.BlockSpec((1, nrow, KV_W), lambda b: (b, 0, 0))
    return pl.pallas_call(
        _cmp_kernel,
        grid=(batch,),
        in_specs=[xspec, xspec, full(wek), full(wev), full(w1k), full(w1v), full(pk), full(pv),
                  full(w2k), full(w2v), full(cos_c), full(sin_c)],
        out_specs=[ospec, ospec],
        out_shape=[jax.ShapeDtypeStruct((batch, nrow, KV_W), F32)] * 2,
        compiler_params=_cparams(("parallel",)),
        name="compress",
    )(xk, xv, wek, wev, w1k, w1v, pk, pv, w2k, w2v, cos_c, sin_c)


def _unpad_heads(acc):
    qb = SEL_BLOCK
    lane = lax.broadcasted_iota(jnp.int32, (qb, LANES), 1)
    low = lane < HEAD_DIM
    outs = []
    for j in range(NSA_HEADS // 2):
        a = acc[(2 * j) * qb:(2 * j + 1) * qb]
        b = acc[(2 * j + 1) * qb:(2 * j + 2) * qb]
        if (2 * j) // NSA_HPG == 0:
            outs.append(jnp.where(low, a, pltpu.roll(b, HEAD_DIM, 1)))
        else:
            outs.append(jnp.where(low, pltpu.roll(a, HEAD_DIM, 1), b))
    return jnp.concatenate(outs, axis=1)


def _nsa_kernel(q_ref, g_ref, kc_ref, vc_ref, ksl_ref, vsl_ref, kwn_ref, vwn_ref,
                ktab_ref, kdiag_ref, kwtab_ref, qc_ref, qw_ref, eye_ref, impt_ref, o_ref,
                imp_scr, m_scr, acc_scr, win_scr, kd_scr, qs_scr):
    qb = SEL_BLOCK
    rows = NSA_HEADS * qb
    i = pl.program_id(1)
    t0 = i * qb
    qbf = jnp.concatenate([q_ref[:, h * LANES:(h + 1) * LANES] for h in range(NSA_HEADS)], axis=0)
    t = t0 + lax.broadcasted_iota(jnp.int32, (rows, 1), 0) % qb
    ones = jnp.ones((SEL_CH, LANES), BF16)

    nblk_w = WIN_KEYS // qb
    ib = jnp.minimum(i, nblk_w - 1)
    w0 = pl.multiple_of((i - ib) * qb, qb)
    toff = pl.multiple_of((nblk_w - 1 - ib) * qb, qb)
    kw = jnp.concatenate([kwn_ref[0, pl.ds(w0, WIN_KEYS), :], kwtab_ref[pl.ds(toff, WIN_KEYS), :]], axis=1)
    vw = jnp.concatenate([vwn_ref[0, pl.ds(w0, WIN_KEYS), :], ones[0:WIN_KEYS]], axis=1)
    q_win = jnp.concatenate([qbf, qw_ref[...]], axis=1)
    for part in range(NSA_ROW_PARTS):
        rs = slice(part * rows // NSA_ROW_PARTS, (part + 1) * rows // NSA_ROW_PARTS)
        sw = _dot_nt(q_win[rs], kw)
        ew = jnp.exp(sw - jnp.max(sw, axis=1, keepdims=True)).astype(BF16)
        ow = _dot(ew, vw)
        win_scr[rs] = ow[:, 0:LANES] / ow[:, LANES:2 * LANES]

    ncmp = kc_ref.shape[1]
    s = _dot_nt(qbf, kc_ref[0].astype(BF16))
    cend = lax.broadcasted_iota(jnp.int32, (1, ncmp), 1) * CMP_STRIDE + (CMP_LEN - 1)
    vis = cend <= t
    sm = jnp.where(vis, s, NEG_BIG)
    mx = jnp.max(sm, axis=1, keepdims=True)
    e = jnp.where(vis, jnp.exp(sm - mx), 0.0)
    d = jnp.sum(e, axis=1, keepdims=True)
    p = e / jnp.where(d > 0, d, 1.0)
    o_cmp = _dot(p.astype(BF16), vc_ref[0].astype(BF16))
    psum = jnp.concatenate(
        [sum(p[(g * NSA_HPG + h) * qb:(g * NSA_HPG + h + 1) * qb] for h in range(NSA_HPG))
         for g in range(NSA_KV_GROUPS)], axis=0)
    imp_scr[...] = _dot_nt(impt_ref[...], psum, HI)

    nsbp = imp_scr.shape[0]
    imp = imp_scr[...]
    n_id = lax.broadcasted_iota(jnp.int32, (nsbp, 1), 0)
    valid = n_id <= i
    forced = (n_id == 0) | (n_id == i) | (n_id == i - 1)
    nforced = 1 + (i >= 1).astype(jnp.int32) + (i >= 2).astype(jnp.int32)

    def rank_body(m, cnt):
        rowm = imp_scr[pl.ds(m, 1), :]
        beats = (rowm > imp) | ((rowm == imp) & (m < n_id))
        return cnt + beats.astype(F32)

    cnt = lax.fori_loop(1, jnp.maximum(i - 1, 1), rank_body, jnp.zeros(imp.shape, F32))
    budget = (SEL_TOPK - nforced).astype(F32)
    sel = valid & (forced | (cnt < budget))
    notsel_t = jnp.where(sel, 0.0, 1.0)
    notsel = _dot_tn(notsel_t, eye_ref[...])
    notsel = jnp.concatenate([notsel[g * qb:(g + 1) * qb]
                              for g in range(NSA_KV_GROUPS) for _ in range(NSA_HPG)], axis=0)

    qs_scr[...] = jnp.concatenate([qbf, (notsel + qc_ref[...]).astype(BF16)], axis=1)
    m_scr[...] = jnp.full(m_scr.shape, NEG_BIG, F32)
    acc_scr[...] = jnp.zeros(acc_scr.shape, F32)
    last = i // (SEL_CH // qb)
    k0_last = pl.multiple_of(last * SEL_CH, SEL_CH)
    off = pl.multiple_of((i % (SEL_CH // qb)) * qb, qb)
    kd_scr[...] = ktab_ref[pl.ds(k0_last, SEL_CH), :]
    kd_scr[pl.ds(off, qb), :] = kd_scr[pl.ds(off, qb), :] + kdiag_ref[...]
    parts = [slice(p * rows // NSA_ROW_PARTS, (p + 1) * rows // NSA_ROW_PARTS) for p in range(NSA_ROW_PARTS)]

    def slc_body(c, carry):
        k0 = pl.multiple_of(c * SEL_CH, SEL_CH)
        kmask = jnp.where(c == last, kd_scr[...], ktab_ref[pl.ds(k0, SEL_CH), :])
        k = jnp.concatenate([ksl_ref[0, pl.ds(k0, SEL_CH), :], kmask], axis=1)
        v = jnp.concatenate([vsl_ref[0, pl.ds(k0, SEL_CH), :], ones], axis=1)
        for rs in parts:
            sc = _dot_nt(qs_scr[rs], k)
            m_old = m_scr[rs]
            m_new = jnp.maximum(m_old, jnp.max(sc, axis=1, keepdims=True))
            alpha = jnp.exp(m_old - m_new)
            pr = jnp.exp(sc - m_new).astype(BF16)
            acc_scr[rs] = alpha * acc_scr[rs] + _dot(pr, v)
            m_scr[rs] = m_new
        return carry

    lax.fori_loop(0, last + 1, slc_body, 0)
    o_slc = acc_scr[:, 0:LANES] / acc_scr[:, LANES:2 * LANES]

    gate = jax.nn.sigmoid(g_ref[...].astype(F32))
    o_ref[...] = (gate[:, 0:NSA_W] * _unpad_heads(o_cmp)
                  + gate[:, NSA_W:2 * NSA_W] * _unpad_heads(o_slc)
                  + gate[:, 2 * NSA_W:3 * NSA_W] * _unpad_heads(win_scr[...]))


def _nsa(qpad, gexp, kcmp, vcmp, ksl, vsl, kwn, vwn, consts, batch, seq):
    qb = SEL_BLOCK
    nq = seq // qb
    rows = NSA_HEADS * qb
    kv3 = lambda a: a.reshape(batch, seq, KV_W)
    per_b = lambda n: pl.BlockSpec((1, n, KV_W), lambda b, i: (b, 0, 0))
    full = lambda a: pl.BlockSpec(a.shape, lambda b, i: (0,) * a.ndim)
    return pl.pallas_call(
        _nsa_kernel,
        grid=(batch, nq),
        in_specs=[
            pl.BlockSpec((qb, QPAD_W), lambda b, i: (b * nq + i, 0)),
            pl.BlockSpec((qb, GEXP_W), lambda b, i: (b * nq + i, 0)),
            per_b(kcmp.shape[1]), per_b(vcmp.shape[1]),
            per_b(seq), per_b(seq), per_b(seq), per_b(seq),
        ] + [full(c) for c in consts],
        out_specs=pl.BlockSpec((qb, NSA_W), lambda b, i: (b * nq + i, 0)),
        out_shape=jax.ShapeDtypeStruct((batch * seq, NSA_W), F32),
        scratch_shapes=[
            pltpu.VMEM((consts[-1].shape[0], NSA_KV_GROUPS * qb), F32),
            pltpu.VMEM((rows, 1), F32),
            pltpu.VMEM((rows, 2 * LANES), F32),
            pltpu.VMEM((rows, LANES), F32),
            pltpu.VMEM((SEL_CH, LANES), BF16),
            pltpu.VMEM((rows, 2 * LANES), BF16),
        ],
        compiler_params=_cparams(("parallel", "arbitrary")),
        name="nsa",
    )(qpad, gexp, kcmp, vcmp, kv3(ksl), kv3(vsl), kv3(kwn), kv3(vwn), *consts)


def _nsa_constants(seq):
    qb = SEL_BLOCK
    nc_pad = seq // CMP_STRIDE
    nsb = seq // qb
    assert nsb <= qb
    nsb_pad = max(nsb, 8)
    cs = np.arange(nc_pad) * CMP_STRIDE
    ce = cs + CMP_LEN
    bs = np.arange(nsb_pad) * qb
    be = bs + qb
    ov = np.clip(np.minimum(ce[None, :], be[:, None]) - np.maximum(cs[None, :], bs[:, None]), 0, None)
    impt = ov / CMP_LEN
    pos = np.arange(seq)
    lane = np.arange(LANES)[None, :]
    ktab = np.where(lane == pos[:, None] // qb, NEG_BIG, 0.0)
    kdiag = np.where(lane == qb + np.arange(qb)[:, None], NEG_BIG, 0.0)
    last = WIN_KEYS // qb - 1
    x = np.arange(WIN_KEYS + last * qb)[:, None]
    kb, j = x // qb, x % qb
    kwtab = np.where((((kb == 0) | (kb > last)) & (lane == 0)) | ((kb == 1) & (lane == j))
                     | ((kb == last) & (lane == qb + j)), NEG_BIG, 0.0)
    r = (np.arange(NSA_HEADS * qb) % qb)[:, None]
    upper = (lane >= qb) & (lane - qb > r)
    qc = np.where(upper, 1.0, 0.0)
    qw = np.where(upper | ((lane < qb) & (lane <= r)), 1.0, 0.0)
    eye = np.eye(nsb_pad, LANES)
    return (jnp.asarray(ktab, BF16), jnp.asarray(kdiag, BF16), jnp.asarray(kwtab, BF16),
            jnp.asarray(qc, F32), jnp.asarray(qw, BF16),
            jnp.asarray(eye, F32), jnp.asarray(impt, F32))


def _gla_kernel(q_ref, k_ref, v_ref, a_ref, r_ref, wup_ref, bup_ref, ng_ref, o_ref, st_ref):
    blk = q_ref.shape[0]
    nsub = blk // GLA_SUB

    @pl.when(pl.program_id(1) == 0)
    def _():
        st_ref[...] = jnp.zeros(st_ref.shape, F32)

    x = _dot(a_ref[...], wup_ref[...], HI) + bup_ref[...]
    g = (jnp.minimum(x, 0.0) - jnp.log1p(jnp.exp(-jnp.abs(x)))) / GLA_TAU
    ri = lax.broadcasted_iota(jnp.int32, (blk, blk), 0)
    ci = lax.broadcasted_iota(jnp.int32, (blk, blk), 1)
    b = _dot(jnp.where(ci <= ri, 1.0, 0.0), g, HI)
    blast = b[blk - 1:blk, :]
    q = q_ref[...] * (GLA_DK ** -0.5)
    k = k_ref[...]
    v = v_ref[...]
    vb = v.astype(BF16)

    st = st_ref[...]
    o = _dot_nt((q * jnp.exp(b)).astype(BF16), st.astype(BF16))
    khat = k * jnp.exp(blast - b)
    srow = lax.broadcasted_iota(jnp.int32, st.shape, 0) // GLA_DV
    scol = lax.broadcasted_iota(jnp.int32, st.shape, 1) // GLA_DK
    st_ref[...] = st * jnp.exp(blast) + jnp.where(srow == scol, _dot_tn(v, khat), 0.0)

    krow = lax.broadcasted_iota(jnp.int32, (blk, 1), 0)
    lane_qk = lax.broadcasted_iota(jnp.int32, (1, GLA_QK_W), 1) // GLA_DK
    lane_v = lax.broadcasted_iota(jnp.int32, (1, GLA_W), 1) // GLA_DV
    qrow = lax.broadcasted_iota(jnp.int32, (GLA_HEADS * GLA_SUB, 1), 0) % GLA_SUB
    kcol = lax.broadcasted_iota(jnp.int32, (1, blk), 1)
    for c in range(nsub):
        lo = c * GLA_SUB
        ref_b = b[lo:lo + 1, :]
        qt = q[lo:lo + GLA_SUB] * jnp.exp(b[lo:lo + GLA_SUB] - ref_b)
        kt = k * jnp.exp(jnp.where(krow < lo + GLA_SUB, ref_b - b, 0.0))
        qs = jnp.concatenate([jnp.where(lane_qk == h, qt, 0.0) for h in range(GLA_HEADS)], axis=0)
        a = _dot_nt(qs.astype(BF16), kt.astype(BF16))
        a = jnp.where(kcol <= lo + qrow, a, 0.0)
        r = _dot(a.astype(BF16), vb)
        oi = sum(jnp.where(lane_v == h, r[h * GLA_SUB:(h + 1) * GLA_SUB], 0.0) for h in range(GLA_HEADS))
        o_ref[lo:lo + GLA_SUB, :] = o[lo:lo + GLA_SUB] + oi

    o = o_ref[...]
    gi = lax.broadcasted_iota(jnp.int32, (GLA_W, GLA_W), 0) // GLA_DV
    gj = lax.broadcasted_iota(jnp.int32, (GLA_W, GLA_W), 1) // GLA_DV
    ms = _dot(o * o, jnp.where(gi == gj, 1.0 / GLA_DV, 0.0), HI)
    rr = r_ref[...]
    o_ref[...] = o * lax.rsqrt(ms + NORM_EPS) * ng_ref[...] * (rr * jax.nn.sigmoid(rr))


def _gla(gq, gk, gv, ga, gr, wup, bup, ng, batch, seq):
    blk = min(GLA_BLK, seq)
    nb = seq // blk
    tok = lambda w: pl.BlockSpec((blk, w), lambda b, j: (b * nb + j, 0))
    full = lambda a: pl.BlockSpec(a.shape, lambda b, j: (0,) * a.ndim)
    return pl.pallas_call(
        _gla_kernel,
        grid=(batch, nb),
        in_specs=[tok(GLA_QK_W), tok(GLA_QK_W), tok(GLA_W), tok(LANES), tok(GLA_W),
                  full(wup), full(bup), full(ng)],
        out_specs=tok(GLA_W),
        out_shape=jax.ShapeDtypeStruct((batch * seq, GLA_W), F32),
        scratch_shapes=[pltpu.VMEM((GLA_W, GLA_QK_W), F32)],
        compiler_params=_cparams(("parallel", "arbitrary")),
        name="gla",
    )(gq, gk, gv, ga, gr, wup, bup, ng)


def _sg_kernel(uv_ref, lg_ref, lb_ref, w_ref, bias_ref, o_ref):
    a = _gelu_tanh(uv_ref[...])
    u = a[:, :SG_W]
    v = a[:, SG_W:]
    mu = jnp.mean(v, axis=-1, keepdims=True)
    var = jnp.mean(jnp.square(v - mu), axis=-1, keepdims=True)
    vn = ((v - mu) * lax.rsqrt(var + NORM_EPS) * lg_ref[...] + lb_ref[...]).astype(BF16)
    n = w_ref.shape[1]
    ri = lax.broadcasted_iota(jnp.int32, (n, n), 0)
    ci = lax.broadcasted_iota(jnp.int32, (n, n), 1)
    lane_g = lax.broadcasted_iota(jnp.int32, (1, SG_W), 1) // SG_CH
    s = bias_ref[...]
    for g in range(SG_GROUPS):
        wg = jnp.where(ci <= ri, w_ref[g], 0.0).astype(BF16)
        s = s + jnp.where(lane_g == g, _dot(wg, vn), 0.0)
    o_ref[...] = u * s


def _spatial_gating(uv, ln_g, ln_b, w_s, bias_exp):
    T = uv.shape[0]
    full = lambda a: pl.BlockSpec(a.shape, lambda i: (0,) * a.ndim)
    return pl.pallas_call(
        _sg_kernel,
        grid=(T // SG_CHUNK,),
        in_specs=[pl.BlockSpec((SG_CHUNK, 2 * SG_W), lambda i: (i, 0)),
                  full(ln_g), full(ln_b), full(w_s), full(bias_exp)],
        out_specs=pl.BlockSpec((SG_CHUNK, SG_W), lambda i: (i, 0)),
        out_shape=jax.ShapeDtypeStruct((T, SG_W), F32),
        compiler_params=_cparams(("parallel",)),
        name="spatial_gating",
    )(uv, ln_g, ln_b, w_s, bias_exp)


def _outffn_kernel(x_ref, on_ref, og_ref, os_ref, wo_ref, fg_ref, wg_ref, wu_ref, wd_ref,
                   fin_ref, o_ref, *, final):
    x = x_ref[...]
    x = x + _dot(on_ref[...].astype(BF16), wo_ref[0:NSA_W, :])
    x = x + _dot(og_ref[...].astype(BF16), wo_ref[NSA_W:NSA_W + GLA_W, :])
    x = x + _dot(os_ref[...].astype(BF16), wo_ref[NSA_W + GLA_W:, :])
    hn = (x * lax.rsqrt(jnp.mean(x * x, axis=-1, keepdims=True) + NORM_EPS) * fg_ref[...]).astype(BF16)
    gt = _dot(hn, wg_ref[...])
    up = _dot(hn, wu_ref[...])
    y = x + _dot((gt * jax.nn.sigmoid(gt) * up).astype(BF16), wd_ref[...])
    if final:
        y = y * lax.rsqrt(jnp.mean(y * y, axis=-1, keepdims=True) + NORM_EPS) * fin_ref[...]
    o_ref[...] = y


def _outffn(x2, o_nsa, o_gla, o_sg, wo, fgain, wg, wu, wd, fin, tm, final):
    T, D = x2.shape
    row = lambda w: pl.BlockSpec((tm, w), lambda i: (i, 0))
    const = lambda a: pl.BlockSpec(a.shape, lambda i: (0,) * a.ndim)
    return pl.pallas_call(
        functools.partial(_outffn_kernel, final=final),
        grid=(T // tm,),
        in_specs=[row(D), row(NSA_W), row(GLA_W), row(SG_W), const(wo), const(fgain),
                  const(wg), const(wu), const(wd), const(fin)],
        out_specs=row(D),
        out_shape=jax.ShapeDtypeStruct((T, D), F32),
        compiler_params=_cparams(("parallel",)),
        name="outproj_ffn",
    )(x2, o_nsa, o_gla, o_sg, wo, fgain, wg, wu, wd, fin)


def kernel(x, attn_norm, w_in, cmp_pos_k, cmp_w1_k, cmp_w2_k, cmp_pos_v, cmp_w1_v, cmp_w2_v,
           gla_w_up, gla_b_up, gla_norm, sg_ln_g, sg_ln_b, sg_w, sg_b, w_out,
           ffn_norm, w_gate, w_up, w_down, final_norm):
    batch, seq, d_model = x.shape
    depth = w_in.shape[0]
    T = batch * seq
    assert seq % GLA_BLK == 0 and seq >= WIN_KEYS and seq % SEL_CH == 0
    tm_in = 256
    tm_ffn = 256

    cos_t, sin_t = _rope_tables(jnp.arange(seq))
    nrow = seq // CMP_STRIDE
    cos_c, sin_c = _rope_tables(jnp.arange(nrow) * CMP_STRIDE + CMP_LEN - 1)
    nsa_consts = _nsa_constants(seq)

    x2 = x.reshape(T, d_model)
    for l in range(depth):
        (qpad, kc, vc, ksl, vsl, kwn, vwn, gexp, gq, gk, gv, ga, gr, uv) = _inproj(
            x2, attn_norm[l][None, :], _prep_w_in(w_in[l]), cos_t, sin_t, seq, tm_in)
        kcmp, vcmp = _compress(
            kc, vc, _prep_cmp_weights(cmp_w1_k[l], cmp_w2_k[l], cmp_pos_k[l]),
            _prep_cmp_weights(cmp_w1_v[l], cmp_w2_v[l], cmp_pos_v[l]), cos_c, sin_c, batch, seq)
        o_nsa = _nsa(qpad, gexp, kcmp, vcmp, ksl, vsl, kwn, vwn, nsa_consts, batch, seq)
        wup = jnp.pad(gla_w_up[l], ((0, LANES - GLA_RANK), (0, 0)))
        o_gla = _gla(gq, gk, gv, ga, gr, wup, gla_b_up[l][None, :],
                     jnp.tile(gla_norm[l], GLA_HEADS)[None, :], batch, seq)
        bias_exp = jnp.repeat(sg_b[l].T, SG_CH, axis=1)
        o_sg = _spatial_gating(uv, sg_ln_g[l][None, :], sg_ln_b[l][None, :], sg_w[l], bias_exp)
        x2 = _outffn(x2, o_nsa, o_gla, o_sg, w_out[l].astype(BF16), ffn_norm[l][None, :],
                     w_gate[l].astype(BF16), w_up[l].astype(BF16), w_down[l].astype(BF16),
                     final_norm[None, :], tm_ffn, final=(l == depth - 1))
    return x2.reshape(batch, seq, d_model)
```

```python
import functools
import math

import numpy as np
import jax
import jax.numpy as jnp
from jax import lax
from jax.experimental import pallas as pl
from jax.experimental.pallas import tpu as pltpu

HEAD_DIM = 64
NSA_HEADS = 8
NSA_KV_GROUPS = 2
NSA_HPG = NSA_HEADS // NSA_KV_GROUPS
CMP_LEN = 32
CMP_STRIDE = 16
CMP_HIDDEN = 256
SEL_BLOCK = 64
SEL_TOPK = 16
WINDOW = 512
GLA_HEADS = 4
GLA_DK = 32
GLA_DV = 64
GLA_RANK = 16
GLA_TAU = 16.0
SG_GROUPS = 4
SG_CH = 64
SG_CHUNK = 128
NSA_W = NSA_HEADS * HEAD_DIM
GLA_W = GLA_HEADS * GLA_DV
SG_W = SG_GROUPS * SG_CH
KV_W = NSA_KV_GROUPS * HEAD_DIM
IN_SIZES = (NSA_W, KV_W, KV_W, KV_W, KV_W, KV_W, KV_W, NSA_HEADS * 3,
            GLA_HEADS * GLA_DK, GLA_HEADS * GLA_DK, GLA_W, GLA_RANK, GLA_W, 2 * SG_W)
ROPE_THETA = 10000.0
NORM_EPS = 1e-6

LANES = 128
NEG_BIG = -1e30
VMEM_LIMIT = 56 * 1024 * 1024

QPAD_W = NSA_HEADS * LANES
GEXP_W = 3 * NSA_W
GLA_QK_W = GLA_HEADS * GLA_DK
GLA_BLK = 256
GLA_SUB = 16
SEL_CH = 1024
WIN_KEYS = WINDOW + 2 * SEL_BLOCK
NSA_ROW_PARTS = 2
RANK_UNROLL = 4

HI = lax.Precision.HIGHEST
F32 = jnp.float32
BF16 = jnp.bfloat16


def _cparams(sem):
    return pltpu.CompilerParams(dimension_semantics=sem, vmem_limit_bytes=VMEM_LIMIT)


def _dot(a, b, precision=None):
    return jnp.dot(a, b, preferred_element_type=F32, precision=precision)


def _dot_nt(a, b, precision=None):
    return lax.dot_general(a, b, (((1,), (1,)), ((), ())),
                           preferred_element_type=F32, precision=precision)


def _dot_tn(a, b, precision=None):
    return lax.dot_general(a, b, (((0,), (0,)), ((), ())),
                           preferred_element_type=F32, precision=precision)


def _gelu_tanh(x):
    c = math.sqrt(2.0 / math.pi)
    return 0.5 * x * (1.0 + jnp.tanh(c * (x + 0.044715 * (x * x * x))))


def _rope_lanes(x, cos, sin_signed):
    n = x.shape[-1]
    lane = lax.broadcasted_iota(jnp.int32, x.shape, 1)
    first_half = (lane % HEAD_DIM) < (HEAD_DIM // 2)
    partner = jnp.where(first_half,
                        pltpu.roll(x, n - HEAD_DIM // 2, 1),
                        pltpu.roll(x, HEAD_DIM // 2, 1))
    return x * cos + partner * sin_signed


_INPROJ_OUT = (
    ("qpad", QPAD_W, BF16), ("kc", KV_W, F32), ("vc", KV_W, F32),
    ("ksl", KV_W, BF16), ("vsl", KV_W, BF16), ("kwn", KV_W, BF16), ("vwn", KV_W, BF16),
    ("gexp", GEXP_W, BF16), ("gq", GLA_QK_W, F32), ("gk", GLA_QK_W, F32),
    ("gv", GLA_W, F32), ("ga", LANES, F32), ("gr", GLA_W, F32), ("uv", 2 * SG_W, F32))
_INPROJ_W = sum(w for _, w, _ in _INPROJ_OUT)


def _inproj_kernel(x_ref, g_ref, w_ref, cos_ref, sin_ref, *out_refs):
    x = x_ref[...]
    hn = x * lax.rsqrt(jnp.mean(x * x, axis=-1, keepdims=True) + NORM_EPS) * g_ref[...]
    z = _dot(hn.astype(BF16), w_ref[...])
    cos = cos_ref[...]
    sin = sin_ref[...]
    off = 0
    for (name, width, dtype), o_ref in zip(_INPROJ_OUT, out_refs):
        if name in ("qpad", "ksl", "kwn"):
            for j in range(width // LANES):
                blk = _rope_lanes(z[:, off + j * LANES: off + (j + 1) * LANES], cos, sin)
                if name == "qpad":
                    blk = blk * (HEAD_DIM ** -0.5)
                o_ref[:, j * LANES:(j + 1) * LANES] = blk.astype(dtype)
        else:
            o_ref[...] = z[:, off:off + width].astype(dtype)
        off += width


def _inproj(x2, gain, w_p, cos_t, sin_t, seq, tm):
    T, D = x2.shape
    nper = seq // tm
    out_shape = [jax.ShapeDtypeStruct((T, w), dt) for _, w, dt in _INPROJ_OUT]
    out_specs = [pl.BlockSpec((tm, w), lambda i: (i, 0)) for _, w, _ in _INPROJ_OUT]
    return pl.pallas_call(
        _inproj_kernel,
        grid=(T // tm,),
        in_specs=[
            pl.BlockSpec((tm, D), lambda i: (i, 0)),
            pl.BlockSpec((1, D), lambda i: (0, 0)),
            pl.BlockSpec((D, _INPROJ_W), lambda i: (0, 0)),
            pl.BlockSpec((tm, LANES), lambda i: (i % nper, 0)),
            pl.BlockSpec((tm, LANES), lambda i: (i % nper, 0)),
        ],
        out_specs=out_specs,
        out_shape=out_shape,
        compiler_params=_cparams(("parallel",)),
        name="inproj",
    )(x2, gain, w_p, cos_t, sin_t)


def _prep_w_in(w_in):
    offs = np.cumsum((0,) + IN_SIZES)
    seg = [w_in[:, offs[k]:offs[k + 1]] for k in range(len(IN_SIZES))]
    (wq, wkc, wvc, wksl, wvsl, wkwn, wvwn, wg, wgq, wgk, wgv, wga, wgr, wuv) = seg
    D = w_in.shape[0]
    zeros64 = jnp.zeros((D, HEAD_DIM), w_in.dtype)
    qcols = []
    for h in range(NSA_HEADS):
        wh = wq[:, h * HEAD_DIM:(h + 1) * HEAD_DIM]
        qcols += [wh, zeros64] if h // NSA_HPG == 0 else [zeros64, wh]
    gcols = []
    for k in range(3):
        for h in range(NSA_HEADS):
            gcols.append(jnp.broadcast_to(wg[:, h * 3 + k][:, None], (D, HEAD_DIM)))
    wga_p = jnp.pad(wga, ((0, 0), (0, LANES - GLA_RANK)))
    return jnp.concatenate(qcols + [wkc, wvc, wksl, wvsl, wkwn, wvwn] + gcols
                           + [wgq, wgk, wgv, wga_p, wgr, wuv], axis=1).astype(BF16)


def _rope_tables(pos):
    half = HEAD_DIM // 2
    inv = 1.0 / (ROPE_THETA ** (jnp.arange(half, dtype=F32) / half))
    ang = pos.astype(F32)[:, None] * inv[None, :]
    cos = jnp.cos(ang)
    sin = jnp.sin(ang)
    cos_t = jnp.concatenate([cos, cos, cos, cos], axis=1)
    sin_t = jnp.concatenate([-sin, sin, -sin, sin], axis=1)
    return cos_t, sin_t


def _cmp_kernel(xk_ref, xv_ref, wek_ref, wev_ref, w1k_ref, w1v_ref, pk_ref, pv_ref,
                w2k_ref, w2v_ref, cos_ref, sin_ref, ok_ref, ov_ref):
    nrow = xk_ref.shape[1]
    row = lax.broadcasted_iota(jnp.int32, (nrow, LANES), 0)

    def compress(x_ref, we_ref, w1_ref, p_ref, w2_ref):
        h = _dot(x_ref[0].astype(BF16), we_ref[...])
        posb = _dot(p_ref[...].astype(BF16), w1_ref[...])[0:1, :]
        y = jnp.zeros((nrow, LANES), F32)
        for g in range(NSA_KV_GROUPS):
            a = h[:, g * CMP_HIDDEN:(g + 1) * CMP_HIDDEN]
            b = h[:, (NSA_KV_GROUPS + g) * CMP_HIDDEN:(NSA_KV_GROUPS + g + 1) * CMP_HIDDEN]
            hid = a + pltpu.roll(b, nrow - 1, 0) + posb
            y = y + _dot(_gelu_tanh(hid).astype(BF16), w2_ref[g])
        return y

    yk = _rope_lanes(compress(xk_ref, wek_ref, w1k_ref, pk_ref, w2k_ref), cos_ref[...], sin_ref[...])
    yv = compress(xv_ref, wev_ref, w1v_ref, pv_ref, w2v_ref)
    keep = row < nrow - 1
    ok_ref[0] = jnp.where(keep, yk, 0.0)
    ov_ref[0] = jnp.where(keep, yv, 0.0)


def _prep_cmp_weights(w1, w2, pos):
    half = CMP_LEN // CMP_STRIDE
    H = w1.shape[1]
    w1r = w1.reshape(half, CMP_STRIDE, HEAD_DIM, H)
    cols = []
    for a in range(half):
        for g in range(NSA_KV_GROUPS):
            blk = jnp.zeros((CMP_STRIDE, NSA_KV_GROUPS, HEAD_DIM, H), w1.dtype)
            blk = blk.at[:, g].set(w1r[a])
            cols.append(blk.reshape(CMP_STRIDE * KV_W, H))
    wexp = jnp.concatenate(cols, axis=1).astype(BF16)
    w2p = jnp.stack([jnp.pad(w2, ((0, 0), (g * HEAD_DIM, KV_W - (g + 1) * HEAD_DIM)))
                     for g in range(NSA_KV_GROUPS)]).astype(BF16)
    posf = jnp.pad(pos.reshape(1, CMP_LEN * HEAD_DIM), ((0, 7), (0, 0)))
    return wexp, w1.astype(BF16), posf, w2p


def _compress(kc, vc, wk, wv, cos_c, sin_c, batch, seq):
    nrow = seq // CMP_STRIDE
    xk = kc.reshape(batch, nrow, CMP_STRIDE * KV_W)
    xv = vc.reshape(batch, nrow, CMP_STRIDE * KV_W)
    wek, w1k, pk, w2k = wk
    wev, w1v, pv, w2v = wv
    full = lambda a: pl.BlockSpec(a.shape, lambda b: (0,) * a.ndim)
    xspec = pl.BlockSpec((1, nrow, CMP_STRIDE * KV_W), lambda b: (b, 0, 0))
    ospec = pl.BlockSpec((1, nrow, KV_W), lambda b: (b, 0, 0))
    return pl.pallas_call(
        _cmp_kernel,
        grid=(batch,),
        in_specs=[xspec, xspec, full(wek), full(wev), full(w1k), full(w1v), full(pk), full(pv),
                  full(w2k), full(w2v), full(cos_c), full(sin_c)],
        out_specs=[ospec, ospec],
        out_shape=[jax.ShapeDtypeStruct((batch, nrow, KV_W), F32)] * 2,
        compiler_params=_cparams(("parallel",)),
        name="compress",
    )(xk, xv, wek, wev, w1k, w1v, pk, pv, w2k, w2v, cos_c, sin_c)


def _unpad_heads(acc):
    qb = SEL_BLOCK
    lane = lax.broadcasted_iota(jnp.int32, (qb, LANES), 1)
    low = lane < HEAD_DIM
    outs = []
    for j in range(NSA_HEADS // 2):
        a = acc[(2 * j) * qb:(2 * j + 1) * qb]
        b = acc[(2 * j + 1) * qb:(2 * j + 2) * qb]
        if (2 * j) // NSA_HPG == 0:
            outs.append(jnp.where(low, a, pltpu.roll(b, HEAD_DIM, 1)))
        else:
            outs.append(jnp.where(low, pltpu.roll(a, HEAD_DIM, 1), b))
    return jnp.concatenate(outs, axis=1)


def _nsa_kernel(q_ref, g_ref, kc_ref, vc_ref, ksl_ref, vsl_ref, kwn_ref, vwn_ref,
                ktab_ref, kdiag_ref, kwtab_ref, qc_ref, qw_ref, eye_ref, impt_ref, o_ref,
                imp_scr, m_scr, acc_scr, win_scr, kd_scr, qs_scr, sc_scr, cmp_scr):
    qb = SEL_BLOCK
    rows = NSA_HEADS * qb
    i = pl.program_id(1)
    t0 = i * qb
    qbf = jnp.concatenate([q_ref[:, h * LANES:(h + 1) * LANES] for h in range(NSA_HEADS)], axis=0)
    t = t0 + lax.broadcasted_iota(jnp.int32, (rows, 1), 0) % qb
    ones = jnp.ones((SEL_CH, LANES), BF16)

    nblk_w = WIN_KEYS // qb
    ib = jnp.minimum(i, nblk_w - 1)
    w0 = pl.multiple_of((i - ib) * qb, qb)
    toff = pl.multiple_of((nblk_w - 1 - ib) * qb, qb)
    kw = jnp.concatenate([kwn_ref[0, pl.ds(w0, WIN_KEYS), :], kwtab_ref[pl.ds(toff, WIN_KEYS), :]], axis=1)
    vw = jnp.concatenate([vwn_ref[0, pl.ds(w0, WIN_KEYS), :], ones[0:WIN_KEYS]], axis=1)
    q_win = jnp.concatenate([qbf, qw_ref[...]], axis=1)
    parts = [slice(p * rows // NSA_ROW_PARTS, (p + 1) * rows // NSA_ROW_PARTS) for p in range(NSA_ROW_PARTS)]
    for rs in parts:
        sw = _dot_nt(q_win[rs], kw)
        ew = jnp.exp(sw - jnp.max(sw, axis=1, keepdims=True)).astype(BF16)
        ow = _dot(ew, vw)
        win_scr[rs] = ow[:, 0:LANES] / ow[:, LANES:2 * LANES]

    ncmp = kc_ref.shape[1]
    s = _dot_nt(qbf, kc_ref[0].astype(BF16))
    cend = lax.broadcasted_iota(jnp.int32, (1, ncmp), 1) * CMP_STRIDE + (CMP_LEN - 1)
    vis = cend <= t
    sm = jnp.where(vis, s, NEG_BIG)
    mx = jnp.max(sm, axis=1, keepdims=True)
    e = jnp.where(vis, jnp.exp(sm - mx), 0.0)
    d = jnp.sum(e, axis=1, keepdims=True)
    p = e / jnp.where(d > 0, d, 1.0)
    cmp_scr[...] = _dot(p.astype(BF16), vc_ref[0].astype(BF16))
    psum = jnp.concatenate(
        [sum(p[(g * NSA_HPG + h) * qb:(g * NSA_HPG + h + 1) * qb] for h in range(NSA_HPG))
         for g in range(NSA_KV_GROUPS)], axis=0)
    p_hi = psum.astype(BF16)
    p_lo = (psum - p_hi.astype(F32)).astype(BF16)
    imap = impt_ref[...].astype(BF16)
    imp_scr[...] = _dot_nt(jnp.concatenate([imap, imap], axis=1),
                           jnp.concatenate([p_hi, p_lo], axis=1))

    nsbp = imp_scr.shape[0]
    imp = imp_scr[...]
    n_id = lax.broadcasted_iota(jnp.int32, (nsbp, 1), 0)
    valid = n_id <= i
    forced = (n_id == 0) | (n_id == i) | (n_id == i - 1)
    nforced = 1 + (i >= 1).astype(jnp.int32) + (i >= 2).astype(jnp.int32)

    def rank_body(j, cnt):
        for u in range(RANK_UNROLL):
            m = 1 + RANK_UNROLL * j + u
            rowm = imp_scr[pl.ds(jnp.minimum(m, nsbp - 1), 1), :]
            rowm = jnp.where(m <= i - 2, rowm, NEG_BIG)
            tie = jnp.where(m < n_id, 1.0, 0.0)
            cnt = cnt + jnp.where(rowm > imp, 1.0, 0.0) + jnp.where(rowm == imp, tie, 0.0)
        return cnt

    ntrip = (jnp.maximum(i - 2, 0) + RANK_UNROLL - 1) // RANK_UNROLL
    cnt = lax.fori_loop(0, ntrip, rank_body, jnp.zeros(imp.shape, F32))
    budget = (SEL_TOPK - nforced).astype(F32)
    sel = valid & (forced | (cnt < budget))
    notsel_t = jnp.where(sel, 0.0, 1.0)
    notsel = _dot_tn(notsel_t, eye_ref[...])
    notsel = jnp.concatenate([notsel[g * qb:(g + 1) * qb]
                              for g in range(NSA_KV_GROUPS) for _ in range(NSA_HPG)], axis=0)


    qs_scr[...] = jnp.concatenate([qbf, (notsel + qc_ref[...]).astype(BF16)], axis=1)
    m_scr[...] = jnp.full(m_scr.shape, NEG_BIG, F32)
    acc_scr[...] = jnp.zeros(acc_scr.shape, F32)
    last = i // (SEL_CH // qb)
    k0_last = pl.multiple_of(last * SEL_CH, SEL_CH)
    off = pl.multiple_of((i % (SEL_CH // qb)) * qb, qb)
    kd_scr[...] = ktab_ref[pl.ds(k0_last, SEL_CH), :]
    kd_scr[pl.ds(off, qb), :] = kd_scr[pl.ds(off, qb), :] + kdiag_ref[...]

    ncol = SEL_CH // LANES

    def score_body(c, carry):
        k0 = pl.multiple_of(c * SEL_CH, SEL_CH)
        kmask = jnp.where(c == last, kd_scr[...], ktab_ref[pl.ds(k0, SEL_CH), :])
        k = jnp.concatenate([ksl_ref[0, pl.ds(k0, SEL_CH), :], kmask], axis=1)
        for rs in parts:
            sc = _dot_nt(qs_scr[rs], k)
            sc_scr[c, rs, :] = sc
            mx = m_scr[rs]
            for j in range(ncol):
                mx = jnp.maximum(mx, sc[:, j * LANES:(j + 1) * LANES])
            m_scr[rs] = mx
        return carry

    lax.fori_loop(0, last + 1, score_body, 0)
    m_scr[...] = jnp.broadcast_to(jnp.max(m_scr[...], axis=1, keepdims=True), m_scr.shape)

    def value_body(c, carry):
        k0 = pl.multiple_of(c * SEL_CH, SEL_CH)
        v = jnp.concatenate([vsl_ref[0, pl.ds(k0, SEL_CH), :], ones], axis=1)
        for rs in parts:
            mb = m_scr[rs]
            pr = jnp.concatenate([jnp.exp(sc_scr[c, rs, j * LANES:(j + 1) * LANES] - mb)
                                  for j in range(ncol)], axis=1).astype(BF16)
            acc_scr[rs] += _dot(pr, v)
        return carry

    lax.fori_loop(0, last + 1, value_body, 0)
    o_slc = acc_scr[:, 0:LANES] / acc_scr[:, LANES:2 * LANES]

    gate = jax.nn.sigmoid(g_ref[...].astype(F32))
    o_ref[...] = (gate[:, 0:NSA_W] * _unpad_heads(cmp_scr[...])
                  + gate[:, NSA_W:2 * NSA_W] * _unpad_heads(o_slc)
                  + gate[:, 2 * NSA_W:3 * NSA_W] * _unpad_heads(win_scr[...]))


def _nsa(qpad, gexp, kcmp, vcmp, ksl, vsl, kwn, vwn, consts, batch, seq):
    qb = SEL_BLOCK
    nq = seq // qb
    rows = NSA_HEADS * qb
    kv3 = lambda a: a.reshape(batch, seq, KV_W)
    per_b = lambda n: pl.BlockSpec((1, n, KV_W), lambda b, i: (b, 0, 0))
    full = lambda a: pl.BlockSpec(a.shape, lambda b, i: (0,) * a.ndim)
    return pl.pallas_call(
        _nsa_kernel,
        grid=(batch, nq),
        in_specs=[
            pl.BlockSpec((qb, QPAD_W), lambda b, i: (b * nq + i, 0)),
            pl.BlockSpec((qb, GEXP_W), lambda b, i: (b * nq + i, 0)),
            per_b(kcmp.shape[1]), per_b(vcmp.shape[1]),
            per_b(seq), per_b(seq), per_b(seq), per_b(seq),
        ] + [full(c) for c in consts],
        out_specs=pl.BlockSpec((qb, NSA_W), lambda b, i: (b * nq + i, 0)),
        out_shape=jax.ShapeDtypeStruct((batch * seq, NSA_W), F32),
        scratch_shapes=[
            pltpu.VMEM((consts[-1].shape[0], NSA_KV_GROUPS * qb), F32),
            pltpu.VMEM((rows, LANES), F32),
            pltpu.VMEM((rows, 2 * LANES), F32),
            pltpu.VMEM((rows, LANES), F32),
            pltpu.VMEM((SEL_CH, LANES), BF16),
            pltpu.VMEM((rows, 2 * LANES), BF16),
            pltpu.VMEM((seq // SEL_CH, rows, SEL_CH), F32),
            pltpu.VMEM((rows, LANES), F32),
        ],
        compiler_params=_cparams(("parallel", "arbitrary")),
        name="nsa",
    )(qpad, gexp, kcmp, vcmp, kv3(ksl), kv3(vsl), kv3(kwn), kv3(vwn), *consts)


def _nsa_constants(seq):
    qb = SEL_BLOCK
    nc_pad = seq // CMP_STRIDE
    nsb = seq // qb
    assert nsb <= qb
    nsb_pad = max(nsb, 8)
    cs = np.arange(nc_pad) * CMP_STRIDE
    ce = cs + CMP_LEN
    bs = np.arange(nsb_pad) * qb
    be = bs + qb
    ov = np.clip(np.minimum(ce[None, :], be[:, None]) - np.maximum(cs[None, :], bs[:, None]), 0, None)
    impt = ov / CMP_LEN
    pos = np.arange(seq)
    lane = np.arange(LANES)[None, :]
    ktab = np.where(lane == pos[:, None] // qb, NEG_BIG, 0.0)
    kdiag = np.where(lane == qb + np.arange(qb)[:, None], NEG_BIG, 0.0)
    last = WIN_KEYS // qb - 1
    x = np.arange(WIN_KEYS + last * qb)[:, None]
    kb, j = x // qb, x % qb
    kwtab = np.where((((kb == 0) | (kb > last)) & (lane == 0)) | ((kb == 1) & (lane == j))
                     | ((kb == last) & (lane == qb + j)), NEG_BIG, 0.0)
    r = (np.arange(NSA_HEADS * qb) % qb)[:, None]
    upper = (lane >= qb) & (lane - qb > r)
    qc = np.where(upper, 1.0, 0.0)
    qw = np.where(upper | ((lane < qb) & (lane <= r)), 1.0, 0.0)
    eye = np.eye(nsb_pad, LANES)
    return (jnp.asarray(ktab, BF16), jnp.asarray(kdiag, BF16), jnp.asarray(kwtab, BF16),
            jnp.asarray(qc, F32), jnp.asarray(qw, BF16),
            jnp.asarray(eye, F32), jnp.asarray(impt, F32))


def _gla_kernel(q_ref, k_ref, v_ref, a_ref, r_ref, wup_ref, bup_ref, ng_ref, o_ref, st_ref):
    blk = q_ref.shape[0]
    nsub = blk // GLA_SUB

    @pl.when(pl.program_id(1) == 0)
    def _():
        st_ref[...] = jnp.zeros(st_ref.shape, F32)

    x = _dot(a_ref[...], wup_ref[...], HI) + bup_ref[...]
    g = (jnp.minimum(x, 0.0) - jnp.log1p(jnp.exp(-jnp.abs(x)))) / GLA_TAU
    ri = lax.broadcasted_iota(jnp.int32, (blk, blk), 0)
    ci = lax.broadcasted_iota(jnp.int32, (blk, blk), 1)
    b = _dot(jnp.where(ci <= ri, 1.0, 0.0), g, HI)
    blast = b[blk - 1:blk, :]
    q = q_ref[...] * (GLA_DK ** -0.5)
    k = k_ref[...]
    v = v_ref[...]
    vb = v.astype(BF16)

    st = st_ref[...]
    o = _dot_nt((q * jnp.exp(b)).astype(BF16), st.astype(BF16))
    khat = k * jnp.exp(blast - b)
    srow = lax.broadcasted_iota(jnp.int32, st.shape, 0) // GLA_DV
    scol = lax.broadcasted_iota(jnp.int32, st.shape, 1) // GLA_DK
    st_ref[...] = st * jnp.exp(blast) + jnp.where(srow == scol, _dot_tn(v, khat), 0.0)

    krow = lax.broadcasted_iota(jnp.int32, (blk, 1), 0)
    lane_qk = lax.broadcasted_iota(jnp.int32, (1, GLA_QK_W), 1) // GLA_DK
    lane_v = lax.broadcasted_iota(jnp.int32, (1, GLA_W), 1) // GLA_DV
    qrow = lax.broadcasted_iota(jnp.int32, (GLA_HEADS * GLA_SUB, 1), 0) % GLA_SUB
    kcol = lax.broadcasted_iota(jnp.int32, (1, blk), 1)
    for c in range(nsub):
        lo = c * GLA_SUB
        ref_b = b[lo:lo + 1, :]
        qt = q[lo:lo + GLA_SUB] * jnp.exp(b[lo:lo + GLA_SUB] - ref_b)
        kt = k * jnp.exp(jnp.where(krow < lo + GLA_SUB, ref_b - b, 0.0))
        qs = jnp.concatenate([jnp.where(lane_qk == h, qt, 0.0) for h in range(GLA_HEADS)], axis=0)
        a = _dot_nt(qs.astype(BF16), kt.astype(BF16))
        a = jnp.where(kcol <= lo + qrow, a, 0.0)
        r = _dot(a.astype(BF16), vb)
        oi = sum(jnp.where(lane_v == h, r[h * GLA_SUB:(h + 1) * GLA_SUB], 0.0) for h in range(GLA_HEADS))
        o_ref[lo:lo + GLA_SUB, :] = o[lo:lo + GLA_SUB] + oi

    o = o_ref[...]
    gi = lax.broadcasted_iota(jnp.int32, (GLA_W, GLA_W), 0) // GLA_DV
    gj = lax.broadcasted_iota(jnp.int32, (GLA_W, GLA_W), 1) // GLA_DV
    ms = _dot(o * o, jnp.where(gi == gj, 1.0 / GLA_DV, 0.0), HI)
    rr = r_ref[...]
    o_ref[...] = o * lax.rsqrt(ms + NORM_EPS) * ng_ref[...] * (rr * jax.nn.sigmoid(rr))


def _gla(gq, gk, gv, ga, gr, wup, bup, ng, batch, seq):
    blk = min(GLA_BLK, seq)
    nb = seq // blk
    tok = lambda w: pl.BlockSpec((blk, w), lambda b, j: (b * nb + j, 0))
    full = lambda a: pl.BlockSpec(a.shape, lambda b, j: (0,) * a.ndim)
    return pl.pallas_call(
        _gla_kernel,
        grid=(batch, nb),
        in_specs=[tok(GLA_QK_W), tok(GLA_QK_W), tok(GLA_W), tok(LANES), tok(GLA_W),
                  full(wup), full(bup), full(ng)],
        out_specs=tok(GLA_W),
        out_shape=jax.ShapeDtypeStruct((batch * seq, GLA_W), F32),
        scratch_shapes=[pltpu.VMEM((GLA_W, GLA_QK_W), F32)],
        compiler_params=_cparams(("parallel", "arbitrary")),
        name="gla",
    )(gq, gk, gv, ga, gr, wup, bup, ng)


def _sg_kernel(uv_ref, lg_ref, lb_ref, w_ref, bias_ref, o_ref):
    a = _gelu_tanh(uv_ref[...])
    u = a[:, :SG_W]
    v = a[:, SG_W:]
    mu = jnp.mean(v, axis=-1, keepdims=True)
    var = jnp.mean(jnp.square(v - mu), axis=-1, keepdims=True)
    vn = ((v - mu) * lax.rsqrt(var + NORM_EPS) * lg_ref[...] + lb_ref[...]).astype(BF16)
    n = w_ref.shape[1]
    ri = lax.broadcasted_iota(jnp.int32, (n, n), 0)
    ci = lax.broadcasted_iota(jnp.int32, (n, n), 1)
    lane_g = lax.broadcasted_iota(jnp.int32, (1, SG_W), 1) // SG_CH
    s = bias_ref[...]
    for g in range(SG_GROUPS):
        wg = jnp.where(ci <= ri, w_ref[g], 0.0).astype(BF16)
        s = s + jnp.where(lane_g == g, _dot(wg, vn), 0.0)
    o_ref[...] = u * s


def _spatial_gating(uv, ln_g, ln_b, w_s, bias_exp):
    T = uv.shape[0]
    full = lambda a: pl.BlockSpec(a.shape, lambda i: (0,) * a.ndim)
    return pl.pallas_call(
        _sg_kernel,
        grid=(T // SG_CHUNK,),
        in_specs=[pl.BlockSpec((SG_CHUNK, 2 * SG_W), lambda i: (i, 0)),
                  full(ln_g), full(ln_b), full(w_s), full(bias_exp)],
        out_specs=pl.BlockSpec((SG_CHUNK, SG_W), lambda i: (i, 0)),
        out_shape=jax.ShapeDtypeStruct((T, SG_W), F32),
        compiler_params=_cparams(("parallel",)),
        name="spatial_gating",
    )(uv, ln_g, ln_b, w_s, bias_exp)


def _outffn_kernel(x_ref, on_ref, og_ref, os_ref, wo_ref, fg_ref, wg_ref, wu_ref, wd_ref,
                   fin_ref, o_ref, *, final):
    x = x_ref[...]
    x = x + _dot(on_ref[...].astype(BF16), wo_ref[0:NSA_W, :])
    x = x + _dot(og_ref[...].astype(BF16), wo_ref[NSA_W:NSA_W + GLA_W, :])
    x = x + _dot(os_ref[...].astype(BF16), wo_ref[NSA_W + GLA_W:, :])
    hn = (x * lax.rsqrt(jnp.mean(x * x, axis=-1, keepdims=True) + NORM_EPS) * fg_ref[...]).astype(BF16)
    gt = _dot(hn, wg_ref[...])
    up = _dot(hn, wu_ref[...])
    y = x + _dot((gt * jax.nn.sigmoid(gt) * up).astype(BF16), wd_ref[...])
    if final:
        y = y * lax.rsqrt(jnp.mean(y * y, axis=-1, keepdims=True) + NORM_EPS) * fin_ref[...]
    o_ref[...] = y


def _outffn(x2, o_nsa, o_gla, o_sg, wo, fgain, wg, wu, wd, fin, tm, final):
    T, D = x2.shape
    row = lambda w: pl.BlockSpec((tm, w), lambda i: (i, 0))
    const = lambda a: pl.BlockSpec(a.shape, lambda i: (0,) * a.ndim)
    return pl.pallas_call(
        functools.partial(_outffn_kernel, final=final),
        grid=(T // tm,),
        in_specs=[row(D), row(NSA_W), row(GLA_W), row(SG_W), const(wo), const(fgain),
                  const(wg), const(wu), const(wd), const(fin)],
        out_specs=row(D),
        out_shape=jax.ShapeDtypeStruct((T, D), F32),
        compiler_params=_cparams(("parallel",)),
        name="outproj_ffn",
    )(x2, o_nsa, o_gla, o_sg, wo, fgain, wg, wu, wd, fin)


def kernel(x, attn_norm, w_in, cmp_pos_k, cmp_w1_k, cmp_w2_k, cmp_pos_v, cmp_w1_v, cmp_w2_v,
           gla_w_up, gla_b_up, gla_norm, sg_ln_g, sg_ln_b, sg_w, sg_b, w_out,
           ffn_norm, w_gate, w_up, w_down, final_norm):
    batch, seq, d_model = x.shape
    depth = w_in.shape[0]
    T = batch * seq
    assert seq % GLA_BLK == 0 and seq >= WIN_KEYS and seq % SEL_CH == 0
    tm_in = 256
    tm_ffn = 256

    cos_t, sin_t = _rope_tables(jnp.arange(seq))
    nrow = seq // CMP_STRIDE
    cos_c, sin_c = _rope_tables(jnp.arange(nrow) * CMP_STRIDE + CMP_LEN - 1)
    nsa_consts = _nsa_constants(seq)

    x2 = x.reshape(T, d_model)
    for l in range(depth):
        (qpad, kc, vc, ksl, vsl, kwn, vwn, gexp, gq, gk, gv, ga, gr, uv) = _inproj(
            x2, attn_norm[l][None, :], _prep_w_in(w_in[l]), cos_t, sin_t, seq, tm_in)
        kcmp, vcmp = _compress(
            kc, vc, _prep_cmp_weights(cmp_w1_k[l], cmp_w2_k[l], cmp_pos_k[l]),
            _prep_cmp_weights(cmp_w1_v[l], cmp_w2_v[l], cmp_pos_v[l]), cos_c, sin_c, batch, seq)
        o_nsa = _nsa(qpad, gexp, kcmp, vcmp, ksl, vsl, kwn, vwn, nsa_consts, batch, seq)
        wup = jnp.pad(gla_w_up[l], ((0, LANES - GLA_RANK), (0, 0)))
        o_gla = _gla(gq, gk, gv, ga, gr, wup, gla_b_up[l][None, :],
                     jnp.tile(gla_norm[l], GLA_HEADS)[None, :], batch, seq)
        bias_exp = jnp.repeat(sg_b[l].T, SG_CH, axis=1)
        o_sg = _spatial_gating(uv, sg_ln_g[l][None, :], sg_ln_b[l][None, :], sg_w[l], bias_exp)
        x2 = _outffn(x2, o_nsa, o_gla, o_sg, w_out[l].astype(BF16), ffn_norm[l][None, :],
                     w_gate[l].astype(BF16), w_up[l].astype(BF16), w_down[l].astype(BF16),
                     final_norm[None, :], tm_ffn, final=(l == depth - 1))
    return x2.reshape(batch, seq, d_model)
```

```python
import functools
import math

import numpy as np
import jax
import jax.numpy as jnp
from jax import lax
from jax.experimental import pallas as pl
from jax.experimental.pallas import tpu as pltpu

HEAD_DIM = 64
NSA_HEADS = 8
NSA_KV_GROUPS = 2
NSA_HPG = NSA_HEADS // NSA_KV_GROUPS
CMP_LEN = 32
CMP_STRIDE = 16
CMP_HIDDEN = 256
SEL_BLOCK = 64
SEL_TOPK = 16
WINDOW = 512
GLA_HEADS = 4
GLA_DK = 32
GLA_DV = 64
GLA_RANK = 16
GLA_TAU = 16.0
SG_GROUPS = 4
SG_CH = 64
SG_CHUNK = 128
NSA_W = NSA_HEADS * HEAD_DIM
GLA_W = GLA_HEADS * GLA_DV
SG_W = SG_GROUPS * SG_CH
KV_W = NSA_KV_GROUPS * HEAD_DIM
IN_SIZES = (NSA_W, KV_W, KV_W, KV_W, KV_W, KV_W, KV_W, NSA_HEADS * 3,
            GLA_HEADS * GLA_DK, GLA_HEADS * GLA_DK, GLA_W, GLA_RANK, GLA_W, 2 * SG_W)
ROPE_THETA = 10000.0
NORM_EPS = 1e-6

LANES = 128
NEG_BIG = -1e30
VMEM_LIMIT = 56 * 1024 * 1024

QPAD_W = NSA_HEADS * LANES
GEXP_W = 3 * NSA_W
GLA_QK_W = GLA_HEADS * GLA_DK
GLA_BLK = 256
GLA_SUB = 16
SEL_CH = 1024
WIN_KEYS = WINDOW + 2 * SEL_BLOCK
NSA_ROW_PARTS = 2
RANK_UNROLL = 4
SG_BLK = 4 * SG_CHUNK

HI = lax.Precision.HIGHEST
F32 = jnp.float32
BF16 = jnp.bfloat16


def _cparams(sem):
    return pltpu.CompilerParams(dimension_semantics=sem, vmem_limit_bytes=VMEM_LIMIT)


def _dot(a, b, precision=None):
    return jnp.dot(a, b, preferred_element_type=F32, precision=precision)


def _dot_nt(a, b, precision=None):
    return lax.dot_general(a, b, (((1,), (1,)), ((), ())),
                           preferred_element_type=F32, precision=precision)


def _dot_tn(a, b, precision=None):
    return lax.dot_general(a, b, (((0,), (0,)), ((), ())),
                           preferred_element_type=F32, precision=precision)


def _gelu_tanh(x):
    c = math.sqrt(2.0 / math.pi)
    return 0.5 * x * (1.0 + jnp.tanh(c * (x + 0.044715 * (x * x * x))))


def _rope_lanes(x, cos, sin_signed):
    n = x.shape[-1]
    lane = lax.broadcasted_iota(jnp.int32, x.shape, 1)
    first_half = (lane % HEAD_DIM) < (HEAD_DIM // 2)
    partner = jnp.where(first_half,
                        pltpu.roll(x, n - HEAD_DIM // 2, 1),
                        pltpu.roll(x, HEAD_DIM // 2, 1))
    return x * cos + partner * sin_signed


_INPROJ_OUT = (
    ("qpad", QPAD_W, BF16), ("kc", KV_W, F32), ("vc", KV_W, F32),
    ("ksl", KV_W, BF16), ("vsl", KV_W, BF16), ("kwn", KV_W, BF16), ("vwn", KV_W, BF16),
    ("gexp", GEXP_W, BF16), ("gq", GLA_QK_W, F32), ("gk", GLA_QK_W, F32),
    ("gv", GLA_W, F32), ("ga", LANES, F32), ("gr", GLA_W, F32), ("uv", 2 * SG_W, F32))
_INPROJ_W = sum(w for _, w, _ in _INPROJ_OUT)


def _inproj_kernel(x_ref, g_ref, w_ref, cos_ref, sin_ref, *out_refs):
    x = x_ref[...]
    hn = x * lax.rsqrt(jnp.mean(x * x, axis=-1, keepdims=True) + NORM_EPS) * g_ref[...]
    z = _dot(hn.astype(BF16), w_ref[...])
    cos = cos_ref[...]
    sin = sin_ref[...]
    off = 0
    for (name, width, dtype), o_ref in zip(_INPROJ_OUT, out_refs):
        if name in ("qpad", "ksl", "kwn"):
            for j in range(width // LANES):
                blk = _rope_lanes(z[:, off + j * LANES: off + (j + 1) * LANES], cos, sin)
                if name == "qpad":
                    blk = blk * (HEAD_DIM ** -0.5)
                o_ref[:, j * LANES:(j + 1) * LANES] = blk.astype(dtype)
        else:
            o_ref[...] = z[:, off:off + width].astype(dtype)
        off += width


def _inproj(x2, gain, w_p, cos_t, sin_t, seq, tm):
    T, D = x2.shape
    nper = seq // tm
    out_shape = [jax.ShapeDtypeStruct((T, w), dt) for _, w, dt in _INPROJ_OUT]
    out_specs = [pl.BlockSpec((tm, w), lambda i: (i, 0)) for _, w, _ in _INPROJ_OUT]
    return pl.pallas_call(
        _inproj_kernel,
        grid=(T // tm,),
        in_specs=[
            pl.BlockSpec((tm, D), lambda i: (i, 0)),
            pl.BlockSpec((1, D), lambda i: (0, 0)),
            pl.BlockSpec((D, _INPROJ_W), lambda i: (0, 0), pipeline_mode=pl.Buffered(1)),
            pl.BlockSpec((tm, LANES), lambda i: (i % nper, 0)),
            pl.BlockSpec((tm, LANES), lambda i: (i % nper, 0)),
        ],
        out_specs=out_specs,
        out_shape=out_shape,
        compiler_params=_cparams(("parallel",)),
        name="inproj",
    )(x2, gain, w_p, cos_t, sin_t)


def _prep_w_in(w_in):
    offs = np.cumsum((0,) + IN_SIZES)
    seg = [w_in[:, offs[k]:offs[k + 1]] for k in range(len(IN_SIZES))]
    (wq, wkc, wvc, wksl, wvsl, wkwn, wvwn, wg, wgq, wgk, wgv, wga, wgr, wuv) = seg
    D = w_in.shape[0]
    zeros64 = jnp.zeros((D, HEAD_DIM), w_in.dtype)
    qcols = []
    for h in range(NSA_HEADS):
        wh = wq[:, h * HEAD_DIM:(h + 1) * HEAD_DIM]
        qcols += [wh, zeros64] if h // NSA_HPG == 0 else [zeros64, wh]
    gcols = []
    for k in range(3):
        for h in range(NSA_HEADS):
            gcols.append(jnp.broadcast_to(wg[:, h * 3 + k][:, None], (D, HEAD_DIM)))
    wga_p = jnp.pad(wga, ((0, 0), (0, LANES - GLA_RANK)))
    return jnp.concatenate(qcols + [wkc, wvc, wksl, wvsl, wkwn, wvwn] + gcols
                           + [wgq, wgk, wgv, wga_p, wgr, wuv], axis=1).astype(BF16)


def _rope_tables(pos):
    half = HEAD_DIM // 2
    inv = 1.0 / (ROPE_THETA ** (jnp.arange(half, dtype=F32) / half))
    ang = pos.astype(F32)[:, None] * inv[None, :]
    cos = jnp.cos(ang)
    sin = jnp.sin(ang)
    cos_t = jnp.concatenate([cos, cos, cos, cos], axis=1)
    sin_t = jnp.concatenate([-sin, sin, -sin, sin], axis=1)
    return cos_t, sin_t


def _cmp_kernel(xk_ref, xv_ref, wek_ref, wev_ref, w1k_ref, w1v_ref, pk_ref, pv_ref,
                w2k_ref, w2v_ref, cos_ref, sin_ref, ok_ref, ov_ref):
    nrow = xk_ref.shape[1]
    row = lax.broadcasted_iota(jnp.int32, (nrow, LANES), 0)

    def compress(x_ref, we_ref, w1_ref, p_ref, w2_ref):
        h = _dot(x_ref[0].astype(BF16), we_ref[...])
        posb = _dot(p_ref[...].astype(BF16), w1_ref[...])[0:1, :]
        y = jnp.zeros((nrow, LANES), F32)
        for g in range(NSA_KV_GROUPS):
            a = h[:, g * CMP_HIDDEN:(g + 1) * CMP_HIDDEN]
            b = h[:, (NSA_KV_GROUPS + g) * CMP_HIDDEN:(NSA_KV_GROUPS + g + 1) * CMP_HIDDEN]
            hid = a + pltpu.roll(b, nrow - 1, 0) + posb
            y = y + _dot(_gelu_tanh(hid).astype(BF16), w2_ref[g])
        return y

    yk = _rope_lanes(compress(xk_ref, wek_ref, w1k_ref, pk_ref, w2k_ref), cos_ref[...], sin_ref[...])
    yv = compress(xv_ref, wev_ref, w1v_ref, pv_ref, w2v_ref)
    keep = row < nrow - 1
    ok_ref[0] = jnp.where(keep, yk, 0.0)
    ov_ref[0] = jnp.where(keep, yv, 0.0)


def _prep_cmp_weights(w1, w2, pos):
    half = CMP_LEN // CMP_STRIDE
    H = w1.shape[1]
    w1r = w1.reshape(half, CMP_STRIDE, HEAD_DIM, H)
    cols = []
    for a in range(half):
        for g in range(NSA_KV_GROUPS):
            blk = jnp.zeros((CMP_STRIDE, NSA_KV_GROUPS, HEAD_DIM, H), w1.dtype)
            blk = blk.at[:, g].set(w1r[a])
            cols.append(blk.reshape(CMP_STRIDE * KV_W, H))
    wexp = jnp.concatenate(cols, axis=1).astype(BF16)
    w2p = jnp.stack([jnp.pad(w2, ((0, 0), (g * HEAD_DIM, KV_W - (g + 1) * HEAD_DIM)))
                     for g in range(NSA_KV_GROUPS)]).astype(BF16)
    posf = jnp.pad(pos.reshape(1, CMP_LEN * HEAD_DIM), ((0, 7), (0, 0)))
    return wexp, w1.astype(BF16), posf, w2p


def _compress(kc, vc, wk, wv, cos_c, sin_c, batch, seq):
    nrow = seq // CMP_STRIDE
    xk = kc.reshape(batch, nrow, CMP_STRIDE * KV_W)
    xv = vc.reshape(batch, nrow, CMP_STRIDE * KV_W)
    wek, w1k, pk, w2k = wk
    wev, w1v, pv, w2v = wv
    full = lambda a: pl.BlockSpec(a.shape, lambda b: (0,) * a.ndim)
    xspec = pl.BlockSpec((1, nrow, CMP_STRIDE * KV_W), lambda b: (b, 0, 0))
    ospec = pl.BlockSpec((1, nrow, KV_W), lambda b: (b, 0, 0))
    return pl.pallas_call(
        _cmp_kernel,
        grid=(batch,),
        in_specs=[xspec, xspec, full(wek), full(wev), full(w1k), full(w1v), full(pk), full(pv),
                  full(w2k), full(w2v), full(cos_c), full(sin_c)],
        out_specs=[ospec, ospec],
        out_shape=[jax.ShapeDtypeStruct((batch, nrow, KV_W), F32)] * 2,
        compiler_params=_cparams(("parallel",)),
        name="compress",
    )(xk, xv, wek, wev, w1k, w1v, pk, pv, w2k, w2v, cos_c, sin_c)


def _unpad_heads(acc):
    qb = SEL_BLOCK
    lane = lax.broadcasted_iota(jnp.int32, (qb, LANES), 1)
    low = lane < HEAD_DIM
    outs = []
    for j in range(NSA_HEADS // 2):
        a = acc[(2 * j) * qb:(2 * j + 1) * qb]
        b = acc[(2 * j + 1) * qb:(2 * j + 2) * qb]
        if (2 * j) // NSA_HPG == 0:
            outs.append(jnp.where(low, a, pltpu.roll(b, HEAD_DIM, 1)))
        else:
            outs.append(jnp.where(low, pltpu.roll(a, HEAD_DIM, 1), b))
    return jnp.concatenate(outs, axis=1)


def _nsa_kernel(q_ref, g_ref, kc_ref, vc_ref, ksl_ref, vsl_ref, kwn_ref, vwn_ref,
                ktab_ref, kdiag_ref, kwtab_ref, qc_ref, qw_ref, eye_ref, impt_ref, o_ref,
                imp_scr, m_scr, acc_scr, win_scr, kd_scr, qs_scr, sc_scr, cmp_scr):
    qb = SEL_BLOCK
    rows = NSA_HEADS * qb
    i = pl.program_id(1)
    t0 = i * qb
    qbf = jnp.concatenate([q_ref[:, h * LANES:(h + 1) * LANES] for h in range(NSA_HEADS)], axis=0)
    t = t0 + lax.broadcasted_iota(jnp.int32, (rows, 1), 0) % qb
    ones = jnp.ones((SEL_CH, LANES), BF16)

    nblk_w = WIN_KEYS // qb
    ib = jnp.minimum(i, nblk_w - 1)
    w0 = pl.multiple_of((i - ib) * qb, qb)
    toff = pl.multiple_of((nblk_w - 1 - ib) * qb, qb)
    kw = jnp.concatenate([kwn_ref[0, pl.ds(w0, WIN_KEYS), :], kwtab_ref[pl.ds(toff, WIN_KEYS), :]], axis=1)
    vw = jnp.concatenate([vwn_ref[0, pl.ds(w0, WIN_KEYS), :], ones[0:WIN_KEYS]], axis=1)
    q_win = jnp.concatenate([qbf, qw_ref[...]], axis=1)
    parts = [slice(p * rows // NSA_ROW_PARTS, (p + 1) * rows // NSA_ROW_PARTS) for p in range(NSA_ROW_PARTS)]
    for rs in parts:
        sw = _dot_nt(q_win[rs], kw)
        ew = jnp.exp(sw - jnp.max(sw, axis=1, keepdims=True)).astype(BF16)
        ow = _dot(ew, vw)
        win_scr[rs] = ow[:, 0:LANES] / ow[:, LANES:2 * LANES]

    ncmp = kc_ref.shape[1]
    s = _dot_nt(qbf, kc_ref[0].astype(BF16))
    cend = lax.broadcasted_iota(jnp.int32, (1, ncmp), 1) * CMP_STRIDE + (CMP_LEN - 1)
    vis = cend <= t
    sm = jnp.where(vis, s, NEG_BIG)
    mx = jnp.max(sm, axis=1, keepdims=True)
    e = jnp.where(vis, jnp.exp(sm - mx), 0.0)
    d = jnp.sum(e, axis=1, keepdims=True)
    p = e / jnp.where(d > 0, d, 1.0)
    cmp_scr[...] = _dot(p.astype(BF16), vc_ref[0].astype(BF16))
    psum = jnp.concatenate(
        [sum(p[(g * NSA_HPG + h) * qb:(g * NSA_HPG + h + 1) * qb] for h in range(NSA_HPG))
         for g in range(NSA_KV_GROUPS)], axis=0)
    p_hi = psum.astype(BF16)
    p_lo = (psum - p_hi.astype(F32)).astype(BF16)
    imap = impt_ref[...].astype(BF16)
    imp_scr[...] = _dot_nt(jnp.concatenate([imap, imap], axis=1),
                           jnp.concatenate([p_hi, p_lo], axis=1))

    nsbp = imp_scr.shape[0]
    imp = imp_scr[...]
    n_id = lax.broadcasted_iota(jnp.int32, (nsbp, 1), 0)
    valid = n_id <= i
    forced = (n_id == 0) | (n_id == i) | (n_id == i - 1)
    nforced = 1 + (i >= 1).astype(jnp.int32) + (i >= 2).astype(jnp.int32)

    def rank_body(j, cnt):
        for u in range(RANK_UNROLL):
            m = 1 + RANK_UNROLL * j + u
            rowm = imp_scr[pl.ds(jnp.minimum(m, nsbp - 1), 1), :]
            rowm = jnp.where(m <= i - 2, rowm, NEG_BIG)
            tie = jnp.where(m < n_id, 1.0, 0.0)
            cnt = cnt + jnp.where(rowm > imp, 1.0, 0.0) + jnp.where(rowm == imp, tie, 0.0)
        return cnt

    ntrip = (jnp.maximum(i - 2, 0) + RANK_UNROLL - 1) // RANK_UNROLL
    cnt = lax.fori_loop(0, ntrip, rank_body, jnp.zeros(imp.shape, F32))
    budget = (SEL_TOPK - nforced).astype(F32)
    sel = valid & (forced | (cnt < budget))
    notsel_t = jnp.where(sel, 0.0, 1.0)
    notsel = _dot_tn(notsel_t, eye_ref[...])
    notsel = jnp.concatenate([notsel[g * qb:(g + 1) * qb]
                              for g in range(NSA_KV_GROUPS) for _ in range(NSA_HPG)], axis=0)


    qs_scr[...] = jnp.concatenate([qbf, (notsel + qc_ref[...]).astype(BF16)], axis=1)
    m_scr[...] = jnp.full(m_scr.shape, NEG_BIG, F32)
    acc_scr[...] = jnp.zeros(acc_scr.shape, F32)
    last = i // (SEL_CH // qb)
    k0_last = pl.multiple_of(last * SEL_CH, SEL_CH)
    off = pl.multiple_of((i % (SEL_CH // qb)) * qb, qb)
    kd_scr[...] = ktab_ref[pl.ds(k0_last, SEL_CH), :]
    kd_scr[pl.ds(off, qb), :] = kd_scr[pl.ds(off, qb), :] + kdiag_ref[...]

    ncol = SEL_CH // LANES

    def score_body(c, carry):
        k0 = pl.multiple_of(c * SEL_CH, SEL_CH)
        kmask = jnp.where(c == last, kd_scr[...], ktab_ref[pl.ds(k0, SEL_CH), :])
        k = jnp.concatenate([ksl_ref[0, pl.ds(k0, SEL_CH), :], kmask], axis=1)
        for rs in parts:
            sc = _dot_nt(qs_scr[rs], k)
            sc_scr[c, rs, :] = sc
            mx = m_scr[rs]
            for j in range(ncol):
                mx = jnp.maximum(mx, sc[:, j * LANES:(j + 1) * LANES])
            m_scr[rs] = mx
        return carry

    lax.fori_loop(0, last + 1, score_body, 0)
    m_scr[...] = jnp.broadcast_to(jnp.max(m_scr[...], axis=1, keepdims=True), m_scr.shape)

    def value_body(c, carry):
        k0 = pl.multiple_of(c * SEL_CH, SEL_CH)
        v = jnp.concatenate([vsl_ref[0, pl.ds(k0, SEL_CH), :], ones], axis=1)
        for rs in parts:
            mb = m_scr[rs]
            pr = jnp.concatenate([jnp.exp(sc_scr[c, rs, j * LANES:(j + 1) * LANES] - mb)
                                  for j in range(ncol)], axis=1).astype(BF16)
            acc_scr[rs] += _dot(pr, v)
        return carry

    lax.fori_loop(0, last + 1, value_body, 0)
    o_slc = acc_scr[:, 0:LANES] / acc_scr[:, LANES:2 * LANES]

    gate = jax.nn.sigmoid(g_ref[...].astype(F32))
    o_ref[...] = (gate[:, 0:NSA_W] * _unpad_heads(cmp_scr[...])
                  + gate[:, NSA_W:2 * NSA_W] * _unpad_heads(o_slc)
                  + gate[:, 2 * NSA_W:3 * NSA_W] * _unpad_heads(win_scr[...]))


def _nsa(qpad, gexp, kcmp, vcmp, ksl, vsl, kwn, vwn, consts, batch, seq):
    qb = SEL_BLOCK
    nq = seq // qb
    rows = NSA_HEADS * qb
    kv3 = lambda a: a.reshape(batch, seq, KV_W)
    per_b = lambda n: pl.BlockSpec((1, n, KV_W), lambda b, i: (b, 0, 0))
    full = lambda a: pl.BlockSpec(a.shape, lambda b, i: (0,) * a.ndim)
    return pl.pallas_call(
        _nsa_kernel,
        grid=(batch, nq),
        in_specs=[
            pl.BlockSpec((qb, QPAD_W), lambda b, i: (b * nq + i, 0)),
            pl.BlockSpec((qb, GEXP_W), lambda b, i: (b * nq + i, 0)),
            per_b(kcmp.shape[1]), per_b(vcmp.shape[1]),
            per_b(seq), per_b(seq), per_b(seq), per_b(seq),
        ] + [full(c) for c in consts],
        out_specs=pl.BlockSpec((qb, NSA_W), lambda b, i: (b * nq + i, 0)),
        out_shape=jax.ShapeDtypeStruct((batch * seq, NSA_W), F32),
        scratch_shapes=[
            pltpu.VMEM((consts[-1].shape[0], NSA_KV_GROUPS * qb), F32),
            pltpu.VMEM((rows, LANES), F32),
            pltpu.VMEM((rows, 2 * LANES), F32),
            pltpu.VMEM((rows, LANES), F32),
            pltpu.VMEM((SEL_CH, LANES), BF16),
            pltpu.VMEM((rows, 2 * LANES), BF16),
            pltpu.VMEM((seq // SEL_CH, rows, SEL_CH), F32),
            pltpu.VMEM((rows, LANES), F32),
        ],
        compiler_params=_cparams(("parallel", "arbitrary")),
        name="nsa",
    )(qpad, gexp, kcmp, vcmp, kv3(ksl), kv3(vsl), kv3(kwn), kv3(vwn), *consts)


def _nsa_constants(seq):
    qb = SEL_BLOCK
    nc_pad = seq // CMP_STRIDE
    nsb = seq // qb
    assert nsb <= qb
    nsb_pad = max(nsb, 8)
    cs = np.arange(nc_pad) * CMP_STRIDE
    ce = cs + CMP_LEN
    bs = np.arange(nsb_pad) * qb
    be = bs + qb
    ov = np.clip(np.minimum(ce[None, :], be[:, None]) - np.maximum(cs[None, :], bs[:, None]), 0, None)
    impt = ov / CMP_LEN
    pos = np.arange(seq)
    lane = np.arange(LANES)[None, :]
    ktab = np.where(lane == pos[:, None] // qb, NEG_BIG, 0.0)
    kdiag = np.where(lane == qb + np.arange(qb)[:, None], NEG_BIG, 0.0)
    last = WIN_KEYS // qb - 1
    x = np.arange(WIN_KEYS + last * qb)[:, None]
    kb, j = x // qb, x % qb
    kwtab = np.where((((kb == 0) | (kb > last)) & (lane == 0)) | ((kb == 1) & (lane == j))
                     | ((kb == last) & (lane == qb + j)), NEG_BIG, 0.0)
    r = (np.arange(NSA_HEADS * qb) % qb)[:, None]
    upper = (lane >= qb) & (lane - qb > r)
    qc = np.where(upper, 1.0, 0.0)
    qw = np.where(upper | ((lane < qb) & (lane <= r)), 1.0, 0.0)
    eye = np.eye(nsb_pad, LANES)
    return (jnp.asarray(ktab, BF16), jnp.asarray(kdiag, BF16), jnp.asarray(kwtab, BF16),
            jnp.asarray(qc, F32), jnp.asarray(qw, BF16),
            jnp.asarray(eye, F32), jnp.asarray(impt, F32))


def _gla_kernel(q_ref, k_ref, v_ref, a_ref, r_ref, wup_ref, bup_ref, ng_ref, o_ref, st_ref):
    blk = q_ref.shape[0]
    nsub = blk // GLA_SUB

    @pl.when(pl.program_id(1) == 0)
    def _():
        st_ref[...] = jnp.zeros(st_ref.shape, F32)

    x = _dot(a_ref[...], wup_ref[...], HI) + bup_ref[...]
    g = (jnp.minimum(x, 0.0) - jnp.log1p(jnp.exp(-jnp.abs(x)))) / GLA_TAU
    ri = lax.broadcasted_iota(jnp.int32, (blk, blk), 0)
    ci = lax.broadcasted_iota(jnp.int32, (blk, blk), 1)
    b = _dot(jnp.where(ci <= ri, 1.0, 0.0), g, HI)
    blast = b[blk - 1:blk, :]
    q = q_ref[...] * (GLA_DK ** -0.5)
    k = k_ref[...]
    v = v_ref[...]
    vb = v.astype(BF16)

    st = st_ref[...]
    o = _dot_nt((q * jnp.exp(b)).astype(BF16), st.astype(BF16))
    khat = k * jnp.exp(blast - b)
    srow = lax.broadcasted_iota(jnp.int32, st.shape, 0) // GLA_DV
    scol = lax.broadcasted_iota(jnp.int32, st.shape, 1) // GLA_DK
    st_ref[...] = st * jnp.exp(blast) + jnp.where(srow == scol, _dot_tn(v, khat), 0.0)

    krow = lax.broadcasted_iota(jnp.int32, (blk, 1), 0)
    lane_qk = lax.broadcasted_iota(jnp.int32, (1, GLA_QK_W), 1) // GLA_DK
    lane_v = lax.broadcasted_iota(jnp.int32, (1, GLA_W), 1) // GLA_DV
    qrow = lax.broadcasted_iota(jnp.int32, (GLA_HEADS * GLA_SUB, 1), 0) % GLA_SUB
    kcol = lax.broadcasted_iota(jnp.int32, (1, blk), 1)
    for c in range(nsub):
        lo = c * GLA_SUB
        ref_b = b[lo:lo + 1, :]
        qt = q[lo:lo + GLA_SUB] * jnp.exp(b[lo:lo + GLA_SUB] - ref_b)
        kt = k * jnp.exp(jnp.where(krow < lo + GLA_SUB, ref_b - b, 0.0))
        qs = jnp.concatenate([jnp.where(lane_qk == h, qt, 0.0) for h in range(GLA_HEADS)], axis=0)
        a = _dot_nt(qs.astype(BF16), kt.astype(BF16))
        a = jnp.where(kcol <= lo + qrow, a, 0.0)
        r = _dot(a.astype(BF16), vb)
        oi = sum(jnp.where(lane_v == h, r[h * GLA_SUB:(h + 1) * GLA_SUB], 0.0) for h in range(GLA_HEADS))
        o_ref[lo:lo + GLA_SUB, :] = o[lo:lo + GLA_SUB] + oi

    o = o_ref[...]
    gi = lax.broadcasted_iota(jnp.int32, (GLA_W, GLA_W), 0) // GLA_DV
    gj = lax.broadcasted_iota(jnp.int32, (GLA_W, GLA_W), 1) // GLA_DV
    ms = _dot(o * o, jnp.where(gi == gj, 1.0 / GLA_DV, 0.0), HI)
    rr = r_ref[...]
    o_ref[...] = o * lax.rsqrt(ms + NORM_EPS) * ng_ref[...] * (rr * jax.nn.sigmoid(rr))


def _gla(gq, gk, gv, ga, gr, wup, bup, ng, batch, seq):
    blk = min(GLA_BLK, seq)
    nb = seq // blk
    tok = lambda w: pl.BlockSpec((blk, w), lambda b, j: (b * nb + j, 0))
    full = lambda a: pl.BlockSpec(a.shape, lambda b, j: (0,) * a.ndim)
    return pl.pallas_call(
        _gla_kernel,
        grid=(batch, nb),
        in_specs=[tok(GLA_QK_W), tok(GLA_QK_W), tok(GLA_W), tok(LANES), tok(GLA_W),
                  full(wup), full(bup), full(ng)],
        out_specs=tok(GLA_W),
        out_shape=jax.ShapeDtypeStruct((batch * seq, GLA_W), F32),
        scratch_shapes=[pltpu.VMEM((GLA_W, GLA_QK_W), F32)],
        compiler_params=_cparams(("parallel", "arbitrary")),
        name="gla",
    )(gq, gk, gv, ga, gr, wup, bup, ng)


def _sg_kernel(uv_ref, lg_ref, lb_ref, w_ref, bias_ref, o_ref):
    n = w_ref.shape[1]
    ri = lax.broadcasted_iota(jnp.int32, (n, n), 0)
    ci = lax.broadcasted_iota(jnp.int32, (n, n), 1)
    lane_g = lax.broadcasted_iota(jnp.int32, (1, SG_W), 1) // SG_CH
    wts = [jnp.where(ci <= ri, w_ref[g], 0.0).astype(BF16) for g in range(SG_GROUPS)]
    for c in range(uv_ref.shape[0] // n):
        rs = slice(c * n, (c + 1) * n)
        a = _gelu_tanh(uv_ref[rs, :])
        u = a[:, :SG_W]
        v = a[:, SG_W:]
        mu = jnp.mean(v, axis=-1, keepdims=True)
        var = jnp.mean(jnp.square(v - mu), axis=-1, keepdims=True)
        vn = ((v - mu) * lax.rsqrt(var + NORM_EPS) * lg_ref[...] + lb_ref[...]).astype(BF16)
        s = bias_ref[...]
        for g in range(SG_GROUPS):
            s = s + jnp.where(lane_g == g, _dot(wts[g], vn), 0.0)
        o_ref[rs, :] = u * s


def _spatial_gating(uv, ln_g, ln_b, w_s, bias_exp):
    T = uv.shape[0]
    full = lambda a: pl.BlockSpec(a.shape, lambda i: (0,) * a.ndim)
    return pl.pallas_call(
        _sg_kernel,
        grid=(T // SG_BLK,),
        in_specs=[pl.BlockSpec((SG_BLK, 2 * SG_W), lambda i: (i, 0)),
                  full(ln_g), full(ln_b), full(w_s), full(bias_exp)],
        out_specs=pl.BlockSpec((SG_BLK, SG_W), lambda i: (i, 0)),
        out_shape=jax.ShapeDtypeStruct((T, SG_W), F32),
        compiler_params=_cparams(("parallel",)),
        name="spatial_gating",
    )(uv, ln_g, ln_b, w_s, bias_exp)


def _outffn_kernel(x_ref, on_ref, og_ref, os_ref, wo_ref, fg_ref, wg_ref, wu_ref, wd_ref,
                   fin_ref, o_ref, *, final):
    x = x_ref[...]
    x = x + _dot(on_ref[...].astype(BF16), wo_ref[0:NSA_W, :])
    x = x + _dot(og_ref[...].astype(BF16), wo_ref[NSA_W:NSA_W + GLA_W, :])
    x = x + _dot(os_ref[...].astype(BF16), wo_ref[NSA_W + GLA_W:, :])
    hn = (x * lax.rsqrt(jnp.mean(x * x, axis=-1, keepdims=True) + NORM_EPS) * fg_ref[...]).astype(BF16)
    gt = _dot(hn, wg_ref[...])
    up = _dot(hn, wu_ref[...])
    y = x + _dot((gt * jax.nn.sigmoid(gt) * up).astype(BF16), wd_ref[...])
    if final:
        y = y * lax.rsqrt(jnp.mean(y * y, axis=-1, keepdims=True) + NORM_EPS) * fin_ref[...]
    o_ref[...] = y


def _outffn(x2, o_nsa, o_gla, o_sg, wo, fgain, wg, wu, wd, fin, tm, final):
    T, D = x2.shape
    row = lambda w: pl.BlockSpec((tm, w), lambda i: (i, 0))
    const = lambda a: pl.BlockSpec(a.shape, lambda i: (0,) * a.ndim, pipeline_mode=pl.Buffered(1))
    return pl.pallas_call(
        functools.partial(_outffn_kernel, final=final),
        grid=(T // tm,),
        in_specs=[row(D), row(NSA_W), row(GLA_W), row(SG_W), const(wo), const(fgain),
                  const(wg), const(wu), const(wd), const(fin)],
        out_specs=row(D),
        out_shape=jax.ShapeDtypeStruct((T, D), F32),
        compiler_params=_cparams(("parallel",)),
        name="outproj_ffn",
    )(x2, o_nsa, o_gla, o_sg, wo, fgain, wg, wu, wd, fin)


def kernel(x, attn_norm, w_in, cmp_pos_k, cmp_w1_k, cmp_w2_k, cmp_pos_v, cmp_w1_v, cmp_w2_v,
           gla_w_up, gla_b_up, gla_norm, sg_ln_g, sg_ln_b, sg_w, sg_b, w_out,
           ffn_norm, w_gate, w_up, w_down, final_norm):
    batch, seq, d_model = x.shape
    depth = w_in.shape[0]
    T = batch * seq
    assert seq % GLA_BLK == 0 and seq >= WIN_KEYS and seq % SEL_CH == 0
    tm_in = 512
    tm_ffn = 512

    cos_t, sin_t = _rope_tables(jnp.arange(seq))
    nrow = seq // CMP_STRIDE
    cos_c, sin_c = _rope_tables(jnp.arange(nrow) * CMP_STRIDE + CMP_LEN - 1)
    nsa_consts = _nsa_constants(seq)

    x2 = x.reshape(T, d_model)
    for l in range(depth):
        (qpad, kc, vc, ksl, vsl, kwn, vwn, gexp, gq, gk, gv, ga, gr, uv) = _inproj(
            x2, attn_norm[l][None, :], _prep_w_in(w_in[l]), cos_t, sin_t, seq, tm_in)
        kcmp, vcmp = _compress(
            kc, vc, _prep_cmp_weights(cmp_w1_k[l], cmp_w2_k[l], cmp_pos_k[l]),
            _prep_cmp_weights(cmp_w1_v[l], cmp_w2_v[l], cmp_pos_v[l]), cos_c, sin_c, batch, seq)
        o_nsa = _nsa(qpad, gexp, kcmp, vcmp, ksl, vsl, kwn, vwn, nsa_consts, batch, seq)
        wup = jnp.pad(gla_w_up[l], ((0, LANES - GLA_RANK), (0, 0)))
        o_gla = _gla(gq, gk, gv, ga, gr, wup, gla_b_up[l][None, :],
                     jnp.tile(gla_norm[l], GLA_HEADS)[None, :], batch, seq)
        bias_exp = jnp.repeat(sg_b[l].T, SG_CH, axis=1)
        o_sg = _spatial_gating(uv, sg_ln_g[l][None, :], sg_ln_b[l][None, :], sg_w[l], bias_exp)
        x2 = _outffn(x2, o_nsa, o_gla, o_sg, w_out[l].astype(BF16), ffn_norm[l][None, :],
                     w_gate[l].astype(BF16), w_up[l].astype(BF16), w_down[l].astype(BF16),
                     final_norm[None, :], tm_ffn, final=(l == depth - 1))
    return x2.reshape(batch, seq, d_model)
```

```python
import functools
import math

import numpy as np
import jax
import jax.numpy as jnp
from jax import lax
from jax.experimental import pallas as pl
from jax.experimental.pallas import tpu as pltpu

HEAD_DIM = 64
NSA_HEADS = 8
NSA_KV_GROUPS = 2
NSA_HPG = NSA_HEADS // NSA_KV_GROUPS
CMP_LEN = 32
CMP_STRIDE = 16
CMP_HIDDEN = 256
SEL_BLOCK = 64
SEL_TOPK = 16
WINDOW = 512
GLA_HEADS = 4
GLA_DK = 32
GLA_DV = 64
GLA_RANK = 16
GLA_TAU = 16.0
SG_GROUPS = 4
SG_CH = 64
SG_CHUNK = 128
NSA_W = NSA_HEADS * HEAD_DIM
GLA_W = GLA_HEADS * GLA_DV
SG_W = SG_GROUPS * SG_CH
KV_W = NSA_KV_GROUPS * HEAD_DIM
IN_SIZES = (NSA_W, KV_W, KV_W, KV_W, KV_W, KV_W, KV_W, NSA_HEADS * 3,
            GLA_HEADS * GLA_DK, GLA_HEADS * GLA_DK, GLA_W, GLA_RANK, GLA_W, 2 * SG_W)
ROPE_THETA = 10000.0
NORM_EPS = 1e-6

LANES = 128
NEG_BIG = -1e30
VMEM_LIMIT = 56 * 1024 * 1024

QPAD_W = NSA_HEADS * LANES
GEXP_W = 3 * NSA_W
GLA_QK_W = GLA_HEADS * GLA_DK
GLA_BLK = 256
GLA_SUB = 16
SEL_CH = 1024
WIN_KEYS = WINDOW + 2 * SEL_BLOCK
NSA_ROW_PARTS = 2
RANK_UNROLL = 4
SG_BLK = 4 * SG_CHUNK
NSA_PAIR = 2

HI = lax.Precision.HIGHEST
F32 = jnp.float32
BF16 = jnp.bfloat16


def _cparams(sem):
    return pltpu.CompilerParams(dimension_semantics=sem, vmem_limit_bytes=VMEM_LIMIT)


def _dot(a, b, precision=None):
    return jnp.dot(a, b, preferred_element_type=F32, precision=precision)


def _dot_nt(a, b, precision=None):
    return lax.dot_general(a, b, (((1,), (1,)), ((), ())),
                           preferred_element_type=F32, precision=precision)


def _dot_tn(a, b, precision=None):
    return lax.dot_general(a, b, (((0,), (0,)), ((), ())),
                           preferred_element_type=F32, precision=precision)


def _gelu_tanh(x):
    c = math.sqrt(2.0 / math.pi)
    return 0.5 * x * (1.0 + jnp.tanh(c * (x + 0.044715 * (x * x * x))))


def _rope_lanes(x, cos, sin_signed):
    n = x.shape[-1]
    lane = lax.broadcasted_iota(jnp.int32, x.shape, 1)
    first_half = (lane % HEAD_DIM) < (HEAD_DIM // 2)
    partner = jnp.where(first_half,
                        pltpu.roll(x, n - HEAD_DIM // 2, 1),
                        pltpu.roll(x, HEAD_DIM // 2, 1))
    return x * cos + partner * sin_signed


_INPROJ_OUT = (
    ("qpad", QPAD_W, BF16), ("kc", KV_W, F32), ("vc", KV_W, F32),
    ("ksl", KV_W, BF16), ("vsl", KV_W, BF16), ("kwn", KV_W, BF16), ("vwn", KV_W, BF16),
    ("gexp", GEXP_W, BF16), ("gq", GLA_QK_W, F32), ("gk", GLA_QK_W, F32),
    ("gv", GLA_W, F32), ("ga", LANES, F32), ("gr", GLA_W, F32), ("uv", 2 * SG_W, F32))
_INPROJ_W = sum(w for _, w, _ in _INPROJ_OUT)


def _inproj_kernel(x_ref, g_ref, w_ref, cos_ref, sin_ref, *out_refs):
    x = x_ref[...]
    hn = x * lax.rsqrt(jnp.mean(x * x, axis=-1, keepdims=True) + NORM_EPS) * g_ref[...]
    z = _dot(hn.astype(BF16), w_ref[...])
    cos = cos_ref[...]
    sin = sin_ref[...]
    off = 0
    for (name, width, dtype), o_ref in zip(_INPROJ_OUT, out_refs):
        if name in ("qpad", "ksl", "kwn"):
            for j in range(width // LANES):
                blk = _rope_lanes(z[:, off + j * LANES: off + (j + 1) * LANES], cos, sin)
                if name == "qpad":
                    blk = blk * (HEAD_DIM ** -0.5)
                o_ref[:, j * LANES:(j + 1) * LANES] = blk.astype(dtype)
        else:
            o_ref[...] = z[:, off:off + width].astype(dtype)
        off += width


def _inproj(x2, gain, w_p, cos_t, sin_t, seq, tm):
    T, D = x2.shape
    nper = seq // tm
    out_shape = [jax.ShapeDtypeStruct((T, w), dt) for _, w, dt in _INPROJ_OUT]
    out_specs = [pl.BlockSpec((tm, w), lambda i: (i, 0)) for _, w, _ in _INPROJ_OUT]
    return pl.pallas_call(
        _inproj_kernel,
        grid=(T // tm,),
        in_specs=[
            pl.BlockSpec((tm, D), lambda i: (i, 0)),
            pl.BlockSpec((1, D), lambda i: (0, 0)),
            pl.BlockSpec((D, _INPROJ_W), lambda i: (0, 0), pipeline_mode=pl.Buffered(1)),
            pl.BlockSpec((tm, LANES), lambda i: (i % nper, 0)),
            pl.BlockSpec((tm, LANES), lambda i: (i % nper, 0)),
        ],
        out_specs=out_specs,
        out_shape=out_shape,
        compiler_params=_cparams(("parallel",)),
        name="inproj",
    )(x2, gain, w_p, cos_t, sin_t)


def _prep_w_in(w_in):
    offs = np.cumsum((0,) + IN_SIZES)
    seg = [w_in[:, offs[k]:offs[k + 1]] for k in range(len(IN_SIZES))]
    (wq, wkc, wvc, wksl, wvsl, wkwn, wvwn, wg, wgq, wgk, wgv, wga, wgr, wuv) = seg
    D = w_in.shape[0]
    zeros64 = jnp.zeros((D, HEAD_DIM), w_in.dtype)
    qcols = []
    for h in range(NSA_HEADS):
        wh = wq[:, h * HEAD_DIM:(h + 1) * HEAD_DIM]
        qcols += [wh, zeros64] if h // NSA_HPG == 0 else [zeros64, wh]
    gcols = []
    for k in range(3):
        for h in range(NSA_HEADS):
            gcols.append(jnp.broadcast_to(wg[:, h * 3 + k][:, None], (D, HEAD_DIM)))
    wga_p = jnp.pad(wga, ((0, 0), (0, LANES - GLA_RANK)))
    return jnp.concatenate(qcols + [wkc, wvc, wksl, wvsl, wkwn, wvwn] + gcols
                           + [wgq, wgk, wgv, wga_p, wgr, wuv], axis=1).astype(BF16)


def _rope_tables(pos):
    half = HEAD_DIM // 2
    inv = 1.0 / (ROPE_THETA ** (jnp.arange(half, dtype=F32) / half))
    ang = pos.astype(F32)[:, None] * inv[None, :]
    cos = jnp.cos(ang)
    sin = jnp.sin(ang)
    cos_t = jnp.concatenate([cos, cos, cos, cos], axis=1)
    sin_t = jnp.concatenate([-sin, sin, -sin, sin], axis=1)
    return cos_t, sin_t


def _cmp_kernel(xk_ref, xv_ref, wek_ref, wev_ref, w1k_ref, w1v_ref, pk_ref, pv_ref,
                w2k_ref, w2v_ref, cos_ref, sin_ref, ok_ref, ov_ref):
    nrow = xk_ref.shape[1]
    row = lax.broadcasted_iota(jnp.int32, (nrow, LANES), 0)

    def compress(x_ref, we_ref, w1_ref, p_ref, w2_ref):
        h = _dot(x_ref[0].astype(BF16), we_ref[...])
        posb = _dot(p_ref[...].astype(BF16), w1_ref[...])[0:1, :]
        y = jnp.zeros((nrow, LANES), F32)
        for g in range(NSA_KV_GROUPS):
            a = h[:, g * CMP_HIDDEN:(g + 1) * CMP_HIDDEN]
            b = h[:, (NSA_KV_GROUPS + g) * CMP_HIDDEN:(NSA_KV_GROUPS + g + 1) * CMP_HIDDEN]
            hid = a + pltpu.roll(b, nrow - 1, 0) + posb
            y = y + _dot(_gelu_tanh(hid).astype(BF16), w2_ref[g])
        return y

    yk = _rope_lanes(compress(xk_ref, wek_ref, w1k_ref, pk_ref, w2k_ref), cos_ref[...], sin_ref[...])
    yv = compress(xv_ref, wev_ref, w1v_ref, pv_ref, w2v_ref)
    keep = row < nrow - 1
    ok_ref[0] = jnp.where(keep, yk, 0.0)
    ov_ref[0] = jnp.where(keep, yv, 0.0)


def _prep_cmp_weights(w1, w2, pos):
    half = CMP_LEN // CMP_STRIDE
    H = w1.shape[1]
    w1r = w1.reshape(half, CMP_STRIDE, HEAD_DIM, H)
    cols = []
    for a in range(half):
        for g in range(NSA_KV_GROUPS):
            blk = jnp.zeros((CMP_STRIDE, NSA_KV_GROUPS, HEAD_DIM, H), w1.dtype)
            blk = blk.at[:, g].set(w1r[a])
            cols.append(blk.reshape(CMP_STRIDE * KV_W, H))
    wexp = jnp.concatenate(cols, axis=1).astype(BF16)
    w2p = jnp.stack([jnp.pad(w2, ((0, 0), (g * HEAD_DIM, KV_W - (g + 1) * HEAD_DIM)))
                     for g in range(NSA_KV_GROUPS)]).astype(BF16)
    posf = jnp.pad(pos.reshape(1, CMP_LEN * HEAD_DIM), ((0, 7), (0, 0)))
    return wexp, w1.astype(BF16), posf, w2p


def _compress(kc, vc, wk, wv, cos_c, sin_c, batch, seq):
    nrow = seq // CMP_STRIDE
    xk = kc.reshape(batch, nrow, CMP_STRIDE * KV_W)
    xv = vc.reshape(batch, nrow, CMP_STRIDE * KV_W)
    wek, w1k, pk, w2k = wk
    wev, w1v, pv, w2v = wv
    full = lambda a: pl.BlockSpec(a.shape, lambda b: (0,) * a.ndim)
    xspec = pl.BlockSpec((1, nrow, CMP_STRIDE * KV_W), lambda b: (b, 0, 0))
    ospec = pl.BlockSpec((1, nrow, KV_W), lambda b: (b, 0, 0))
    return pl.pallas_call(
        _cmp_kernel,
        grid=(batch,),
        in_specs=[xspec, xspec, full(wek), full(wev), full(w1k), full(w1v), full(pk), full(pv),
                  full(w2k), full(w2v), full(cos_c), full(sin_c)],
        out_specs=[ospec, ospec],
        out_shape=[jax.ShapeDtypeStruct((batch, nrow, KV_W), F32)] * 2,
        compiler_params=_cparams(("parallel",)),
        name="compress",
    )(xk, xv, wek, wev, w1k, w1v, pk, pv, w2k, w2v, cos_c, sin_c)


def _unpad_heads(acc):
    qb = SEL_BLOCK
    lane = lax.broadcasted_iota(jnp.int32, (qb, LANES), 1)
    low = lane < HEAD_DIM
    outs = []
    for j in range(NSA_HEADS // 2):
        a = acc[(2 * j) * qb:(2 * j + 1) * qb]
        b = acc[(2 * j + 1) * qb:(2 * j + 2) * qb]
        if (2 * j) // NSA_HPG == 0:
            outs.append(jnp.where(low, a, pltpu.roll(b, HEAD_DIM, 1)))
        else:
            outs.append(jnp.where(low, pltpu.roll(a, HEAD_DIM, 1), b))
    return jnp.concatenate(outs, axis=1)


def _nsa_kernel(q_ref, g_ref, kc_ref, vc_ref, ksl_ref, vsl_ref, kwn_ref, vwn_ref,
                ktab_ref, kdiag_ref, kwtab_ref, qc_ref, qw_ref, eye_ref, impt_ref, o_ref,
                imp_scr, m_scr, acc_scr, win_scr, kd_scr, qs_scr, sc_scr, cmp_scr):
    qb = SEL_BLOCK
    rows = NSA_HEADS * qb
    pair = range(q_ref.shape[0])
    i = pl.program_id(1)
    t0 = i * qb
    t = t0 + lax.broadcasted_iota(jnp.int32, (rows, 1), 0) % qb
    ones = jnp.ones((SEL_CH, LANES), BF16)
    parts = [slice(p * rows // NSA_ROW_PARTS, (p + 1) * rows // NSA_ROW_PARTS) for p in range(NSA_ROW_PARTS)]
    qbf = [jnp.concatenate([q_ref[e, :, h * LANES:(h + 1) * LANES] for h in range(NSA_HEADS)], axis=0)
           for e in pair]

    nblk_w = WIN_KEYS // qb
    ib = jnp.minimum(i, nblk_w - 1)
    w0 = pl.multiple_of((i - ib) * qb, qb)
    toff = pl.multiple_of((nblk_w - 1 - ib) * qb, qb)
    kwmask = kwtab_ref[pl.ds(toff, WIN_KEYS), :]
    for e in pair:
        kw = jnp.concatenate([kwn_ref[e, 0, pl.ds(w0, WIN_KEYS), :], kwmask], axis=1)
        vw = jnp.concatenate([vwn_ref[e, 0, pl.ds(w0, WIN_KEYS), :], ones[0:WIN_KEYS]], axis=1)
        q_win = jnp.concatenate([qbf[e], qw_ref[...]], axis=1)
        for rs in parts:
            sw = _dot_nt(q_win[rs], kw)
            ew = jnp.exp(sw - jnp.max(sw, axis=1, keepdims=True)).astype(BF16)
            ow = _dot(ew, vw)
            win_scr[e, rs, :] = ow[:, 0:LANES] / ow[:, LANES:2 * LANES]

    ncmp = kc_ref.shape[2]
    cend = lax.broadcasted_iota(jnp.int32, (1, ncmp), 1) * CMP_STRIDE + (CMP_LEN - 1)
    vis = cend <= t
    imap = impt_ref[...].astype(BF16)
    imap2 = jnp.concatenate([imap, imap], axis=1)
    for e in pair:
        s = _dot_nt(qbf[e], kc_ref[e, 0].astype(BF16))
        sm = jnp.where(vis, s, NEG_BIG)
        mx = jnp.max(sm, axis=1, keepdims=True)
        ex = jnp.where(vis, jnp.exp(sm - mx), 0.0)
        d = jnp.sum(ex, axis=1, keepdims=True)
        p = ex / jnp.where(d > 0, d, 1.0)
        cmp_scr[e] = _dot(p.astype(BF16), vc_ref[e, 0].astype(BF16))
        psum = jnp.concatenate(
            [sum(p[(g * NSA_HPG + h) * qb:(g * NSA_HPG + h + 1) * qb] for h in range(NSA_HPG))
             for g in range(NSA_KV_GROUPS)], axis=0)
        p_hi = psum.astype(BF16)
        p_lo = (psum - p_hi.astype(F32)).astype(BF16)
        imp_scr[e] = _dot_nt(imap2, jnp.concatenate([p_hi, p_lo], axis=1))

    nsbp = imp_scr.shape[1]
    imps = [imp_scr[e] for e in pair]
    n_id = lax.broadcasted_iota(jnp.int32, (nsbp, 1), 0)
    valid = n_id <= i
    forced = (n_id == 0) | (n_id == i) | (n_id == i - 1)
    nforced = 1 + (i >= 1).astype(jnp.int32) + (i >= 2).astype(jnp.int32)

    def rank_body(j, cnts):
        cnts = list(cnts)
        for u in range(RANK_UNROLL):
            m = 1 + RANK_UNROLL * j + u
            tie = jnp.where(m < n_id, 1.0, 0.0)
            for e in pair:
                rowm = imp_scr[e, pl.ds(jnp.minimum(m, nsbp - 1), 1), :]
                rowm = jnp.where(m <= i - 2, rowm, NEG_BIG)
                cnts[e] = cnts[e] + jnp.where(rowm > imps[e], 1.0, 0.0) + jnp.where(rowm == imps[e], tie, 0.0)
        return tuple(cnts)

    ntrip = (jnp.maximum(i - 2, 0) + RANK_UNROLL - 1) // RANK_UNROLL
    cnts = lax.fori_loop(0, ntrip, rank_body, tuple(jnp.zeros(imps[0].shape, F32) for _ in pair))
    budget = (SEL_TOPK - nforced).astype(F32)

    for e in pair:
        sel = valid & (forced | (cnts[e] < budget))
        notsel_t = jnp.where(sel, 0.0, 1.0)
        notsel = _dot_tn(notsel_t, eye_ref[...])
        notsel = jnp.concatenate([notsel[g * qb:(g + 1) * qb]
                                  for g in range(NSA_KV_GROUPS) for _ in range(NSA_HPG)], axis=0)
        qs_scr[e] = jnp.concatenate([qbf[e], (notsel + qc_ref[...]).astype(BF16)], axis=1)
    m_scr[...] = jnp.full(m_scr.shape, NEG_BIG, F32)
    acc_scr[...] = jnp.zeros(acc_scr.shape, F32)
    last = i // (SEL_CH // qb)
    k0_last = pl.multiple_of(last * SEL_CH, SEL_CH)
    off = pl.multiple_of((i % (SEL_CH // qb)) * qb, qb)
    kd_scr[...] = ktab_ref[pl.ds(k0_last, SEL_CH), :]
    kd_scr[pl.ds(off, qb), :] = kd_scr[pl.ds(off, qb), :] + kdiag_ref[...]
    ncol = SEL_CH // LANES

    def score_body(c, carry):
        k0 = pl.multiple_of(c * SEL_CH, SEL_CH)
        kmask = jnp.where(c == last, kd_scr[...], ktab_ref[pl.ds(k0, SEL_CH), :])
        for e in pair:
            k = jnp.concatenate([ksl_ref[e, 0, pl.ds(k0, SEL_CH), :], kmask], axis=1)
            for rs in parts:
                sc = _dot_nt(qs_scr[e, rs, :], k)
                sc_scr[e, c, rs, :] = sc
                mx = m_scr[e, rs, :]
                for j in range(ncol):
                    mx = jnp.maximum(mx, sc[:, j * LANES:(j + 1) * LANES])
                m_scr[e, rs, :] = mx
        return carry

    lax.fori_loop(0, last + 1, score_body, 0)
    for e in pair:
        m_scr[e] = jnp.broadcast_to(jnp.max(m_scr[e], axis=1, keepdims=True), (rows, LANES))

    def value_body(c, carry):
        k0 = pl.multiple_of(c * SEL_CH, SEL_CH)
        for e in pair:
            v = jnp.concatenate([vsl_ref[e, 0, pl.ds(k0, SEL_CH), :], ones], axis=1)
            for rs in parts:
                mb = m_scr[e, rs, :]
                pr = jnp.concatenate([jnp.exp(sc_scr[e, c, rs, j * LANES:(j + 1) * LANES] - mb)
                                      for j in range(ncol)], axis=1).astype(BF16)
                acc_scr[e, rs, :] += _dot(pr, v)
        return carry

    lax.fori_loop(0, last + 1, value_body, 0)

    for e in pair:
        o_slc = acc_scr[e, :, 0:LANES] / acc_scr[e, :, LANES:2 * LANES]
        gate = jax.nn.sigmoid(g_ref[e].astype(F32))
        o_ref[e] = (gate[:, 0:NSA_W] * _unpad_heads(cmp_scr[e])
                    + gate[:, NSA_W:2 * NSA_W] * _unpad_heads(o_slc)
                    + gate[:, 2 * NSA_W:3 * NSA_W] * _unpad_heads(win_scr[e]))


def _nsa(qpad, gexp, kcmp, vcmp, ksl, vsl, kwn, vwn, consts, batch, seq):
    qb = SEL_BLOCK
    nq = seq // qb
    rows = NSA_HEADS * qb
    npair = NSA_PAIR if batch % NSA_PAIR == 0 else 1
    nbh = batch // npair
    tok = lambda a: a.reshape(npair, nbh * seq, a.shape[-1])
    per = lambda a: a.reshape(npair, nbh, a.shape[-2], KV_W)
    tok_spec = lambda w: pl.BlockSpec((npair, qb, w), lambda bh, i: (0, bh * nq + i, 0))
    per_spec = lambda n: pl.BlockSpec((npair, 1, n, KV_W), lambda bh, i: (0, bh, 0, 0))
    full = lambda a: pl.BlockSpec(a.shape, lambda bh, i: (0,) * a.ndim)
    out = pl.pallas_call(
        _nsa_kernel,
        grid=(nbh, nq),
        in_specs=[
            tok_spec(QPAD_W), tok_spec(GEXP_W),
            per_spec(kcmp.shape[1]), per_spec(vcmp.shape[1]),
            per_spec(seq), per_spec(seq), per_spec(seq), per_spec(seq),
        ] + [full(c) for c in consts],
        out_specs=tok_spec(NSA_W),
        out_shape=jax.ShapeDtypeStruct((npair, nbh * seq, NSA_W), F32),
        scratch_shapes=[
            pltpu.VMEM((npair, consts[-1].shape[0], NSA_KV_GROUPS * qb), F32),
            pltpu.VMEM((npair, rows, LANES), F32),
            pltpu.VMEM((npair, rows, 2 * LANES), F32),
            pltpu.VMEM((npair, rows, LANES), F32),
            pltpu.VMEM((SEL_CH, LANES), BF16),
            pltpu.VMEM((npair, rows, 2 * LANES), BF16),
            pltpu.VMEM((npair, seq // SEL_CH, rows, SEL_CH), F32),
            pltpu.VMEM((npair, rows, LANES), F32),
        ],
        compiler_params=_cparams(("parallel", "arbitrary")),
        name="nsa",
    )(tok(qpad), tok(gexp), per(kcmp), per(vcmp), per(ksl.reshape(batch, seq, KV_W)),
      per(vsl.reshape(batch, seq, KV_W)), per(kwn.reshape(batch, seq, KV_W)),
      per(vwn.reshape(batch, seq, KV_W)), *consts)
    return out.reshape(batch * seq, NSA_W)


def _nsa_constants(seq):
    qb = SEL_BLOCK
    nc_pad = seq // CMP_STRIDE
    nsb = seq // qb
    assert nsb <= qb
    nsb_pad = max(nsb, 8)
    cs = np.arange(nc_pad) * CMP_STRIDE
    ce = cs + CMP_LEN
    bs = np.arange(nsb_pad) * qb
    be = bs + qb
    ov = np.clip(np.minimum(ce[None, :], be[:, None]) - np.maximum(cs[None, :], bs[:, None]), 0, None)
    impt = ov / CMP_LEN
    pos = np.arange(seq)
    lane = np.arange(LANES)[None, :]
    ktab = np.where(lane == pos[:, None] // qb, NEG_BIG, 0.0)
    kdiag = np.where(lane == qb + np.arange(qb)[:, None], NEG_BIG, 0.0)
    last = WIN_KEYS // qb - 1
    x = np.arange(WIN_KEYS + last * qb)[:, None]
    kb, j = x // qb, x % qb
    kwtab = np.where((((kb == 0) | (kb > last)) & (lane == 0)) | ((kb == 1) & (lane == j))
                     | ((kb == last) & (lane == qb + j)), NEG_BIG, 0.0)
    r = (np.arange(NSA_HEADS * qb) % qb)[:, None]
    upper = (lane >= qb) & (lane - qb > r)
    qc = np.where(upper, 1.0, 0.0)
    qw = np.where(upper | ((lane < qb) & (lane <= r)), 1.0, 0.0)
    eye = np.eye(nsb_pad, LANES)
    return (jnp.asarray(ktab, BF16), jnp.asarray(kdiag, BF16), jnp.asarray(kwtab, BF16),
            jnp.asarray(qc, F32), jnp.asarray(qw, BF16),
            jnp.asarray(eye, F32), jnp.asarray(impt, F32))


def _gla_kernel(q_ref, k_ref, v_ref, a_ref, r_ref, wup_ref, bup_ref, ng_ref, o_ref, st_ref):
    blk = q_ref.shape[0]
    nsub = blk // GLA_SUB

    @pl.when(pl.program_id(1) == 0)
    def _():
        st_ref[...] = jnp.zeros(st_ref.shape, F32)

    x = _dot(a_ref[...], wup_ref[...], HI) + bup_ref[...]
    g = (jnp.minimum(x, 0.0) - jnp.log1p(jnp.exp(-jnp.abs(x)))) / GLA_TAU
    ri = lax.broadcasted_iota(jnp.int32, (blk, blk), 0)
    ci = lax.broadcasted_iota(jnp.int32, (blk, blk), 1)
    b = _dot(jnp.where(ci <= ri, 1.0, 0.0), g, HI)
    blast = b[blk - 1:blk, :]
    q = q_ref[...] * (GLA_DK ** -0.5)
    k = k_ref[...]
    v = v_ref[...]
    vb = v.astype(BF16)

    st = st_ref[...]
    o = _dot_nt((q * jnp.exp(b)).astype(BF16), st.astype(BF16))
    khat = k * jnp.exp(blast - b)
    srow = lax.broadcasted_iota(jnp.int32, st.shape, 0) // GLA_DV
    scol = lax.broadcasted_iota(jnp.int32, st.shape, 1) // GLA_DK
    st_ref[...] = st * jnp.exp(blast) + jnp.where(srow == scol, _dot_tn(v, khat), 0.0)

    krow = lax.broadcasted_iota(jnp.int32, (blk, 1), 0)
    lane_qk = lax.broadcasted_iota(jnp.int32, (1, GLA_QK_W), 1) // GLA_DK
    lane_v = lax.broadcasted_iota(jnp.int32, (1, GLA_W), 1) // GLA_DV
    qrow = lax.broadcasted_iota(jnp.int32, (GLA_HEADS * GLA_SUB, 1), 0) % GLA_SUB
    kcol = lax.broadcasted_iota(jnp.int32, (1, blk), 1)
    for c in range(nsub):
        lo = c * GLA_SUB
        ref_b = b[lo:lo + 1, :]
        qt = q[lo:lo + GLA_SUB] * jnp.exp(b[lo:lo + GLA_SUB] - ref_b)
        kt = k * jnp.exp(jnp.where(krow < lo + GLA_SUB, ref_b - b, 0.0))
        qs = jnp.concatenate([jnp.where(lane_qk == h, qt, 0.0) for h in range(GLA_HEADS)], axis=0)
        a = _dot_nt(qs.astype(BF16), kt.astype(BF16))
        a = jnp.where(kcol <= lo + qrow, a, 0.0)
        r = _dot(a.astype(BF16), vb)
        oi = sum(jnp.where(lane_v == h, r[h * GLA_SUB:(h + 1) * GLA_SUB], 0.0) for h in range(GLA_HEADS))
        o_ref[lo:lo + GLA_SUB, :] = o[lo:lo + GLA_SUB] + oi

    o = o_ref[...]
    gi = lax.broadcasted_iota(jnp.int32, (GLA_W, GLA_W), 0) // GLA_DV
    gj = lax.broadcasted_iota(jnp.int32, (GLA_W, GLA_W), 1) // GLA_DV
    ms = _dot(o * o, jnp.where(gi == gj, 1.0 / GLA_DV, 0.0), HI)
    rr = r_ref[...]
    o_ref[...] = o * lax.rsqrt(ms + NORM_EPS) * ng_ref[...] * (rr * jax.nn.sigmoid(rr))


def _gla(gq, gk, gv, ga, gr, wup, bup, ng, batch, seq):
    blk = min(GLA_BLK, seq)
    nb = seq // blk
    tok = lambda w: pl.BlockSpec((blk, w), lambda b, j: (b * nb + j, 0))
    full = lambda a: pl.BlockSpec(a.shape, lambda b, j: (0,) * a.ndim)
    return pl.pallas_call(
        _gla_kernel,
        grid=(batch, nb),
        in_specs=[tok(GLA_QK_W), tok(GLA_QK_W), tok(GLA_W), tok(LANES), tok(GLA_W),
                  full(wup), full(bup), full(ng)],
        out_specs=tok(GLA_W),
        out_shape=jax.ShapeDtypeStruct((batch * seq, GLA_W), F32),
        scratch_shapes=[pltpu.VMEM((GLA_W, GLA_QK_W), F32)],
        compiler_params=_cparams(("parallel", "arbitrary")),
        name="gla",
    )(gq, gk, gv, ga, gr, wup, bup, ng)


def _sg_kernel(uv_ref, lg_ref, lb_ref, w_ref, bias_ref, o_ref):
    n = w_ref.shape[1]
    ri = lax.broadcasted_iota(jnp.int32, (n, n), 0)
    ci = lax.broadcasted_iota(jnp.int32, (n, n), 1)
    lane_g = lax.broadcasted_iota(jnp.int32, (1, SG_W), 1) // SG_CH
    wts = [jnp.where(ci <= ri, w_ref[g], 0.0).astype(BF16) for g in range(SG_GROUPS)]
    for c in range(uv_ref.shape[0] // n):
        rs = slice(c * n, (c + 1) * n)
        a = _gelu_tanh(uv_ref[rs, :])
        u = a[:, :SG_W]
        v = a[:, SG_W:]
        mu = jnp.mean(v, axis=-1, keepdims=True)
        var = jnp.mean(jnp.square(v - mu), axis=-1, keepdims=True)
        vn = ((v - mu) * lax.rsqrt(var + NORM_EPS) * lg_ref[...] + lb_ref[...]).astype(BF16)
        s = bias_ref[...]
        for g in range(SG_GROUPS):
            s = s + jnp.where(lane_g == g, _dot(wts[g], vn), 0.0)
        o_ref[rs, :] = u * s


def _spatial_gating(uv, ln_g, ln_b, w_s, bias_exp):
    T = uv.shape[0]
    full = lambda a: pl.BlockSpec(a.shape, lambda i: (0,) * a.ndim)
    return pl.pallas_call(
        _sg_kernel,
        grid=(T // SG_BLK,),
        in_specs=[pl.BlockSpec((SG_BLK, 2 * SG_W), lambda i: (i, 0)),
                  full(ln_g), full(ln_b), full(w_s), full(bias_exp)],
        out_specs=pl.BlockSpec((SG_BLK, SG_W), lambda i: (i, 0)),
        out_shape=jax.ShapeDtypeStruct((T, SG_W), F32),
        compiler_params=_cparams(("parallel",)),
        name="spatial_gating",
    )(uv, ln_g, ln_b, w_s, bias_exp)


def _outffn_kernel(x_ref, on_ref, og_ref, os_ref, wo_ref, fg_ref, wg_ref, wu_ref, wd_ref,
                   fin_ref, o_ref, *, final):
    x = x_ref[...]
    x = x + _dot(on_ref[...].astype(BF16), wo_ref[0:NSA_W, :])
    x = x + _dot(og_ref[...].astype(BF16), wo_ref[NSA_W:NSA_W + GLA_W, :])
    x = x + _dot(os_ref[...].astype(BF16), wo_ref[NSA_W + GLA_W:, :])
    hn = (x * lax.rsqrt(jnp.mean(x * x, axis=-1, keepdims=True) + NORM_EPS) * fg_ref[...]).astype(BF16)
    gt = _dot(hn, wg_ref[...])
    up = _dot(hn, wu_ref[...])
    y = x + _dot((gt * jax.nn.sigmoid(gt) * up).astype(BF16), wd_ref[...])
    if final:
        y = y * lax.rsqrt(jnp.mean(y * y, axis=-1, keepdims=True) + NORM_EPS) * fin_ref[...]
    o_ref[...] = y


def _outffn(x2, o_nsa, o_gla, o_sg, wo, fgain, wg, wu, wd, fin, tm, final):
    T, D = x2.shape
    row = lambda w: pl.BlockSpec((tm, w), lambda i: (i, 0))
    const = lambda a: pl.BlockSpec(a.shape, lambda i: (0,) * a.ndim, pipeline_mode=pl.Buffered(1))
    return pl.pallas_call(
        functools.partial(_outffn_kernel, final=final),
        grid=(T // tm,),
        in_specs=[row(D), row(NSA_W), row(GLA_W), row(SG_W), const(wo), const(fgain),
                  const(wg), const(wu), const(wd), const(fin)],
        out_specs=row(D),
        out_shape=jax.ShapeDtypeStruct((T, D), F32),
        compiler_params=_cparams(("parallel",)),
        name="outproj_ffn",
    )(x2, o_nsa, o_gla, o_sg, wo, fgain, wg, wu, wd, fin)


def kernel(x, attn_norm, w_in, cmp_pos_k, cmp_w1_k, cmp_w2_k, cmp_pos_v, cmp_w1_v, cmp_w2_v,
           gla_w_up, gla_b_up, gla_norm, sg_ln_g, sg_ln_b, sg_w, sg_b, w_out,
           ffn_norm, w_gate, w_up, w_down, final_norm):
    batch, seq, d_model = x.shape
    depth = w_in.shape[0]
    T = batch * seq
    assert seq % GLA_BLK == 0 and seq >= WIN_KEYS and seq % SEL_CH == 0
    tm_in = 512
    tm_ffn = 512

    cos_t, sin_t = _rope_tables(jnp.arange(seq))
    nrow = seq // CMP_STRIDE
    cos_c, sin_c = _rope_tables(jnp.arange(nrow) * CMP_STRIDE + CMP_LEN - 1)
    nsa_consts = _nsa_constants(seq)

    x2 = x.reshape(T, d_model)
    for l in range(depth):
        (qpad, kc, vc, ksl, vsl, kwn, vwn, gexp, gq, gk, gv, ga, gr, uv) = _inproj(
            x2, attn_norm[l][None, :], _prep_w_in(w_in[l]), cos_t, sin_t, seq, tm_in)
        kcmp, vcmp = _compress(
            kc, vc, _prep_cmp_weights(cmp_w1_k[l], cmp_w2_k[l], cmp_pos_k[l]),
            _prep_cmp_weights(cmp_w1_v[l], cmp_w2_v[l], cmp_pos_v[l]), cos_c, sin_c, batch, seq)
        o_nsa = _nsa(qpad, gexp, kcmp, vcmp, ksl, vsl, kwn, vwn, nsa_consts, batch, seq)
        wup = jnp.pad(gla_w_up[l], ((0, LANES - GLA_RANK), (0, 0)))
        o_gla = _gla(gq, gk, gv, ga, gr, wup, gla_b_up[l][None, :],
                     jnp.tile(gla_norm[l], GLA_HEADS)[None, :], batch, seq)
        bias_exp = jnp.repeat(sg_b[l].T, SG_CH, axis=1)
        o_sg = _spatial_gating(uv, sg_ln_g[l][None, :], sg_ln_b[l][None, :], sg_w[l], bias_exp)
        x2 = _outffn(x2, o_nsa, o_gla, o_sg, w_out[l].astype(BF16), ffn_norm[l][None, :],
                     w_gate[l].astype(BF16), w_up[l].astype(BF16), w_down[l].astype(BF16),
                     final_norm[None, :], tm_ffn, final=(l == depth - 1))
    return x2.reshape(batch, seq, d_model)
```

```python
import functools
import math

import numpy as np
import jax
import jax.numpy as jnp
from jax import lax
from jax.experimental import pallas as pl
from jax.experimental.pallas import tpu as pltpu

HEAD_DIM = 64
NSA_HEADS = 8
NSA_KV_GROUPS = 2
NSA_HPG = NSA_HEADS // NSA_KV_GROUPS
CMP_LEN = 32
CMP_STRIDE = 16
CMP_HIDDEN = 256
SEL_BLOCK = 64
SEL_TOPK = 16
WINDOW = 512
GLA_HEADS = 4
GLA_DK = 32
GLA_DV = 64
GLA_RANK = 16
GLA_TAU = 16.0
SG_GROUPS = 4
SG_CH = 64
SG_CHUNK = 128
NSA_W = NSA_HEADS * HEAD_DIM
GLA_W = GLA_HEADS * GLA_DV
SG_W = SG_GROUPS * SG_CH
KV_W = NSA_KV_GROUPS * HEAD_DIM
IN_SIZES = (NSA_W, KV_W, KV_W, KV_W, KV_W, KV_W, KV_W, NSA_HEADS * 3,
            GLA_HEADS * GLA_DK, GLA_HEADS * GLA_DK, GLA_W, GLA_RANK, GLA_W, 2 * SG_W)
ROPE_THETA = 10000.0
NORM_EPS = 1e-6

LANES = 128
NEG_BIG = -1e30
VMEM_LIMIT = 56 * 1024 * 1024

QPAD_W = NSA_HEADS * LANES
GEXP_W = 3 * NSA_W
GLA_QK_W = GLA_HEADS * GLA_DK
GLA_BLK = 256
GLA_SUB = 16
SEL_CH = 1024
WIN_KEYS = WINDOW + 2 * SEL_BLOCK
NSA_ROW_PARTS = 2
RANK_UNROLL = 4
SG_BLK = 4 * SG_CHUNK
NSA_PAIR = 2
GLA_PAIR = 2

HI = lax.Precision.HIGHEST
F32 = jnp.float32
BF16 = jnp.bfloat16


def _cparams(sem):
    return pltpu.CompilerParams(dimension_semantics=sem, vmem_limit_bytes=VMEM_LIMIT)


def _dot(a, b, precision=None):
    return jnp.dot(a, b, preferred_element_type=F32, precision=precision)


def _dot_nt(a, b, precision=None):
    return lax.dot_general(a, b, (((1,), (1,)), ((), ())),
                           preferred_element_type=F32, precision=precision)


def _dot_tn(a, b, precision=None):
    return lax.dot_general(a, b, (((0,), (0,)), ((), ())),
                           preferred_element_type=F32, precision=precision)


def _gelu_tanh(x):
    c = math.sqrt(2.0 / math.pi)
    return 0.5 * x * (1.0 + jnp.tanh(c * (x + 0.044715 * (x * x * x))))


def _rope_lanes(x, cos, sin_signed):
    n = x.shape[-1]
    lane = lax.broadcasted_iota(jnp.int32, x.shape, 1)
    first_half = (lane % HEAD_DIM) < (HEAD_DIM // 2)
    partner = jnp.where(first_half,
                        pltpu.roll(x, n - HEAD_DIM // 2, 1),
                        pltpu.roll(x, HEAD_DIM // 2, 1))
    return x * cos + partner * sin_signed


_INPROJ_OUT = (
    ("qpad", QPAD_W, BF16), ("kc", KV_W, F32), ("vc", KV_W, F32),
    ("ksl", KV_W, BF16), ("vsl", KV_W, BF16), ("kwn", KV_W, BF16), ("vwn", KV_W, BF16),
    ("gts", LANES, F32), ("gq", GLA_QK_W, F32), ("gk", GLA_QK_W, F32),
    ("gv", GLA_W, F32), ("ga", LANES, F32), ("gr", GLA_W, F32), ("uv", 2 * SG_W, F32))
_INPROJ_W = sum(w for _, w, _ in _INPROJ_OUT)


def _inproj_kernel(x_ref, g_ref, w_ref, cos_ref, sin_ref, *out_refs):
    x = x_ref[...]
    hn = x * lax.rsqrt(jnp.mean(x * x, axis=-1, keepdims=True) + NORM_EPS) * g_ref[...]
    z = _dot(hn.astype(BF16), w_ref[...])
    cos = cos_ref[...]
    sin = sin_ref[...]
    off = 0
    for (name, width, dtype), o_ref in zip(_INPROJ_OUT, out_refs):
        if name in ("qpad", "ksl", "kwn"):
            for j in range(width // LANES):
                blk = _rope_lanes(z[:, off + j * LANES: off + (j + 1) * LANES], cos, sin)
                if name == "qpad":
                    blk = blk * (HEAD_DIM ** -0.5)
                o_ref[:, j * LANES:(j + 1) * LANES] = blk.astype(dtype)
        else:
            o_ref[...] = z[:, off:off + width].astype(dtype)
        off += width


def _inproj(x2, gain, w_p, cos_t, sin_t, seq, tm):
    T, D = x2.shape
    nper = seq // tm
    out_shape = [jax.ShapeDtypeStruct((T, w), dt) for _, w, dt in _INPROJ_OUT]
    out_specs = [pl.BlockSpec((tm, w), lambda i: (i, 0)) for _, w, _ in _INPROJ_OUT]
    return pl.pallas_call(
        _inproj_kernel,
        grid=(T // tm,),
        in_specs=[
            pl.BlockSpec((tm, D), lambda i: (i, 0)),
            pl.BlockSpec((1, D), lambda i: (0, 0)),
            pl.BlockSpec((D, _INPROJ_W), lambda i: (0, 0), pipeline_mode=pl.Buffered(1)),
            pl.BlockSpec((tm, LANES), lambda i: (i % nper, 0)),
            pl.BlockSpec((tm, LANES), lambda i: (i % nper, 0)),
        ],
        out_specs=out_specs,
        out_shape=out_shape,
        compiler_params=_cparams(("parallel",)),
        name="inproj",
    )(x2, gain, w_p, cos_t, sin_t)


def _prep_w_in(w_in):
    offs = np.cumsum((0,) + IN_SIZES)
    seg = [w_in[:, offs[k]:offs[k + 1]] for k in range(len(IN_SIZES))]
    (wq, wkc, wvc, wksl, wvsl, wkwn, wvwn, wg, wgq, wgk, wgv, wga, wgr, wuv) = seg
    D = w_in.shape[0]
    zeros64 = jnp.zeros((D, HEAD_DIM), w_in.dtype)
    qcols = []
    for h in range(NSA_HEADS):
        wh = wq[:, h * HEAD_DIM:(h + 1) * HEAD_DIM]
        qcols += [wh, zeros64] if h // NSA_HPG == 0 else [zeros64, wh]
    gcols = [jnp.pad(wg, ((0, 0), (0, LANES - wg.shape[1])))]
    wga_p = jnp.pad(jnp.concatenate([wga] * 6, axis=1), ((0, 0), (0, LANES - 6 * GLA_RANK)))
    return jnp.concatenate(qcols + [wkc, wvc, wksl, wvsl, wkwn, wvwn] + gcols
                           + [wgq, wgk, wgv, wga_p, wgr, wuv], axis=1).astype(BF16)


def _rope_tables(pos):
    half = HEAD_DIM // 2
    inv = 1.0 / (ROPE_THETA ** (jnp.arange(half, dtype=F32) / half))
    ang = pos.astype(F32)[:, None] * inv[None, :]
    cos = jnp.cos(ang)
    sin = jnp.sin(ang)
    cos_t = jnp.concatenate([cos, cos, cos, cos], axis=1)
    sin_t = jnp.concatenate([-sin, sin, -sin, sin], axis=1)
    return cos_t, sin_t


def _cmp_kernel(xk_ref, xv_ref, wek_ref, wev_ref, w1k_ref, w1v_ref, pk_ref, pv_ref,
                w2k_ref, w2v_ref, cos_ref, sin_ref, ok_ref, ov_ref):
    nrow = xk_ref.shape[1]
    row = lax.broadcasted_iota(jnp.int32, (nrow, LANES), 0)

    def compress(x_ref, we_ref, w1_ref, p_ref, w2_ref):
        h = _dot(x_ref[0].astype(BF16), we_ref[...])
        posb = _dot(p_ref[...].astype(BF16), w1_ref[...])[0:1, :]
        y = jnp.zeros((nrow, LANES), F32)
        for g in range(NSA_KV_GROUPS):
            a = h[:, g * CMP_HIDDEN:(g + 1) * CMP_HIDDEN]
            b = h[:, (NSA_KV_GROUPS + g) * CMP_HIDDEN:(NSA_KV_GROUPS + g + 1) * CMP_HIDDEN]
            hid = a + pltpu.roll(b, nrow - 1, 0) + posb
            y = y + _dot(_gelu_tanh(hid).astype(BF16), w2_ref[g])
        return y

    yk = _rope_lanes(compress(xk_ref, wek_ref, w1k_ref, pk_ref, w2k_ref), cos_ref[...], sin_ref[...])
    yv = compress(xv_ref, wev_ref, w1v_ref, pv_ref, w2v_ref)
    keep = row < nrow - 1
    ok_ref[0] = jnp.where(keep, yk, 0.0)
    ov_ref[0] = jnp.where(keep, yv, 0.0)


def _prep_cmp_weights(w1, w2, pos):
    half = CMP_LEN // CMP_STRIDE
    H = w1.shape[1]
    w1r = w1.reshape(half, CMP_STRIDE, HEAD_DIM, H)
    cols = []
    for a in range(half):
        for g in range(NSA_KV_GROUPS):
            blk = jnp.zeros((CMP_STRIDE, NSA_KV_GROUPS, HEAD_DIM, H), w1.dtype)
            blk = blk.at[:, g].set(w1r[a])
            cols.append(blk.reshape(CMP_STRIDE * KV_W, H))
    wexp = jnp.concatenate(cols, axis=1).astype(BF16)
    w2p = jnp.stack([jnp.pad(w2, ((0, 0), (g * HEAD_DIM, KV_W - (g + 1) * HEAD_DIM)))
                     for g in range(NSA_KV_GROUPS)]).astype(BF16)
    posf = jnp.pad(pos.reshape(1, CMP_LEN * HEAD_DIM), ((0, 7), (0, 0)))
    return wexp, w1.astype(BF16), posf, w2p


def _compress(kc, vc, wk, wv, cos_c, sin_c, batch, seq):
    nrow = seq // CMP_STRIDE
    xk = kc.reshape(batch, nrow, CMP_STRIDE * KV_W)
    xv = vc.reshape(batch, nrow, CMP_STRIDE * KV_W)
    wek, w1k, pk, w2k = wk
    wev, w1v, pv, w2v = wv
    full = lambda a: pl.BlockSpec(a.shape, lambda b: (0,) * a.ndim)
    xspec = pl.BlockSpec((1, nrow, CMP_STRIDE * KV_W), lambda b: (b, 0, 0))
    ospec = pl.BlockSpec((1, nrow, KV_W), lambda b: (b, 0, 0))
    return pl.pallas_call(
        _cmp_kernel,
        grid=(batch,),
        in_specs=[xspec, xspec, full(wek), full(wev), full(w1k), full(w1v), full(pk), full(pv),
                  full(w2k), full(w2v), full(cos_c), full(sin_c)],
        out_specs=[ospec, ospec],
        out_shape=[jax.ShapeDtypeStruct((batch, nrow, KV_W), F32)] * 2,
        compiler_params=_cparams(("parallel",)),
        name="compress",
    )(xk, xv, wek, wev, w1k, w1v, pk, pv, w2k, w2v, cos_c, sin_c)


def _unpad_heads(acc):
    qb = SEL_BLOCK
    lane = lax.broadcasted_iota(jnp.int32, (qb, LANES), 1)
    low = lane < HEAD_DIM
    outs = []
    for j in range(NSA_HEADS // 2):
        a = acc[(2 * j) * qb:(2 * j + 1) * qb]
        b = acc[(2 * j + 1) * qb:(2 * j + 2) * qb]
        if (2 * j) // NSA_HPG == 0:
            outs.append(jnp.where(low, a, pltpu.roll(b, HEAD_DIM, 1)))
        else:
            outs.append(jnp.where(low, pltpu.roll(a, HEAD_DIM, 1), b))
    return jnp.concatenate(outs, axis=1)


def _nsa_kernel(q_ref, g_ref, kc_ref, vc_ref, ksl_ref, vsl_ref, kwn_ref, vwn_ref,
                ktab_ref, kdiag_ref, kwtab_ref, qc_ref, qw_ref, eye_ref, gexp_ref, impt_ref, o_ref,
                imp_scr, m_scr, acc_scr, win_scr, kd_scr, qs_scr, sc_scr, cmp_scr):
    qb = SEL_BLOCK
    rows = NSA_HEADS * qb
    pair = range(q_ref.shape[0])
    i = pl.program_id(1)
    t0 = i * qb
    t = t0 + lax.broadcasted_iota(jnp.int32, (rows, 1), 0) % qb
    ones = jnp.ones((SEL_CH, LANES), BF16)
    parts = [slice(p * rows // NSA_ROW_PARTS, (p + 1) * rows // NSA_ROW_PARTS) for p in range(NSA_ROW_PARTS)]
    qbf = [jnp.concatenate([q_ref[e, :, h * LANES:(h + 1) * LANES] for h in range(NSA_HEADS)], axis=0)
           for e in pair]

    nblk_w = WIN_KEYS // qb
    ib = jnp.minimum(i, nblk_w - 1)
    w0 = pl.multiple_of((i - ib) * qb, qb)
    toff = pl.multiple_of((nblk_w - 1 - ib) * qb, qb)
    kwmask = kwtab_ref[pl.ds(toff, WIN_KEYS), :]
    for e in pair:
        kw = jnp.concatenate([kwn_ref[e, 0, pl.ds(w0, WIN_KEYS), :], kwmask], axis=1)
        vw = jnp.concatenate([vwn_ref[e, 0, pl.ds(w0, WIN_KEYS), :], ones[0:WIN_KEYS]], axis=1)
        q_win = jnp.concatenate([qbf[e], qw_ref[...]], axis=1)
        for rs in parts:
            sw = _dot_nt(q_win[rs], kw)
            ew = jnp.exp(sw - jnp.max(sw, axis=1, keepdims=True)).astype(BF16)
            ow = _dot(ew, vw)
            win_scr[e, rs, :] = ow[:, 0:LANES] / ow[:, LANES:2 * LANES]

    ncmp = kc_ref.shape[2]
    cend = lax.broadcasted_iota(jnp.int32, (1, ncmp), 1) * CMP_STRIDE + (CMP_LEN - 1)
    vis = cend <= t
    imap = impt_ref[...].astype(BF16)
    imap2 = jnp.concatenate([imap, imap], axis=1)
    for e in pair:
        s = _dot_nt(qbf[e], kc_ref[e, 0].astype(BF16))
        sm = jnp.where(vis, s, NEG_BIG)
        mx = jnp.max(sm, axis=1, keepdims=True)
        ex = jnp.where(vis, jnp.exp(sm - mx), 0.0)
        d = jnp.sum(ex, axis=1, keepdims=True)
        p = ex / jnp.where(d > 0, d, 1.0)
        cmp_scr[e] = _dot(p.astype(BF16), vc_ref[e, 0].astype(BF16))
        psum = jnp.concatenate(
            [sum(p[(g * NSA_HPG + h) * qb:(g * NSA_HPG + h + 1) * qb] for h in range(NSA_HPG))
             for g in range(NSA_KV_GROUPS)], axis=0)
        p_hi = psum.astype(BF16)
        p_lo = (psum - p_hi.astype(F32)).astype(BF16)
        imp_scr[e] = _dot_nt(imap2, jnp.concatenate([p_hi, p_lo], axis=1))

    nsbp = imp_scr.shape[1]
    imps = [imp_scr[e] for e in pair]
    n_id = lax.broadcasted_iota(jnp.int32, (nsbp, 1), 0)
    valid = n_id <= i
    forced = (n_id == 0) | (n_id == i) | (n_id == i - 1)
    nforced = 1 + (i >= 1).astype(jnp.int32) + (i >= 2).astype(jnp.int32)

    def rank_body(j, cnts):
        cnts = list(cnts)
        for u in range(RANK_UNROLL):
            m = 1 + RANK_UNROLL * j + u
            tie = jnp.where(m < n_id, 1.0, 0.0)
            for e in pair:
                rowm = imp_scr[e, pl.ds(jnp.minimum(m, nsbp - 1), 1), :]
                rowm = jnp.where(m <= i - 2, rowm, NEG_BIG)
                cnts[e] = cnts[e] + jnp.where(rowm > imps[e], 1.0, 0.0) + jnp.where(rowm == imps[e], tie, 0.0)
        return tuple(cnts)

    ntrip = (jnp.maximum(i - 2, 0) + RANK_UNROLL - 1) // RANK_UNROLL
    cnts = lax.fori_loop(0, ntrip, rank_body, tuple(jnp.zeros(imps[0].shape, F32) for _ in pair))
    budget = (SEL_TOPK - nforced).astype(F32)

    for e in pair:
        sel = valid & (forced | (cnts[e] < budget))
        notsel_t = jnp.where(sel, 0.0, 1.0)
        notsel = _dot_tn(notsel_t, eye_ref[...])
        notsel = jnp.concatenate([notsel[g * qb:(g + 1) * qb]
                                  for g in range(NSA_KV_GROUPS) for _ in range(NSA_HPG)], axis=0)
        qs_scr[e] = jnp.concatenate([qbf[e], (notsel + qc_ref[...]).astype(BF16)], axis=1)
    m_scr[...] = jnp.full(m_scr.shape, NEG_BIG, F32)
    acc_scr[...] = jnp.zeros(acc_scr.shape, F32)
    last = i // (SEL_CH // qb)
    k0_last = pl.multiple_of(last * SEL_CH, SEL_CH)
    off = pl.multiple_of((i % (SEL_CH // qb)) * qb, qb)
    kd_scr[...] = ktab_ref[pl.ds(k0_last, SEL_CH), :]
    kd_scr[pl.ds(off, qb), :] = kd_scr[pl.ds(off, qb), :] + kdiag_ref[...]
    ncol = SEL_CH // LANES

    def score_body(c, carry):
        k0 = pl.multiple_of(c * SEL_CH, SEL_CH)
        kmask = jnp.where(c == last, kd_scr[...], ktab_ref[pl.ds(k0, SEL_CH), :])
        for e in pair:
            k = jnp.concatenate([ksl_ref[e, 0, pl.ds(k0, SEL_CH), :], kmask], axis=1)
            for rs in parts:
                sc = _dot_nt(qs_scr[e, rs, :], k)
                sc_scr[e, c, rs, :] = sc
                mx = m_scr[e, rs, :]
                for j in range(ncol):
                    mx = jnp.maximum(mx, sc[:, j * LANES:(j + 1) * LANES])
                m_scr[e, rs, :] = mx
        return carry

    lax.fori_loop(0, last + 1, score_body, 0)
    for e in pair:
        m_scr[e] = jnp.broadcast_to(jnp.max(m_scr[e], axis=1, keepdims=True), (rows, LANES))

    def value_body(c, carry):
        k0 = pl.multiple_of(c * SEL_CH, SEL_CH)
        for e in pair:
            v = jnp.concatenate([vsl_ref[e, 0, pl.ds(k0, SEL_CH), :], ones], axis=1)
            for rs in parts:
                mb = m_scr[e, rs, :]
                pr = jnp.concatenate([jnp.exp(sc_scr[e, c, rs, j * LANES:(j + 1) * LANES] - mb)
                                      for j in range(ncol)], axis=1).astype(BF16)
                acc_scr[e, rs, :] += _dot(pr, v)
        return carry

    lax.fori_loop(0, last + 1, value_body, 0)

    for e in pair:
        o_slc = acc_scr[e, :, 0:LANES] / acc_scr[e, :, LANES:2 * LANES]
        g_hi = jax.nn.sigmoid(g_ref[e])
        g_lo = g_hi - g_hi.astype(BF16).astype(F32)
        gate = _dot(jnp.concatenate([g_hi.astype(BF16), g_lo.astype(BF16)], axis=1), gexp_ref[...])
        o_ref[e] = (gate[:, 0:NSA_W] * _unpad_heads(cmp_scr[e])
                    + gate[:, NSA_W:2 * NSA_W] * _unpad_heads(o_slc)
                    + gate[:, 2 * NSA_W:3 * NSA_W] * _unpad_heads(win_scr[e]))


def _nsa(qpad, gexp, kcmp, vcmp, ksl, vsl, kwn, vwn, consts, batch, seq):
    qb = SEL_BLOCK
    nq = seq // qb
    rows = NSA_HEADS * qb
    npair = NSA_PAIR if batch % NSA_PAIR == 0 else 1
    nbh = batch // npair
    tok = lambda a: a.reshape(npair, nbh * seq, a.shape[-1])
    per = lambda a: a.reshape(npair, nbh, a.shape[-2], KV_W)
    tok_spec = lambda w: pl.BlockSpec((npair, qb, w), lambda bh, i: (0, bh * nq + i, 0))
    per_spec = lambda n: pl.BlockSpec((npair, 1, n, KV_W), lambda bh, i: (0, bh, 0, 0))
    full = lambda a: pl.BlockSpec(a.shape, lambda bh, i: (0,) * a.ndim)
    out = pl.pallas_call(
        _nsa_kernel,
        grid=(nbh, nq),
        in_specs=[
            tok_spec(QPAD_W), tok_spec(LANES),
            per_spec(kcmp.shape[1]), per_spec(vcmp.shape[1]),
            per_spec(seq), per_spec(seq), per_spec(seq), per_spec(seq),
        ] + [full(c) for c in consts],
        out_specs=tok_spec(NSA_W),
        out_shape=jax.ShapeDtypeStruct((npair, nbh * seq, NSA_W), F32),
        scratch_shapes=[
            pltpu.VMEM((npair, consts[-1].shape[0], NSA_KV_GROUPS * qb), F32),
            pltpu.VMEM((npair, rows, LANES), F32),
            pltpu.VMEM((npair, rows, 2 * LANES), F32),
            pltpu.VMEM((npair, rows, LANES), F32),
            pltpu.VMEM((SEL_CH, LANES), BF16),
            pltpu.VMEM((npair, rows, 2 * LANES), BF16),
            pltpu.VMEM((npair, seq // SEL_CH, rows, SEL_CH), F32),
            pltpu.VMEM((npair, rows, LANES), F32),
        ],
        compiler_params=_cparams(("parallel", "arbitrary")),
        name="nsa",
    )(tok(qpad), tok(gexp), per(kcmp), per(vcmp), per(ksl.reshape(batch, seq, KV_W)),
      per(vsl.reshape(batch, seq, KV_W)), per(kwn.reshape(batch, seq, KV_W)),
      per(vwn.reshape(batch, seq, KV_W)), *consts)
    return out.reshape(batch * seq, NSA_W)


def _nsa_constants(seq):
    qb = SEL_BLOCK
    nc_pad = seq // CMP_STRIDE
    nsb = seq // qb
    assert nsb <= qb
    nsb_pad = max(nsb, 8)
    cs = np.arange(nc_pad) * CMP_STRIDE
    ce = cs + CMP_LEN
    bs = np.arange(nsb_pad) * qb
    be = bs + qb
    ov = np.clip(np.minimum(ce[None, :], be[:, None]) - np.maximum(cs[None, :], bs[:, None]), 0, None)
    impt = ov / CMP_LEN
    pos = np.arange(seq)
    lane = np.arange(LANES)[None, :]
    ktab = np.where(lane == pos[:, None] // qb, NEG_BIG, 0.0)
    kdiag = np.where(lane == qb + np.arange(qb)[:, None], NEG_BIG, 0.0)
    last = WIN_KEYS // qb - 1
    x = np.arange(WIN_KEYS + last * qb)[:, None]
    kb, j = x // qb, x % qb
    kwtab = np.where((((kb == 0) | (kb > last)) & (lane == 0)) | ((kb == 1) & (lane == j))
                     | ((kb == last) & (lane == qb + j)), NEG_BIG, 0.0)
    r = (np.arange(NSA_HEADS * qb) % qb)[:, None]
    upper = (lane >= qb) & (lane - qb > r)
    qc = np.where(upper, 1.0, 0.0)
    qw = np.where(upper | ((lane < qb) & (lane <= r)), 1.0, 0.0)
    eye = np.eye(nsb_pad, LANES)
    col = np.arange(GEXP_W)[None, :]
    src = np.arange(LANES)[:, None]
    rep = (src == (col % NSA_W) // HEAD_DIM * 3 + col // NSA_W) & (src < 3 * NSA_HEADS)
    gexp = np.concatenate([rep, rep], axis=0)
    return (jnp.asarray(ktab, BF16), jnp.asarray(kdiag, BF16), jnp.asarray(kwtab, BF16),
            jnp.asarray(qc, F32), jnp.asarray(qw, BF16), jnp.asarray(eye, F32), jnp.asarray(gexp, BF16),
            jnp.asarray(impt, F32))


def _split3(x):
    hi = x.astype(BF16)
    r1 = x - hi.astype(F32)
    mid = r1.astype(BF16)
    lo = (r1 - mid.astype(F32)).astype(BF16)
    return hi, mid, lo


def _gla_kernel(q_ref, k_ref, v_ref, a_ref, r_ref, w6_ref, bup_ref, ng_ref, o_ref, st_ref):
    pair = range(q_ref.shape[0])
    blk = q_ref.shape[1]
    nsub = blk // GLA_SUB

    @pl.when(pl.program_id(1) == 0)
    def _():
        st_ref[...] = jnp.zeros(st_ref.shape, F32)

    ri = lax.broadcasted_iota(jnp.int32, (blk, blk), 0)
    ci = lax.broadcasted_iota(jnp.int32, (blk, blk), 1)
    ltri = jnp.where(ci <= ri, 1.0, 0.0).astype(BF16)
    ltri3 = jnp.concatenate([ltri, ltri, ltri], axis=1)
    lane_rep = lax.broadcasted_iota(jnp.int32, (1, LANES), 1) // GLA_RANK
    srow = lax.broadcasted_iota(jnp.int32, st_ref.shape[1:], 0) // GLA_DV
    scol = lax.broadcasted_iota(jnp.int32, st_ref.shape[1:], 1) // GLA_DK
    krow = lax.broadcasted_iota(jnp.int32, (blk, 1), 0)
    lane_qk = lax.broadcasted_iota(jnp.int32, (1, GLA_QK_W), 1) // GLA_DK
    lane_v = lax.broadcasted_iota(jnp.int32, (1, GLA_W), 1) // GLA_DV
    qrow = lax.broadcasted_iota(jnp.int32, (GLA_HEADS * GLA_SUB, 1), 0) % GLA_SUB
    kcol = lax.broadcasted_iota(jnp.int32, (1, blk), 1)
    gi = lax.broadcasted_iota(jnp.int32, (GLA_W, GLA_W), 0) // GLA_DV
    gj = lax.broadcasted_iota(jnp.int32, (GLA_W, GLA_W), 1) // GLA_DV
    gmean = jnp.where(gi == gj, 1.0 / GLA_DV, 0.0).astype(BF16)
    gmean2 = jnp.concatenate([gmean, gmean], axis=0)

    for e in pair:
        a_hi, a_mid, a_lo = _split3(a_ref[e])
        a6 = jnp.where(lane_rep < 3, a_hi, jnp.where(lane_rep < 5, a_mid, a_lo))
        x = _dot(a6, w6_ref[...]) + bup_ref[...]
        g = (jnp.minimum(x, 0.0) - jnp.log1p(jnp.exp(-jnp.abs(x)))) / GLA_TAU
        b = _dot(ltri3, jnp.concatenate(_split3(g), axis=0))
        blast = b[blk - 1:blk, :]
        q = q_ref[e] * (GLA_DK ** -0.5)
        k = k_ref[e]
        v = v_ref[e]
        vb = v.astype(BF16)

        st = st_ref[e]
        o = _dot_nt((q * jnp.exp(b)).astype(BF16), st.astype(BF16))
        khat = k * jnp.exp(blast - b)
        st_ref[e] = st * jnp.exp(blast) + jnp.where(srow == scol, _dot_tn(v, khat), 0.0)

        for c in range(nsub):
            lo = c * GLA_SUB
            ref_b = b[lo:lo + 1, :]
            qt = q[lo:lo + GLA_SUB] * jnp.exp(b[lo:lo + GLA_SUB] - ref_b)
            kt = k * jnp.exp(jnp.where(krow < lo + GLA_SUB, ref_b - b, 0.0))
            qs = jnp.concatenate([jnp.where(lane_qk == h, qt, 0.0) for h in range(GLA_HEADS)], axis=0)
            a = _dot_nt(qs.astype(BF16), kt.astype(BF16))
            a = jnp.where(kcol <= lo + qrow, a, 0.0)
            r = _dot(a.astype(BF16), vb)
            oi = sum(jnp.where(lane_v == h, r[h * GLA_SUB:(h + 1) * GLA_SUB], 0.0) for h in range(GLA_HEADS))
            o_ref[e, lo:lo + GLA_SUB, :] = o[lo:lo + GLA_SUB] + oi

    for e in pair:
        o = o_ref[e]
        oo = o * o
        oo_hi = oo.astype(BF16)
        oo_lo = (oo - oo_hi.astype(F32)).astype(BF16)
        ms = _dot(jnp.concatenate([oo_hi, oo_lo], axis=1), gmean2)
        rr = r_ref[e]
        o_ref[e] = o * lax.rsqrt(ms + NORM_EPS) * ng_ref[...] * (rr * jax.nn.sigmoid(rr))


def _prep_gla_wup(w_up):
    hi = w_up.astype(BF16)
    r1 = w_up - hi.astype(F32)
    mid = r1.astype(BF16)
    lo = (r1 - mid.astype(F32)).astype(BF16)
    w6 = jnp.concatenate([hi, mid, lo, hi, mid, hi], axis=0)
    return jnp.pad(w6, ((0, LANES - w6.shape[0]), (0, 0)))


def _gla(gq, gk, gv, ga, gr, w6, bup, ng, batch, seq):
    blk = min(GLA_BLK, seq)
    nb = seq // blk
    npair = GLA_PAIR if batch % GLA_PAIR == 0 else 1
    nbh = batch // npair
    tok = lambda a: a.reshape(npair, nbh * seq, a.shape[-1])
    spec = lambda w: pl.BlockSpec((npair, blk, w), lambda bh, j: (0, bh * nb + j, 0))
    full = lambda a: pl.BlockSpec(a.shape, lambda bh, j: (0,) * a.ndim)
    out = pl.pallas_call(
        _gla_kernel,
        grid=(nbh, nb),
        in_specs=[spec(GLA_QK_W), spec(GLA_QK_W), spec(GLA_W), spec(LANES), spec(GLA_W),
                  full(w6), full(bup), full(ng)],
        out_specs=spec(GLA_W),
        out_shape=jax.ShapeDtypeStruct((npair, nbh * seq, GLA_W), F32),
        scratch_shapes=[pltpu.VMEM((npair, GLA_W, GLA_QK_W), F32)],
        compiler_params=_cparams(("parallel", "arbitrary")),
        name="gla",
    )(tok(gq), tok(gk), tok(gv), tok(ga), tok(gr), w6, bup, ng)
    return out.reshape(batch * seq, GLA_W)


def _sg_kernel(uv_ref, lg_ref, lb_ref, w_ref, bias_ref, o_ref):
    n = w_ref.shape[1]
    ri = lax.broadcasted_iota(jnp.int32, (n, n), 0)
    ci = lax.broadcasted_iota(jnp.int32, (n, n), 1)
    lane_g = lax.broadcasted_iota(jnp.int32, (1, SG_W), 1) // SG_CH
    wts = [jnp.where(ci <= ri, w_ref[g], 0.0).astype(BF16) for g in range(SG_GROUPS)]
    for c in range(uv_ref.shape[0] // n):
        rs = slice(c * n, (c + 1) * n)
        a = _gelu_tanh(uv_ref[rs, :])
        u = a[:, :SG_W]
        v = a[:, SG_W:]
        mu = jnp.mean(v, axis=-1, keepdims=True)
        var = jnp.mean(jnp.square(v - mu), axis=-1, keepdims=True)
        vn = ((v - mu) * lax.rsqrt(var + NORM_EPS) * lg_ref[...] + lb_ref[...]).astype(BF16)
        s = bias_ref[...]
        for g in range(SG_GROUPS):
            s = s + jnp.where(lane_g == g, _dot(wts[g], vn), 0.0)
        o_ref[rs, :] = u * s


def _spatial_gating(uv, ln_g, ln_b, w_s, bias_exp):
    T = uv.shape[0]
    full = lambda a: pl.BlockSpec(a.shape, lambda i: (0,) * a.ndim)
    return pl.pallas_call(
        _sg_kernel,
        grid=(T // SG_BLK,),
        in_specs=[pl.BlockSpec((SG_BLK, 2 * SG_W), lambda i: (i, 0)),
                  full(ln_g), full(ln_b), full(w_s), full(bias_exp)],
        out_specs=pl.BlockSpec((SG_BLK, SG_W), lambda i: (i, 0)),
        out_shape=jax.ShapeDtypeStruct((T, SG_W), F32),
        compiler_params=_cparams(("parallel",)),
        name="spatial_gating",
    )(uv, ln_g, ln_b, w_s, bias_exp)


def _outffn_kernel(x_ref, on_ref, og_ref, os_ref, wo_ref, fg_ref, wg_ref, wu_ref, wd_ref,
                   fin_ref, o_ref, *, final):
    x = x_ref[...]
    x = x + _dot(on_ref[...].astype(BF16), wo_ref[0:NSA_W, :])
    x = x + _dot(og_ref[...].astype(BF16), wo_ref[NSA_W:NSA_W + GLA_W, :])
    x = x + _dot(os_ref[...].astype(BF16), wo_ref[NSA_W + GLA_W:, :])
    hn = (x * lax.rsqrt(jnp.mean(x * x, axis=-1, keepdims=True) + NORM_EPS) * fg_ref[...]).astype(BF16)
    gt = _dot(hn, wg_ref[...])
    up = _dot(hn, wu_ref[...])
    y = x + _dot((gt * jax.nn.sigmoid(gt) * up).astype(BF16), wd_ref[...])
    if final:
        y = y * lax.rsqrt(jnp.mean(y * y, axis=-1, keepdims=True) + NORM_EPS) * fin_ref[...]
    o_ref[...] = y


def _outffn(x2, o_nsa, o_gla, o_sg, wo, fgain, wg, wu, wd, fin, tm, final):
    T, D = x2.shape
    row = lambda w: pl.BlockSpec((tm, w), lambda i: (i, 0))
    const = lambda a: pl.BlockSpec(a.shape, lambda i: (0,) * a.ndim, pipeline_mode=pl.Buffered(1))
    return pl.pallas_call(
        functools.partial(_outffn_kernel, final=final),
        grid=(T // tm,),
        in_specs=[row(D), row(NSA_W), row(GLA_W), row(SG_W), const(wo), const(fgain),
                  const(wg), const(wu), const(wd), const(fin)],
        out_specs=row(D),
        out_shape=jax.ShapeDtypeStruct((T, D), F32),
        compiler_params=_cparams(("parallel",)),
        name="outproj_ffn",
    )(x2, o_nsa, o_gla, o_sg, wo, fgain, wg, wu, wd, fin)


def kernel(x, attn_norm, w_in, cmp_pos_k, cmp_w1_k, cmp_w2_k, cmp_pos_v, cmp_w1_v, cmp_w2_v,
           gla_w_up, gla_b_up, gla_norm, sg_ln_g, sg_ln_b, sg_w, sg_b, w_out,
           ffn_norm, w_gate, w_up, w_down, final_norm):
    batch, seq, d_model = x.shape
    depth = w_in.shape[0]
    T = batch * seq
    assert seq % GLA_BLK == 0 and seq >= WIN_KEYS and seq % SEL_CH == 0
    tm_in = 512
    tm_ffn = 512

    cos_t, sin_t = _rope_tables(jnp.arange(seq))
    nrow = seq // CMP_STRIDE
    cos_c, sin_c = _rope_tables(jnp.arange(nrow) * CMP_STRIDE + CMP_LEN - 1)
    nsa_consts = _nsa_constants(seq)

    x2 = x.reshape(T, d_model)
    for l in range(depth):
        (qpad, kc, vc, ksl, vsl, kwn, vwn, gexp, gq, gk, gv, ga, gr, uv) = _inproj(
            x2, attn_norm[l][None, :], _prep_w_in(w_in[l]), cos_t, sin_t, seq, tm_in)
        kcmp, vcmp = _compress(
            kc, vc, _prep_cmp_weights(cmp_w1_k[l], cmp_w2_k[l], cmp_pos_k[l]),
            _prep_cmp_weights(cmp_w1_v[l], cmp_w2_v[l], cmp_pos_v[l]), cos_c, sin_c, batch, seq)
        o_nsa = _nsa(qpad, gexp, kcmp, vcmp, ksl, vsl, kwn, vwn, nsa_consts, batch, seq)
        o_gla = _gla(gq, gk, gv, ga, gr, _prep_gla_wup(gla_w_up[l]), gla_b_up[l][None, :],
                     jnp.tile(gla_norm[l], GLA_HEADS)[None, :], batch, seq)
        bias_exp = jnp.repeat(sg_b[l].T, SG_CH, axis=1)
        o_sg = _spatial_gating(uv, sg_ln_g[l][None, :], sg_ln_b[l][None, :], sg_w[l], bias_exp)
        x2 = _outffn(x2, o_nsa, o_gla, o_sg, w_out[l].astype(BF16), ffn_norm[l][None, :],
                     w_gate[l].astype(BF16), w_up[l].astype(BF16), w_down[l].astype(BF16),
                     final_norm[None, :], tm_ffn, final=(l == depth - 1))
    return x2.reshape(batch, seq, d_model)
```

```python
import functools
import math

import numpy as np
import jax
import jax.numpy as jnp
from jax import lax
from jax.experimental import pallas as pl
from jax.experimental.pallas import tpu as pltpu

HEAD_DIM = 64
NSA_HEADS = 8
NSA_KV_GROUPS = 2
NSA_HPG = NSA_HEADS // NSA_KV_GROUPS
CMP_LEN = 32
CMP_STRIDE = 16
CMP_HIDDEN = 256
SEL_BLOCK = 64
SEL_TOPK = 16
WINDOW = 512
GLA_HEADS = 4
GLA_DK = 32
GLA_DV = 64
GLA_RANK = 16
GLA_TAU = 16.0
SG_GROUPS = 4
SG_CH = 64
SG_CHUNK = 128
NSA_W = NSA_HEADS * HEAD_DIM
GLA_W = GLA_HEADS * GLA_DV
SG_W = SG_GROUPS * SG_CH
KV_W = NSA_KV_GROUPS * HEAD_DIM
IN_SIZES = (NSA_W, KV_W, KV_W, KV_W, KV_W, KV_W, KV_W, NSA_HEADS * 3,
            GLA_HEADS * GLA_DK, GLA_HEADS * GLA_DK, GLA_W, GLA_RANK, GLA_W, 2 * SG_W)
ROPE_THETA = 10000.0
NORM_EPS = 1e-6

LANES = 128
NEG_BIG = -1e30
VMEM_LIMIT = 56 * 1024 * 1024

QPAD_W = NSA_HEADS * LANES
GEXP_W = 3 * NSA_W
GLA_QK_W = GLA_HEADS * GLA_DK
GLA_BLK = 256
GLA_SUB = 16
SEL_CH = 1024
WIN_KEYS = WINDOW + 2 * SEL_BLOCK
NSA_ROW_PARTS = 2
RANK_UNROLL = 4
SG_BLK = 4 * SG_CHUNK
NSA_PAIR = 2
GLA_PAIR = 4

HI = lax.Precision.HIGHEST
F32 = jnp.float32
BF16 = jnp.bfloat16


def _cparams(sem):
    return pltpu.CompilerParams(dimension_semantics=sem, vmem_limit_bytes=VMEM_LIMIT)


def _dot(a, b, precision=None):
    return jnp.dot(a, b, preferred_element_type=F32, precision=precision)


def _dot_nt(a, b, precision=None):
    return lax.dot_general(a, b, (((1,), (1,)), ((), ())),
                           preferred_element_type=F32, precision=precision)


def _dot_tn(a, b, precision=None):
    return lax.dot_general(a, b, (((0,), (0,)), ((), ())),
                           preferred_element_type=F32, precision=precision)


def _gelu_tanh(x):
    c = math.sqrt(2.0 / math.pi)
    return 0.5 * x * (1.0 + jnp.tanh(c * (x + 0.044715 * (x * x * x))))


def _rope_lanes(x, cos, sin_signed):
    n = x.shape[-1]
    lane = lax.broadcasted_iota(jnp.int32, x.shape, 1)
    first_half = (lane % HEAD_DIM) < (HEAD_DIM // 2)
    partner = jnp.where(first_half,
                        pltpu.roll(x, n - HEAD_DIM // 2, 1),
                        pltpu.roll(x, HEAD_DIM // 2, 1))
    return x * cos + partner * sin_signed


_INPROJ_OUT = (
    ("qpad", QPAD_W, BF16), ("kc", KV_W, F32), ("vc", KV_W, F32),
    ("ksl", KV_W, BF16), ("vsl", KV_W, BF16), ("kwn", KV_W, BF16), ("vwn", KV_W, BF16),
    ("gts", LANES, F32), ("gq", GLA_QK_W, F32), ("gk", GLA_QK_W, F32),
    ("gv", GLA_W, F32), ("ga", LANES, F32), ("gr", GLA_W, F32), ("uv", 2 * SG_W, F32))
_INPROJ_W = sum(w for _, w, _ in _INPROJ_OUT)


def _inproj_kernel(x_ref, g_ref, w_ref, cos_ref, sin_ref, *out_refs):
    x = x_ref[...]
    hn = x * lax.rsqrt(jnp.mean(x * x, axis=-1, keepdims=True) + NORM_EPS) * g_ref[...]
    z = _dot(hn.astype(BF16), w_ref[...])
    cos = cos_ref[...]
    sin = sin_ref[...]
    off = 0
    for (name, width, dtype), o_ref in zip(_INPROJ_OUT, out_refs):
        if name in ("qpad", "ksl", "kwn"):
            for j in range(width // LANES):
                blk = _rope_lanes(z[:, off + j * LANES: off + (j + 1) * LANES], cos, sin)
                if name == "qpad":
                    blk = blk * (HEAD_DIM ** -0.5)
                o_ref[:, j * LANES:(j + 1) * LANES] = blk.astype(dtype)
        else:
            o_ref[...] = z[:, off:off + width].astype(dtype)
        off += width


def _inproj(x2, gain, w_p, cos_t, sin_t, seq, tm):
    T, D = x2.shape
    nper = seq // tm
    out_shape = [jax.ShapeDtypeStruct((T, w), dt) for _, w, dt in _INPROJ_OUT]
    out_specs = [pl.BlockSpec((tm, w), lambda i: (i, 0)) for _, w, _ in _INPROJ_OUT]
    return pl.pallas_call(
        _inproj_kernel,
        grid=(T // tm,),
        in_specs=[
            pl.BlockSpec((tm, D), lambda i: (i, 0)),
            pl.BlockSpec((1, D), lambda i: (0, 0)),
            pl.BlockSpec((D, _INPROJ_W), lambda i: (0, 0), pipeline_mode=pl.Buffered(1)),
            pl.BlockSpec((tm, LANES), lambda i: (i % nper, 0)),
            pl.BlockSpec((tm, LANES), lambda i: (i % nper, 0)),
        ],
        out_specs=out_specs,
        out_shape=out_shape,
        compiler_params=_cparams(("parallel",)),
        name="inproj",
    )(x2, gain, w_p, cos_t, sin_t)


def _prep_w_in(w_in):
    offs = np.cumsum((0,) + IN_SIZES)
    seg = [w_in[:, offs[k]:offs[k + 1]] for k in range(len(IN_SIZES))]
    (wq, wkc, wvc, wksl, wvsl, wkwn, wvwn, wg, wgq, wgk, wgv, wga, wgr, wuv) = seg
    D = w_in.shape[0]
    zeros64 = jnp.zeros((D, HEAD_DIM), w_in.dtype)
    qcols = []
    for h in range(NSA_HEADS):
        wh = wq[:, h * HEAD_DIM:(h + 1) * HEAD_DIM]
        qcols += [wh, zeros64] if h // NSA_HPG == 0 else [zeros64, wh]
    gcols = [jnp.pad(wg, ((0, 0), (0, LANES - wg.shape[1])))]
    wga_p = jnp.pad(jnp.concatenate([wga] * 6, axis=1), ((0, 0), (0, LANES - 6 * GLA_RANK)))
    return jnp.concatenate(qcols + [wkc, wvc, wksl, wvsl, wkwn, wvwn] + gcols
                           + [wgq, wgk, wgv, wga_p, wgr, wuv], axis=1).astype(BF16)


def _rope_tables(pos):
    half = HEAD_DIM // 2
    inv = 1.0 / (ROPE_THETA ** (jnp.arange(half, dtype=F32) / half))
    ang = pos.astype(F32)[:, None] * inv[None, :]
    cos = jnp.cos(ang)
    sin = jnp.sin(ang)
    cos_t = jnp.concatenate([cos, cos, cos, cos], axis=1)
    sin_t = jnp.concatenate([-sin, sin, -sin, sin], axis=1)
    return cos_t, sin_t


def _cmp_kernel(xk_ref, xv_ref, wek_ref, wev_ref, w1k_ref, w1v_ref, pk_ref, pv_ref,
                w2k_ref, w2v_ref, cos_ref, sin_ref, ok_ref, ov_ref):
    nrow = xk_ref.shape[1]
    row = lax.broadcasted_iota(jnp.int32, (nrow, LANES), 0)

    def compress(x_ref, we_ref, w1_ref, p_ref, w2_ref):
        h = _dot(x_ref[0].astype(BF16), we_ref[...])
        posb = _dot(p_ref[...].astype(BF16), w1_ref[...])[0:1, :]
        y = jnp.zeros((nrow, LANES), F32)
        for g in range(NSA_KV_GROUPS):
            a = h[:, g * CMP_HIDDEN:(g + 1) * CMP_HIDDEN]
            b = h[:, (NSA_KV_GROUPS + g) * CMP_HIDDEN:(NSA_KV_GROUPS + g + 1) * CMP_HIDDEN]
            hid = a + pltpu.roll(b, nrow - 1, 0) + posb
            y = y + _dot(_gelu_tanh(hid).astype(BF16), w2_ref[g])
        return y

    yk = _rope_lanes(compress(xk_ref, wek_ref, w1k_ref, pk_ref, w2k_ref), cos_ref[...], sin_ref[...])
    yv = compress(xv_ref, wev_ref, w1v_ref, pv_ref, w2v_ref)
    keep = row < nrow - 1
    ok_ref[0] = jnp.where(keep, yk, 0.0)
    ov_ref[0] = jnp.where(keep, yv, 0.0)


def _prep_cmp_weights(w1, w2, pos):
    half = CMP_LEN // CMP_STRIDE
    H = w1.shape[1]
    w1r = w1.reshape(half, CMP_STRIDE, HEAD_DIM, H)
    cols = []
    for a in range(half):
        for g in range(NSA_KV_GROUPS):
            blk = jnp.zeros((CMP_STRIDE, NSA_KV_GROUPS, HEAD_DIM, H), w1.dtype)
            blk = blk.at[:, g].set(w1r[a])
            cols.append(blk.reshape(CMP_STRIDE * KV_W, H))
    wexp = jnp.concatenate(cols, axis=1).astype(BF16)
    w2p = jnp.stack([jnp.pad(w2, ((0, 0), (g * HEAD_DIM, KV_W - (g + 1) * HEAD_DIM)))
                     for g in range(NSA_KV_GROUPS)]).astype(BF16)
    posf = jnp.pad(pos.reshape(1, CMP_LEN * HEAD_DIM), ((0, 7), (0, 0)))
    return wexp, w1.astype(BF16), posf, w2p


def _compress(kc, vc, wk, wv, cos_c, sin_c, batch, seq):
    nrow = seq // CMP_STRIDE
    xk = kc.reshape(batch, nrow, CMP_STRIDE * KV_W)
    xv = vc.reshape(batch, nrow, CMP_STRIDE * KV_W)
    wek, w1k, pk, w2k = wk
    wev, w1v, pv, w2v = wv
    full = lambda a: pl.BlockSpec(a.shape, lambda b: (0,) * a.ndim)
    xspec = pl.BlockSpec((1, nrow, CMP_STRIDE * KV_W), lambda b: (b, 0, 0))
    ospec = pl.BlockSpec((1, nrow, KV_W), lambda b: (b, 0, 0))
    return pl.pallas_call(
        _cmp_kernel,
        grid=(batch,),
        in_specs=[xspec, xspec, full(wek), full(wev), full(w1k), full(w1v), full(pk), full(pv),
                  full(w2k), full(w2v), full(cos_c), full(sin_c)],
        out_specs=[ospec, ospec],
        out_shape=[jax.ShapeDtypeStruct((batch, nrow, KV_W), F32)] * 2,
        compiler_params=_cparams(("parallel",)),
        name="compress",
    )(xk, xv, wek, wev, w1k, w1v, pk, pv, w2k, w2v, cos_c, sin_c)


def _unpad_heads(acc):
    qb = SEL_BLOCK
    lane = lax.broadcasted_iota(jnp.int32, (qb, LANES), 1)
    low = lane < HEAD_DIM
    outs = []
    for j in range(NSA_HEADS // 2):
        a = acc[(2 * j) * qb:(2 * j + 1) * qb]
        b = acc[(2 * j + 1) * qb:(2 * j + 2) * qb]
        if (2 * j) // NSA_HPG == 0:
            outs.append(jnp.where(low, a, pltpu.roll(b, HEAD_DIM, 1)))
        else:
            outs.append(jnp.where(low, pltpu.roll(a, HEAD_DIM, 1), b))
    return jnp.concatenate(outs, axis=1)


def _nsa_kernel(q_ref, g_ref, kc_ref, vc_ref, ksl_ref, vsl_ref, kwn_ref, vwn_ref,
                ktab_ref, kdiag_ref, kwtab_ref, qc_ref, qw_ref, eye_ref, gexp_ref, impt_ref, o_ref,
                imp_scr, m_scr, acc_scr, win_scr, kd_scr, qs_scr, sc_scr, cmp_scr):
    qb = SEL_BLOCK
    rows = NSA_HEADS * qb
    pair = range(q_ref.shape[0])
    i = pl.program_id(1)
    t0 = i * qb
    t = t0 + lax.broadcasted_iota(jnp.int32, (rows, 1), 0) % qb
    ones = jnp.ones((SEL_CH, LANES), BF16)
    parts = [slice(p * rows // NSA_ROW_PARTS, (p + 1) * rows // NSA_ROW_PARTS) for p in range(NSA_ROW_PARTS)]
    qbf = [jnp.concatenate([q_ref[e, :, h * LANES:(h + 1) * LANES] for h in range(NSA_HEADS)], axis=0)
           for e in pair]

    nblk_w = WIN_KEYS // qb
    ib = jnp.minimum(i, nblk_w - 1)
    w0 = pl.multiple_of((i - ib) * qb, qb)
    toff = pl.multiple_of((nblk_w - 1 - ib) * qb, qb)
    kwmask = kwtab_ref[pl.ds(toff, WIN_KEYS), :]
    for e in pair:
        kw = jnp.concatenate([kwn_ref[e, 0, pl.ds(w0, WIN_KEYS), :], kwmask], axis=1)
        vw = jnp.concatenate([vwn_ref[e, 0, pl.ds(w0, WIN_KEYS), :], ones[0:WIN_KEYS]], axis=1)
        q_win = jnp.concatenate([qbf[e], qw_ref[...]], axis=1)
        for rs in parts:
            sw = _dot_nt(q_win[rs], kw)
            ew = jnp.exp(sw - jnp.max(sw, axis=1, keepdims=True)).astype(BF16)
            ow = _dot(ew, vw)
            win_scr[e, rs, :] = ow[:, 0:LANES] / ow[:, LANES:2 * LANES]

    ncmp = kc_ref.shape[2]
    cend = lax.broadcasted_iota(jnp.int32, (1, ncmp), 1) * CMP_STRIDE + (CMP_LEN - 1)
    vis = cend <= t
    imap = impt_ref[...].astype(BF16)
    imap2 = jnp.concatenate([imap, imap], axis=1)
    for e in pair:
        s = _dot_nt(qbf[e], kc_ref[e, 0].astype(BF16))
        sm = jnp.where(vis, s, NEG_BIG)
        mx = jnp.max(sm, axis=1, keepdims=True)
        ex = jnp.where(vis, jnp.exp(sm - mx), 0.0)
        d = jnp.sum(ex, axis=1, keepdims=True)
        p = ex / jnp.where(d > 0, d, 1.0)
        cmp_scr[e] = _dot(p.astype(BF16), vc_ref[e, 0].astype(BF16))
        psum = jnp.concatenate(
            [sum(p[(g * NSA_HPG + h) * qb:(g * NSA_HPG + h + 1) * qb] for h in range(NSA_HPG))
             for g in range(NSA_KV_GROUPS)], axis=0)
        p_hi = psum.astype(BF16)
        p_lo = (psum - p_hi.astype(F32)).astype(BF16)
        imp_scr[e] = _dot_nt(imap2, jnp.concatenate([p_hi, p_lo], axis=1))

    nsbp = imp_scr.shape[1]
    imps = [imp_scr[e] for e in pair]
    n_id = lax.broadcasted_iota(jnp.int32, (nsbp, 1), 0)
    valid = n_id <= i
    forced = (n_id == 0) | (n_id == i) | (n_id == i - 1)
    nforced = 1 + (i >= 1).astype(jnp.int32) + (i >= 2).astype(jnp.int32)

    def rank_body(j, cnts):
        cnts = list(cnts)
        for u in range(RANK_UNROLL):
            m = 1 + RANK_UNROLL * j + u
            tie = jnp.where(m < n_id, 1.0, 0.0)
            for e in pair:
                rowm = imp_scr[e, pl.ds(jnp.minimum(m, nsbp - 1), 1), :]
                rowm = jnp.where(m <= i - 2, rowm, NEG_BIG)
                cnts[e] = cnts[e] + jnp.where(rowm > imps[e], 1.0, 0.0) + jnp.where(rowm == imps[e], tie, 0.0)
        return tuple(cnts)

    ntrip = (jnp.maximum(i - 2, 0) + RANK_UNROLL - 1) // RANK_UNROLL
    cnts = lax.fori_loop(0, ntrip, rank_body, tuple(jnp.zeros(imps[0].shape, F32) for _ in pair))
    budget = (SEL_TOPK - nforced).astype(F32)

    for e in pair:
        sel = valid & (forced | (cnts[e] < budget))
        notsel_t = jnp.where(sel, 0.0, 1.0)
        notsel = _dot_tn(notsel_t, eye_ref[...])
        notsel = jnp.concatenate([notsel[g * qb:(g + 1) * qb]
                                  for g in range(NSA_KV_GROUPS) for _ in range(NSA_HPG)], axis=0)
        qs_scr[e] = jnp.concatenate([qbf[e], (notsel + qc_ref[...]).astype(BF16)], axis=1)
    m_scr[...] = jnp.full(m_scr.shape, NEG_BIG, F32)
    acc_scr[...] = jnp.zeros(acc_scr.shape, F32)
    half = SEL_CH // 2
    nhalf = i // (half // qb) + 1
    lead = nhalf % 2
    nfull = nhalf // 2
    base = lead * half
    k0_last = pl.multiple_of(jnp.maximum(base + (nfull - 1) * SEL_CH, 0), half)
    off = pl.multiple_of(i * qb - k0_last, qb)
    kd_scr[...] = ktab_ref[pl.ds(k0_last, SEL_CH), :]
    kd_scr[pl.ds(off, qb), :] = kd_scr[pl.ds(off, qb), :] + kdiag_ref[...]

    def scores(slot, k0, size, kmask):
        for e in pair:
            k = jnp.concatenate([ksl_ref[e, 0, pl.ds(k0, size), :], kmask], axis=1)
            for rs in parts:
                sc = _dot_nt(qs_scr[e, rs, :], k)
                sc_scr[e, slot, rs, 0:size] = sc
                mx = m_scr[e, rs, :]
                for j in range(size // LANES):
                    mx = jnp.maximum(mx, sc[:, j * LANES:(j + 1) * LANES])
                m_scr[e, rs, :] = mx

    def values(slot, k0, size):
        for e in pair:
            v = jnp.concatenate([vsl_ref[e, 0, pl.ds(k0, size), :], ones[0:size]], axis=1)
            for rs in parts:
                mb = m_scr[e, rs, :]
                pr = jnp.concatenate([jnp.exp(sc_scr[e, slot, rs, j * LANES:(j + 1) * LANES] - mb)
                                      for j in range(size // LANES)], axis=1).astype(BF16)
                acc_scr[e, rs, :] += _dot(pr, v)

    @pl.when(lead == 1)
    def _():
        scores(0, 0, half, jnp.where(nfull == 0, kd_scr[0:half, :], ktab_ref[0:half, :]))

    def score_body(c, carry):
        k0 = pl.multiple_of(base + c * SEL_CH, half)
        scores(1 + c, k0, SEL_CH, jnp.where(c == nfull - 1, kd_scr[...], ktab_ref[pl.ds(k0, SEL_CH), :]))
        return carry

    lax.fori_loop(0, nfull, score_body, 0)
    for e in pair:
        m_scr[e] = jnp.broadcast_to(jnp.max(m_scr[e], axis=1, keepdims=True), (rows, LANES))

    @pl.when(lead == 1)
    def _():
        values(0, 0, half)

    def value_body(c, carry):
        values(1 + c, pl.multiple_of(base + c * SEL_CH, half), SEL_CH)
        return carry

    lax.fori_loop(0, nfull, value_body, 0)

    for e in pair:
        o_slc = acc_scr[e, :, 0:LANES] / acc_scr[e, :, LANES:2 * LANES]
        g_hi = jax.nn.sigmoid(g_ref[e])
        g_lo = g_hi - g_hi.astype(BF16).astype(F32)
        gate = _dot(jnp.concatenate([g_hi.astype(BF16), g_lo.astype(BF16)], axis=1), gexp_ref[...])
        o_ref[e] = (gate[:, 0:NSA_W] * _unpad_heads(cmp_scr[e])
                    + gate[:, NSA_W:2 * NSA_W] * _unpad_heads(o_slc)
                    + gate[:, 2 * NSA_W:3 * NSA_W] * _unpad_heads(win_scr[e]))


def _nsa(qpad, gexp, kcmp, vcmp, ksl, vsl, kwn, vwn, consts, batch, seq):
    qb = SEL_BLOCK
    nq = seq // qb
    rows = NSA_HEADS * qb
    npair = NSA_PAIR if batch % NSA_PAIR == 0 else 1
    nbh = batch // npair
    tok = lambda a: a.reshape(npair, nbh * seq, a.shape[-1])
    per = lambda a: a.reshape(npair, nbh, a.shape[-2], KV_W)
    tok_spec = lambda w: pl.BlockSpec((npair, qb, w), lambda bh, i: (0, bh * nq + i, 0))
    per_spec = lambda n: pl.BlockSpec((npair, 1, n, KV_W), lambda bh, i: (0, bh, 0, 0))
    full = lambda a: pl.BlockSpec(a.shape, lambda bh, i: (0,) * a.ndim)
    out = pl.pallas_call(
        _nsa_kernel,
        grid=(nbh, nq),
        in_specs=[
            tok_spec(QPAD_W), tok_spec(LANES),
            per_spec(kcmp.shape[1]), per_spec(vcmp.shape[1]),
            per_spec(seq), per_spec(seq), per_spec(seq), per_spec(seq),
        ] + [full(c) for c in consts],
        out_specs=tok_spec(NSA_W),
        out_shape=jax.ShapeDtypeStruct((npair, nbh * seq, NSA_W), F32),
        scratch_shapes=[
            pltpu.VMEM((npair, consts[-1].shape[0], NSA_KV_GROUPS * qb), F32),
            pltpu.VMEM((npair, rows, LANES), F32),
            pltpu.VMEM((npair, rows, 2 * LANES), F32),
            pltpu.VMEM((npair, rows, LANES), F32),
            pltpu.VMEM((SEL_CH, LANES), BF16),
            pltpu.VMEM((npair, rows, 2 * LANES), BF16),
            pltpu.VMEM((npair, seq // SEL_CH + 1, rows, SEL_CH), F32),
            pltpu.VMEM((npair, rows, LANES), F32),
        ],
        compiler_params=_cparams(("parallel", "arbitrary")),
        name="nsa",
    )(tok(qpad), tok(gexp), per(kcmp), per(vcmp), per(ksl.reshape(batch, seq, KV_W)),
      per(vsl.reshape(batch, seq, KV_W)), per(kwn.reshape(batch, seq, KV_W)),
      per(vwn.reshape(batch, seq, KV_W)), *consts)
    return out.reshape(batch * seq, NSA_W)


def _nsa_constants(seq):
    qb = SEL_BLOCK
    nc_pad = seq // CMP_STRIDE
    nsb = seq // qb
    assert nsb <= qb
    nsb_pad = max(nsb, 8)
    cs = np.arange(nc_pad) * CMP_STRIDE
    ce = cs + CMP_LEN
    bs = np.arange(nsb_pad) * qb
    be = bs + qb
    ov = np.clip(np.minimum(ce[None, :], be[:, None]) - np.maximum(cs[None, :], bs[:, None]), 0, None)
    impt = ov / CMP_LEN
    pos = np.arange(seq)
    lane = np.arange(LANES)[None, :]
    ktab = np.where(lane == pos[:, None] // qb, NEG_BIG, 0.0)
    kdiag = np.where(lane == qb + np.arange(qb)[:, None], NEG_BIG, 0.0)
    last = WIN_KEYS // qb - 1
    x = np.arange(WIN_KEYS + last * qb)[:, None]
    kb, j = x // qb, x % qb
    kwtab = np.where((((kb == 0) | (kb > last)) & (lane == 0)) | ((kb == 1) & (lane == j))
                     | ((kb == last) & (lane == qb + j)), NEG_BIG, 0.0)
    r = (np.arange(NSA_HEADS * qb) % qb)[:, None]
    upper = (lane >= qb) & (lane - qb > r)
    qc = np.where(upper, 1.0, 0.0)
    qw = np.where(upper | ((lane < qb) & (lane <= r)), 1.0, 0.0)
    eye = np.eye(nsb_pad, LANES)
    col = np.arange(GEXP_W)[None, :]
    src = np.arange(LANES)[:, None]
    rep = (src == (col % NSA_W) // HEAD_DIM * 3 + col // NSA_W) & (src < 3 * NSA_HEADS)
    gexp = np.concatenate([rep, rep], axis=0)
    return (jnp.asarray(ktab, BF16), jnp.asarray(kdiag, BF16), jnp.asarray(kwtab, BF16),
            jnp.asarray(qc, F32), jnp.asarray(qw, BF16), jnp.asarray(eye, F32), jnp.asarray(gexp, BF16),
            jnp.asarray(impt, F32))


def _split3(x):
    hi = x.astype(BF16)
    r1 = x - hi.astype(F32)
    mid = r1.astype(BF16)
    lo = (r1 - mid.astype(F32)).astype(BF16)
    return hi, mid, lo


def _gla_kernel(q_ref, k_ref, v_ref, a_ref, r_ref, w6_ref, bup_ref, ng_ref, o_ref, st_ref):
    pair = range(q_ref.shape[0])
    blk = q_ref.shape[1]
    nsub = blk // GLA_SUB

    @pl.when(pl.program_id(1) == 0)
    def _():
        st_ref[...] = jnp.zeros(st_ref.shape, F32)

    ri = lax.broadcasted_iota(jnp.int32, (blk, blk), 0)
    ci = lax.broadcasted_iota(jnp.int32, (blk, blk), 1)
    ltri = jnp.where(ci <= ri, 1.0, 0.0).astype(BF16)
    ltri3 = jnp.concatenate([ltri, ltri, ltri], axis=1)
    lane_rep = lax.broadcasted_iota(jnp.int32, (1, LANES), 1) // GLA_RANK
    srow = lax.broadcasted_iota(jnp.int32, st_ref.shape[1:], 0) // GLA_DV
    scol = lax.broadcasted_iota(jnp.int32, st_ref.shape[1:], 1) // GLA_DK
    krow = lax.broadcasted_iota(jnp.int32, (blk, 1), 0)
    lane_qk = lax.broadcasted_iota(jnp.int32, (1, GLA_QK_W), 1) // GLA_DK
    lane_v = lax.broadcasted_iota(jnp.int32, (1, GLA_W), 1) // GLA_DV
    qrow = lax.broadcasted_iota(jnp.int32, (GLA_HEADS * GLA_SUB, 1), 0) % GLA_SUB
    kcol = lax.broadcasted_iota(jnp.int32, (1, blk), 1)
    gi = lax.broadcasted_iota(jnp.int32, (GLA_W, GLA_W), 0) // GLA_DV
    gj = lax.broadcasted_iota(jnp.int32, (GLA_W, GLA_W), 1) // GLA_DV
    gmean = jnp.where(gi == gj, 1.0 / GLA_DV, 0.0).astype(BF16)
    gmean2 = jnp.concatenate([gmean, gmean], axis=0)

    for e in pair:
        a_hi, a_mid, a_lo = _split3(a_ref[e])
        a6 = jnp.where(lane_rep < 3, a_hi, jnp.where(lane_rep < 5, a_mid, a_lo))
        x = _dot(a6, w6_ref[...]) + bup_ref[...]
        g = (jnp.minimum(x, 0.0) - jnp.log1p(jnp.exp(-jnp.abs(x)))) / GLA_TAU
        b = _dot(ltri3, jnp.concatenate(_split3(g), axis=0))
        blast = b[blk - 1:blk, :]
        q = q_ref[e] * (GLA_DK ** -0.5)
        k = k_ref[e]
        v = v_ref[e]
        vb = v.astype(BF16)

        st = st_ref[e]
        o = _dot_nt((q * jnp.exp(b)).astype(BF16), st.astype(BF16))
        khat = k * jnp.exp(blast - b)
        st_ref[e] = st * jnp.exp(blast) + jnp.where(srow == scol, _dot_tn(v, khat), 0.0)

        for c in range(nsub):
            lo = c * GLA_SUB
            ref_b = b[lo:lo + 1, :]
            qt = q[lo:lo + GLA_SUB] * jnp.exp(b[lo:lo + GLA_SUB] - ref_b)
            kt = k * jnp.exp(jnp.where(krow < lo + GLA_SUB, ref_b - b, 0.0))
            qs = jnp.concatenate([jnp.where(lane_qk == h, qt, 0.0) for h in range(GLA_HEADS)], axis=0)
            a = _dot_nt(qs.astype(BF16), kt.astype(BF16))
            a = jnp.where(kcol <= lo + qrow, a, 0.0)
            r = _dot(a.astype(BF16), vb)
            oi = sum(jnp.where(lane_v == h, r[h * GLA_SUB:(h + 1) * GLA_SUB], 0.0) for h in range(GLA_HEADS))
            o_ref[e, lo:lo + GLA_SUB, :] = o[lo:lo + GLA_SUB] + oi

    for e in pair:
        o = o_ref[e]
        oo = o * o
        oo_hi = oo.astype(BF16)
        oo_lo = (oo - oo_hi.astype(F32)).astype(BF16)
        ms = _dot(jnp.concatenate([oo_hi, oo_lo], axis=1), gmean2)
        rr = r_ref[e]
        o_ref[e] = o * lax.rsqrt(ms + NORM_EPS) * ng_ref[...] * (rr * jax.nn.sigmoid(rr))


def _prep_gla_wup(w_up):
    hi = w_up.astype(BF16)
    r1 = w_up - hi.astype(F32)
    mid = r1.astype(BF16)
    lo = (r1 - mid.astype(F32)).astype(BF16)
    w6 = jnp.concatenate([hi, mid, lo, hi, mid, hi], axis=0)
    return jnp.pad(w6, ((0, LANES - w6.shape[0]), (0, 0)))


def _gla(gq, gk, gv, ga, gr, w6, bup, ng, batch, seq):
    blk = min(GLA_BLK, seq)
    nb = seq // blk
    npair = GLA_PAIR if batch % GLA_PAIR == 0 else 1
    nbh = batch // npair
    tok = lambda a: a.reshape(npair, nbh * seq, a.shape[-1])
    spec = lambda w: pl.BlockSpec((npair, blk, w), lambda bh, j: (0, bh * nb + j, 0))
    full = lambda a: pl.BlockSpec(a.shape, lambda bh, j: (0,) * a.ndim)
    out = pl.pallas_call(
        _gla_kernel,
        grid=(nbh, nb),
        in_specs=[spec(GLA_QK_W), spec(GLA_QK_W), spec(GLA_W), spec(LANES), spec(GLA_W),
                  full(w6), full(bup), full(ng)],
        out_specs=spec(GLA_W),
        out_shape=jax.ShapeDtypeStruct((npair, nbh * seq, GLA_W), F32),
        scratch_shapes=[pltpu.VMEM((npair, GLA_W, GLA_QK_W), F32)],
        compiler_params=_cparams(("parallel", "arbitrary")),
        name="gla",
    )(tok(gq), tok(gk), tok(gv), tok(ga), tok(gr), w6, bup, ng)
    return out.reshape(batch * seq, GLA_W)


def _sg_kernel(uv_ref, lg_ref, lb_ref, w_ref, bias_ref, o_ref):
    n = w_ref.shape[1]
    ri = lax.broadcasted_iota(jnp.int32, (n, n), 0)
    ci = lax.broadcasted_iota(jnp.int32, (n, n), 1)
    lane_g = lax.broadcasted_iota(jnp.int32, (1, SG_W), 1) // SG_CH
    wts = [jnp.where(ci <= ri, w_ref[g], 0.0).astype(BF16) for g in range(SG_GROUPS)]
    for c in range(uv_ref.shape[0] // n):
        rs = slice(c * n, (c + 1) * n)
        a = _gelu_tanh(uv_ref[rs, :])
        u = a[:, :SG_W]
        v = a[:, SG_W:]
        mu = jnp.mean(v, axis=-1, keepdims=True)
        var = jnp.mean(jnp.square(v - mu), axis=-1, keepdims=True)
        vn = ((v - mu) * lax.rsqrt(var + NORM_EPS) * lg_ref[...] + lb_ref[...]).astype(BF16)
        s = bias_ref[...]
        for g in range(SG_GROUPS):
            s = s + jnp.where(lane_g == g, _dot(wts[g], vn), 0.0)
        o_ref[rs, :] = u * s


def _spatial_gating(uv, ln_g, ln_b, w_s, bias_exp):
    T = uv.shape[0]
    full = lambda a: pl.BlockSpec(a.shape, lambda i: (0,) * a.ndim)
    return pl.pallas_call(
        _sg_kernel,
        grid=(T // SG_BLK,),
        in_specs=[pl.BlockSpec((SG_BLK, 2 * SG_W), lambda i: (i, 0)),
                  full(ln_g), full(ln_b), full(w_s), full(bias_exp)],
        out_specs=pl.BlockSpec((SG_BLK, SG_W), lambda i: (i, 0)),
        out_shape=jax.ShapeDtypeStruct((T, SG_W), F32),
        compiler_params=_cparams(("parallel",)),
        name="spatial_gating",
    )(uv, ln_g, ln_b, w_s, bias_exp)


def _outffn_kernel(x_ref, on_ref, og_ref, os_ref, wo_ref, fg_ref, wg_ref, wu_ref, wd_ref,
                   fin_ref, o_ref, *, final):
    x = x_ref[...]
    x = x + _dot(on_ref[...].astype(BF16), wo_ref[0:NSA_W, :])
    x = x + _dot(og_ref[...].astype(BF16), wo_ref[NSA_W:NSA_W + GLA_W, :])
    x = x + _dot(os_ref[...].astype(BF16), wo_ref[NSA_W + GLA_W:, :])
    hn = (x * lax.rsqrt(jnp.mean(x * x, axis=-1, keepdims=True) + NORM_EPS) * fg_ref[...]).astype(BF16)
    gt = _dot(hn, wg_ref[...])
    up = _dot(hn, wu_ref[...])
    y = x + _dot((gt * jax.nn.sigmoid(gt) * up).astype(BF16), wd_ref[...])
    if final:
        y = y * lax.rsqrt(jnp.mean(y * y, axis=-1, keepdims=True) + NORM_EPS) * fin_ref[...]
    o_ref[...] = y


def _outffn(x2, o_nsa, o_gla, o_sg, wo, fgain, wg, wu, wd, fin, tm, final):
    T, D = x2.shape
    row = lambda w: pl.BlockSpec((tm, w), lambda i: (i, 0))
    const = lambda a: pl.BlockSpec(a.shape, lambda i: (0,) * a.ndim, pipeline_mode=pl.Buffered(1))
    return pl.pallas_call(
        functools.partial(_outffn_kernel, final=final),
        grid=(T // tm,),
        in_specs=[row(D), row(NSA_W), row(GLA_W), row(SG_W), const(wo), const(fgain),
                  const(wg), const(wu), const(wd), const(fin)],
        out_specs=row(D),
        out_shape=jax.ShapeDtypeStruct((T, D), F32),
        compiler_params=_cparams(("parallel",)),
        name="outproj_ffn",
    )(x2, o_nsa, o_gla, o_sg, wo, fgain, wg, wu, wd, fin)


def kernel(x, attn_norm, w_in, cmp_pos_k, cmp_w1_k, cmp_w2_k, cmp_pos_v, cmp_w1_v, cmp_w2_v,
           gla_w_up, gla_b_up, gla_norm, sg_ln_g, sg_ln_b, sg_w, sg_b, w_out,
           ffn_norm, w_gate, w_up, w_down, final_norm):
    batch, seq, d_model = x.shape
    depth = w_in.shape[0]
    T = batch * seq
    assert seq % GLA_BLK == 0 and seq >= WIN_KEYS and seq % SEL_CH == 0
    tm_in = 512
    tm_ffn = 512

    cos_t, sin_t = _rope_tables(jnp.arange(seq))
    nrow = seq // CMP_STRIDE
    cos_c, sin_c = _rope_tables(jnp.arange(nrow) * CMP_STRIDE + CMP_LEN - 1)
    nsa_consts = _nsa_constants(seq)

    x2 = x.reshape(T, d_model)
    for l in range(depth):
        (qpad, kc, vc, ksl, vsl, kwn, vwn, gexp, gq, gk, gv, ga, gr, uv) = _inproj(
            x2, attn_norm[l][None, :], _prep_w_in(w_in[l]), cos_t, sin_t, seq, tm_in)
        kcmp, vcmp = _compress(
            kc, vc, _prep_cmp_weights(cmp_w1_k[l], cmp_w2_k[l], cmp_pos_k[l]),
            _prep_cmp_weights(cmp_w1_v[l], cmp_w2_v[l], cmp_pos_v[l]), cos_c, sin_c, batch, seq)
        o_nsa = _nsa(qpad, gexp, kcmp, vcmp, ksl, vsl, kwn, vwn, nsa_consts, batch, seq)
        o_gla = _gla(gq, gk, gv, ga, gr, _prep_gla_wup(gla_w_up[l]), gla_b_up[l][None, :],
                     jnp.tile(gla_norm[l], GLA_HEADS)[None, :], batch, seq)
        bias_exp = jnp.repeat(sg_b[l].T, SG_CH, axis=1)
        o_sg = _spatial_gating(uv, sg_ln_g[l][None, :], sg_ln_b[l][None, :], sg_w[l], bias_exp)
        x2 = _outffn(x2, o_nsa, o_gla, o_sg, w_out[l].astype(BF16), ffn_norm[l][None, :],
                     w_gate[l].astype(BF16), w_up[l].astype(BF16), w_down[l].astype(BF16),
                     final_norm[None, :], tm_ffn, final=(l == depth - 1))
    return x2.reshape(batch, seq, d_model)
```

```python
import functools
import math

import numpy as np
import jax
import jax.numpy as jnp
from jax import lax
from jax.experimental import pallas as pl
from jax.experimental.pallas import tpu as pltpu

HEAD_DIM = 64
NSA_HEADS = 8
NSA_KV_GROUPS = 2
NSA_HPG = NSA_HEADS // NSA_KV_GROUPS
CMP_LEN = 32
CMP_STRIDE = 16
CMP_HIDDEN = 256
SEL_BLOCK = 64
SEL_TOPK = 16
WINDOW = 512
GLA_HEADS = 4
GLA_DK = 32
GLA_DV = 64
GLA_RANK = 16
GLA_TAU = 16.0
SG_GROUPS = 4
SG_CH = 64
SG_CHUNK = 128
NSA_W = NSA_HEADS * HEAD_DIM
GLA_W = GLA_HEADS * GLA_DV
SG_W = SG_GROUPS * SG_CH
KV_W = NSA_KV_GROUPS * HEAD_DIM
IN_SIZES = (NSA_W, KV_W, KV_W, KV_W, KV_W, KV_W, KV_W, NSA_HEADS * 3,
            GLA_HEADS * GLA_DK, GLA_HEADS * GLA_DK, GLA_W, GLA_RANK, GLA_W, 2 * SG_W)
ROPE_THETA = 10000.0
NORM_EPS = 1e-6

LANES = 128
NEG_BIG = -1e30
VMEM_LIMIT = 56 * 1024 * 1024

QPAD_W = NSA_HEADS * LANES
GEXP_W = 3 * NSA_W
GLA_QK_W = GLA_HEADS * GLA_DK
GLA_BLK = 256
GLA_SUB = 16
SEL_CH = 1024
WIN_KEYS = WINDOW + 2 * SEL_BLOCK
NSA_ROW_PARTS = 2
RANK_UNROLL = 4
NSA_PAIR = 2
GLA_PAIR = 4

HI = lax.Precision.HIGHEST
F32 = jnp.float32
BF16 = jnp.bfloat16


def _cparams(sem):
    return pltpu.CompilerParams(dimension_semantics=sem, vmem_limit_bytes=VMEM_LIMIT)


def _dot(a, b, precision=None):
    return jnp.dot(a, b, preferred_element_type=F32, precision=precision)


def _dot_nt(a, b, precision=None):
    return lax.dot_general(a, b, (((1,), (1,)), ((), ())),
                           preferred_element_type=F32, precision=precision)


def _dot_tn(a, b, precision=None):
    return lax.dot_general(a, b, (((0,), (0,)), ((), ())),
                           preferred_element_type=F32, precision=precision)


def _gelu_tanh(x):
    c = math.sqrt(2.0 / math.pi)
    return 0.5 * x * (1.0 + jnp.tanh(c * (x + 0.044715 * (x * x * x))))


def _rope_lanes(x, cos, sin_signed):
    n = x.shape[-1]
    lane = lax.broadcasted_iota(jnp.int32, x.shape, 1)
    first_half = (lane % HEAD_DIM) < (HEAD_DIM // 2)
    partner = jnp.where(first_half,
                        pltpu.roll(x, n - HEAD_DIM // 2, 1),
                        pltpu.roll(x, HEAD_DIM // 2, 1))
    return x * cos + partner * sin_signed


_INPROJ_OUT = (
    ("sg", SG_W, F32),
    ("qpad", QPAD_W, BF16), ("kc", KV_W, F32), ("vc", KV_W, F32),
    ("ksl", KV_W, BF16), ("vsl", KV_W, BF16), ("kwn", KV_W, BF16), ("vwn", KV_W, BF16),
    ("gts", LANES, F32), ("gq", GLA_QK_W, F32), ("gk", GLA_QK_W, F32),
    ("gv", GLA_W, F32), ("ga", LANES, F32), ("gr", GLA_W, F32))
_INPROJ_W = sum(w for _, w, _ in _INPROJ_OUT) + SG_W


def _spatial_gate(uv, lg, lb, wts, bias, lane_g):
    a = _gelu_tanh(uv)
    u = a[:, :SG_W]
    v = a[:, SG_W:]
    mu = jnp.mean(v, axis=-1, keepdims=True)
    var = jnp.mean(jnp.square(v - mu), axis=-1, keepdims=True)
    vn = ((v - mu) * lax.rsqrt(var + NORM_EPS) * lg + lb).astype(BF16)
    s = bias
    for g in range(SG_GROUPS):
        s = s + jnp.where(lane_g == g, _dot(wts[g], vn), 0.0)
    return u * s


def _inproj_kernel(x_ref, g_ref, w_ref, cos_ref, sin_ref, lg_ref, lb_ref, ws_ref, sb_ref, *out_refs):
    x = x_ref[...]
    hn = x * lax.rsqrt(jnp.mean(x * x, axis=-1, keepdims=True) + NORM_EPS) * g_ref[...]
    z = _dot(hn.astype(BF16), w_ref[...])
    cos = cos_ref[...]
    sin = sin_ref[...]
    off = 0
    for (name, width, dtype), o_ref in zip(_INPROJ_OUT, out_refs):
        if name in ("qpad", "ksl", "kwn"):
            for j in range(width // LANES):
                blk = _rope_lanes(z[:, off + j * LANES: off + (j + 1) * LANES], cos, sin)
                if name == "qpad":
                    blk = blk * (HEAD_DIM ** -0.5)
                o_ref[:, j * LANES:(j + 1) * LANES] = blk.astype(dtype)
        elif name == "sg":
            n = ws_ref.shape[1]
            ri = lax.broadcasted_iota(jnp.int32, (n, n), 0)
            ci = lax.broadcasted_iota(jnp.int32, (n, n), 1)
            lane_g = lax.broadcasted_iota(jnp.int32, (1, SG_W), 1) // SG_CH
            wts = [jnp.where(ci <= ri, ws_ref[g], 0.0).astype(BF16) for g in range(SG_GROUPS)]
            for c in range(x.shape[0] // n):
                rs = slice(c * n, (c + 1) * n)
                o_ref[rs, :] = _spatial_gate(z[rs, off:off + 2 * SG_W], lg_ref[...], lb_ref[...], wts,
                                             sb_ref[...], lane_g)
        else:
            o_ref[...] = z[:, off:off + width].astype(dtype)
        off += 2 * SG_W if name == "sg" else width


def _inproj(x2, gain, w_p, cos_t, sin_t, sg_params, seq, tm):
    T, D = x2.shape
    nper = seq // tm
    assert tm % SG_CHUNK == 0
    out_shape = [jax.ShapeDtypeStruct((T, w), dt) for _, w, dt in _INPROJ_OUT]
    out_specs = [pl.BlockSpec((tm, w), lambda i: (i, 0)) for _, w, _ in _INPROJ_OUT]
    full = lambda a: pl.BlockSpec(a.shape, lambda i: (0,) * a.ndim)
    return pl.pallas_call(
        _inproj_kernel,
        grid=(T // tm,),
        in_specs=[
            pl.BlockSpec((tm, D), lambda i: (i, 0)),
            pl.BlockSpec((1, D), lambda i: (0, 0)),
            pl.BlockSpec((D, _INPROJ_W), lambda i: (0, 0), pipeline_mode=pl.Buffered(1)),
            pl.BlockSpec((tm, LANES), lambda i: (i % nper, 0)),
            pl.BlockSpec((tm, LANES), lambda i: (i % nper, 0)),
        ] + [full(a) for a in sg_params],
        out_specs=out_specs,
        out_shape=out_shape,
        compiler_params=_cparams(("parallel",)),
        name="inproj",
    )(x2, gain, w_p, cos_t, sin_t, *sg_params)


def _prep_w_in(w_in):
    offs = np.cumsum((0,) + IN_SIZES)
    seg = [w_in[:, offs[k]:offs[k + 1]] for k in range(len(IN_SIZES))]
    (wq, wkc, wvc, wksl, wvsl, wkwn, wvwn, wg, wgq, wgk, wgv, wga, wgr, wuv) = seg
    D = w_in.shape[0]
    zeros64 = jnp.zeros((D, HEAD_DIM), w_in.dtype)
    qcols = []
    for h in range(NSA_HEADS):
        wh = wq[:, h * HEAD_DIM:(h + 1) * HEAD_DIM]
        qcols += [wh, zeros64] if h // NSA_HPG == 0 else [zeros64, wh]
    gcols = [jnp.pad(wg, ((0, 0), (0, LANES - wg.shape[1])))]
    wga_p = jnp.pad(jnp.concatenate([wga] * 6, axis=1), ((0, 0), (0, LANES - 6 * GLA_RANK)))
    return jnp.concatenate([wuv] + qcols + [wkc, wvc, wksl, wvsl, wkwn, wvwn] + gcols
                           + [wgq, wgk, wgv, wga_p, wgr], axis=1).astype(BF16)


def _rope_tables(pos):
    half = HEAD_DIM // 2
    inv = 1.0 / (ROPE_THETA ** (jnp.arange(half, dtype=F32) / half))
    ang = pos.astype(F32)[:, None] * inv[None, :]
    cos = jnp.cos(ang)
    sin = jnp.sin(ang)
    cos_t = jnp.concatenate([cos, cos, cos, cos], axis=1)
    sin_t = jnp.concatenate([-sin, sin, -sin, sin], axis=1)
    return cos_t, sin_t


def _cmp_kernel(xk_ref, xv_ref, wek_ref, wev_ref, w1k_ref, w1v_ref, pk_ref, pv_ref,
                w2k_ref, w2v_ref, cos_ref, sin_ref, ok_ref, ov_ref):
    nrow = xk_ref.shape[1]
    row = lax.broadcasted_iota(jnp.int32, (nrow, LANES), 0)

    def compress(x_ref, we_ref, w1_ref, p_ref, w2_ref):
        h = _dot(x_ref[0].astype(BF16), we_ref[...])
        posb = _dot(p_ref[...].astype(BF16), w1_ref[...])[0:1, :]
        y = jnp.zeros((nrow, LANES), F32)
        for g in range(NSA_KV_GROUPS):
            a = h[:, g * CMP_HIDDEN:(g + 1) * CMP_HIDDEN]
            b = h[:, (NSA_KV_GROUPS + g) * CMP_HIDDEN:(NSA_KV_GROUPS + g + 1) * CMP_HIDDEN]
            hid = a + pltpu.roll(b, nrow - 1, 0) + posb
            y = y + _dot(_gelu_tanh(hid).astype(BF16), w2_ref[g])
        return y

    yk = _rope_lanes(compress(xk_ref, wek_ref, w1k_ref, pk_ref, w2k_ref), cos_ref[...], sin_ref[...])
    yv = compress(xv_ref, wev_ref, w1v_ref, pv_ref, w2v_ref)
    keep = row < nrow - 1
    ok_ref[0] = jnp.where(keep, yk, 0.0)
    ov_ref[0] = jnp.where(keep, yv, 0.0)


def _prep_cmp_weights(w1, w2, pos):
    half = CMP_LEN // CMP_STRIDE
    H = w1.shape[1]
    w1r = w1.reshape(half, CMP_STRIDE, HEAD_DIM, H)
    cols = []
    for a in range(half):
        for g in range(NSA_KV_GROUPS):
            blk = jnp.zeros((CMP_STRIDE, NSA_KV_GROUPS, HEAD_DIM, H), w1.dtype)
            blk = blk.at[:, g].set(w1r[a])
            cols.append(blk.reshape(CMP_STRIDE * KV_W, H))
    wexp = jnp.concatenate(cols, axis=1).astype(BF16)
    w2p = jnp.stack([jnp.pad(w2, ((0, 0), (g * HEAD_DIM, KV_W - (g + 1) * HEAD_DIM)))
                     for g in range(NSA_KV_GROUPS)]).astype(BF16)
    posf = jnp.pad(pos.reshape(1, CMP_LEN * HEAD_DIM), ((0, 7), (0, 0)))
    return wexp, w1.astype(BF16), posf, w2p


def _compress(kc, vc, wk, wv, cos_c, sin_c, batch, seq):
    nrow = seq // CMP_STRIDE
    xk = kc.reshape(batch, nrow, CMP_STRIDE * KV_W)
    xv = vc.reshape(batch, nrow, CMP_STRIDE * KV_W)
    wek, w1k, pk, w2k = wk
    wev, w1v, pv, w2v = wv
    full = lambda a: pl.BlockSpec(a.shape, lambda b: (0,) * a.ndim)
    xspec = pl.BlockSpec((1, nrow, CMP_STRIDE * KV_W), lambda b: (b, 0, 0))
    ospec = pl.BlockSpec((1, nrow, KV_W), lambda b: (b, 0, 0))
    return pl.pallas_call(
        _cmp_kernel,
        grid=(batch,),
        in_specs=[xspec, xspec, full(wek), full(wev), full(w1k), full(w1v), full(pk), full(pv),
                  full(w2k), full(w2v), full(cos_c), full(sin_c)],
        out_specs=[ospec, ospec],
        out_shape=[jax.ShapeDtypeStruct((batch, nrow, KV_W), F32)] * 2,
        compiler_params=_cparams(("parallel",)),
        name="compress",
    )(xk, xv, wek, wev, w1k, w1v, pk, pv, w2k, w2v, cos_c, sin_c)


def _unpad_heads(acc):
    qb = SEL_BLOCK
    lane = lax.broadcasted_iota(jnp.int32, (qb, LANES), 1)
    low = lane < HEAD_DIM
    outs = []
    for j in range(NSA_HEADS // 2):
        a = acc[(2 * j) * qb:(2 * j + 1) * qb]
        b = acc[(2 * j + 1) * qb:(2 * j + 2) * qb]
        if (2 * j) // NSA_HPG == 0:
            outs.append(jnp.where(low, a, pltpu.roll(b, HEAD_DIM, 1)))
        else:
            outs.append(jnp.where(low, pltpu.roll(a, HEAD_DIM, 1), b))
    return jnp.concatenate(outs, axis=1)


def _nsa_kernel(q_ref, g_ref, kc_ref, vc_ref, ksl_ref, vsl_ref, kwn_ref, vwn_ref,
                ktab_ref, kdiag_ref, kwtab_ref, qc_ref, qw_ref, eye_ref, gexp_ref, impt_ref, o_ref,
                imp_scr, m_scr, acc_scr, win_scr, kd_scr, qs_scr, sc_scr, cmp_scr):
    qb = SEL_BLOCK
    rows = NSA_HEADS * qb
    pair = range(q_ref.shape[0])
    i = pl.program_id(1)
    t0 = i * qb
    t = t0 + lax.broadcasted_iota(jnp.int32, (rows, 1), 0) % qb
    ones = jnp.ones((SEL_CH, LANES), BF16)
    parts = [slice(p * rows // NSA_ROW_PARTS, (p + 1) * rows // NSA_ROW_PARTS) for p in range(NSA_ROW_PARTS)]
    qbf = [jnp.concatenate([q_ref[e, :, h * LANES:(h + 1) * LANES] for h in range(NSA_HEADS)], axis=0)
           for e in pair]

    nblk_w = WIN_KEYS // qb
    ib = jnp.minimum(i, nblk_w - 1)
    w0 = pl.multiple_of((i - ib) * qb, qb)
    toff = pl.multiple_of((nblk_w - 1 - ib) * qb, qb)
    kwmask = kwtab_ref[pl.ds(toff, WIN_KEYS), :]
    for e in pair:
        kw = jnp.concatenate([kwn_ref[e, 0, pl.ds(w0, WIN_KEYS), :], kwmask], axis=1)
        vw = jnp.concatenate([vwn_ref[e, 0, pl.ds(w0, WIN_KEYS), :], ones[0:WIN_KEYS]], axis=1)
        q_win = jnp.concatenate([qbf[e], qw_ref[...]], axis=1)
        for rs in parts:
            sw = _dot_nt(q_win[rs], kw)
            ew = jnp.exp(sw - jnp.max(sw, axis=1, keepdims=True)).astype(BF16)
            ow = _dot(ew, vw)
            win_scr[e, rs, :] = ow[:, 0:LANES] / ow[:, LANES:2 * LANES]

    ncmp = kc_ref.shape[2]
    cend = lax.broadcasted_iota(jnp.int32, (1, ncmp), 1) * CMP_STRIDE + (CMP_LEN - 1)
    vis = cend <= t
    imap = impt_ref[...].astype(BF16)
    imap2 = jnp.concatenate([imap, imap], axis=1)
    for e in pair:
        s = _dot_nt(qbf[e], kc_ref[e, 0].astype(BF16))
        sm = jnp.where(vis, s, NEG_BIG)
        mx = jnp.max(sm, axis=1, keepdims=True)
        ex = jnp.where(vis, jnp.exp(sm - mx), 0.0)
        d = jnp.sum(ex, axis=1, keepdims=True)
        p = ex / jnp.where(d > 0, d, 1.0)
        cmp_scr[e] = _dot(p.astype(BF16), vc_ref[e, 0].astype(BF16))
        psum = jnp.concatenate(
            [sum(p[(g * NSA_HPG + h) * qb:(g * NSA_HPG + h + 1) * qb] for h in range(NSA_HPG))
             for g in range(NSA_KV_GROUPS)], axis=0)
        p_hi = psum.astype(BF16)
        p_lo = (psum - p_hi.astype(F32)).astype(BF16)
        imp_scr[e] = jnp.abs(_dot_nt(imap2, jnp.concatenate([p_hi, p_lo], axis=1)))

    nsbp = imp_scr.shape[1]
    keys = [pltpu.bitcast(imp_scr[e], jnp.int32) for e in pair]
    n_id = lax.broadcasted_iota(jnp.int32, (nsbp, 1), 0)
    n_full = lax.broadcasted_iota(jnp.int32, (nsbp, LANES), 0)
    valid = n_id <= i
    forced = (n_id == 0) | (n_id == i) | (n_id == i - 1)
    nforced = 1 + (i >= 1).astype(jnp.int32) + (i >= 2).astype(jnp.int32)

    def rank_body(j, cnts):
        cnts = list(cnts)
        for u in range(RANK_UNROLL):
            m = 1 + RANK_UNROLL * j + u
            tadj = jnp.right_shift(n_full - 1 - m, 31)
            for e in pair:
                rowk = pltpu.bitcast(imp_scr[e, pl.ds(jnp.minimum(m, nsbp - 1), 1), :], jnp.int32)
                rowk = jnp.where(m <= i - 2, rowk, -1)
                cnts[e] = cnts[e] + jnp.right_shift(rowk - keys[e] + tadj, 31)
        return tuple(cnts)

    ntrip = (jnp.maximum(i - 2, 0) + RANK_UNROLL - 1) // RANK_UNROLL
    cnts = lax.fori_loop(0, ntrip, rank_body, tuple(jnp.zeros(keys[0].shape, jnp.int32) for _ in pair))
    budget = SEL_TOPK - nforced - ntrip * RANK_UNROLL

    for e in pair:
        sel = valid & (forced | (cnts[e] < budget))
        notsel_t = jnp.where(sel, 0.0, 1.0)
        notsel = _dot_tn(notsel_t, eye_ref[...])
        notsel = jnp.concatenate([notsel[g * qb:(g + 1) * qb]
                                  for g in range(NSA_KV_GROUPS) for _ in range(NSA_HPG)], axis=0)
        qs_scr[e] = jnp.concatenate([qbf[e], (notsel + qc_ref[...]).astype(BF16)], axis=1)
    m_scr[...] = jnp.full(m_scr.shape, NEG_BIG, F32)
    acc_scr[...] = jnp.zeros(acc_scr.shape, F32)
    half = SEL_CH // 2
    nhalf = i // (half // qb) + 1
    lead = nhalf % 2
    nfull = nhalf // 2
    base = lead * half
    k0_last = pl.multiple_of(jnp.maximum(base + (nfull - 1) * SEL_CH, 0), half)
    off = pl.multiple_of(i * qb - k0_last, qb)
    kd_scr[...] = ktab_ref[pl.ds(k0_last, SEL_CH), :]
    kd_scr[pl.ds(off, qb), :] = kd_scr[pl.ds(off, qb), :] + kdiag_ref[...]

    def scores(slot, k0, size, kmask):
        for e in pair:
            k = jnp.concatenate([ksl_ref[e, 0, pl.ds(k0, size), :], kmask], axis=1)
            for rs in parts:
                sc = _dot_nt(qs_scr[e, rs, :], k)
                sc_scr[e, slot, rs, 0:size] = sc
                mx = m_scr[e, rs, :]
                for j in range(size // LANES):
                    mx = jnp.maximum(mx, sc[:, j * LANES:(j + 1) * LANES])
                m_scr[e, rs, :] = mx

    def values(slot, k0, size):
        for e in pair:
            v = jnp.concatenate([vsl_ref[e, 0, pl.ds(k0, size), :], ones[0:size]], axis=1)
            for rs in parts:
                mb = m_scr[e, rs, :]
                pr = jnp.concatenate([jnp.exp(sc_scr[e, slot, rs, j * LANES:(j + 1) * LANES] - mb)
                                      for j in range(size // LANES)], axis=1).astype(BF16)
                acc_scr[e, rs, :] += _dot(pr, v)

    @pl.when(lead == 1)
    def _():
        scores(0, 0, half, jnp.where(nfull == 0, kd_scr[0:half, :], ktab_ref[0:half, :]))

    def score_body(c, carry):
        k0 = pl.multiple_of(base + c * SEL_CH, half)
        scores(1 + c, k0, SEL_CH, jnp.where(c == nfull - 1, kd_scr[...], ktab_ref[pl.ds(k0, SEL_CH), :]))
        return carry

    lax.fori_loop(0, nfull, score_body, 0)
    for e in pair:
        m_scr[e] = jnp.broadcast_to(jnp.max(m_scr[e], axis=1, keepdims=True), (rows, LANES))

    @pl.when(lead == 1)
    def _():
        values(0, 0, half)

    def value_body(c, carry):
        values(1 + c, pl.multiple_of(base + c * SEL_CH, half), SEL_CH)
        return carry

    lax.fori_loop(0, nfull, value_body, 0)

    for e in pair:
        o_slc = acc_scr[e, :, 0:LANES] / acc_scr[e, :, LANES:2 * LANES]
        g_hi = jax.nn.sigmoid(g_ref[e])
        g_lo = g_hi - g_hi.astype(BF16).astype(F32)
        gate = _dot(jnp.concatenate([g_hi.astype(BF16), g_lo.astype(BF16)], axis=1), gexp_ref[...])
        o_ref[e] = (gate[:, 0:NSA_W] * _unpad_heads(cmp_scr[e])
                    + gate[:, NSA_W:2 * NSA_W] * _unpad_heads(o_slc)
                    + gate[:, 2 * NSA_W:3 * NSA_W] * _unpad_heads(win_scr[e]))


def _nsa(qpad, gts, kcmp, vcmp, ksl, vsl, kwn, vwn, consts, batch, seq):
    qb = SEL_BLOCK
    nq = seq // qb
    rows = NSA_HEADS * qb
    npair = NSA_PAIR if batch % NSA_PAIR == 0 else 1
    nbh = batch // npair
    tok = lambda a: a.reshape(npair, nbh * seq, a.shape[-1])
    per = lambda a: a.reshape(npair, nbh, a.shape[-2], KV_W)
    tok_spec = lambda w: pl.BlockSpec((npair, qb, w), lambda bh, i: (0, bh * nq + i, 0))
    per_spec = lambda n: pl.BlockSpec((npair, 1, n, KV_W), lambda bh, i: (0, bh, 0, 0))
    full = lambda a: pl.BlockSpec(a.shape, lambda bh, i: (0,) * a.ndim)
    out = pl.pallas_call(
        _nsa_kernel,
        grid=(nbh, nq),
        in_specs=[
            tok_spec(QPAD_W), tok_spec(LANES),
            per_spec(kcmp.shape[1]), per_spec(vcmp.shape[1]),
            per_spec(seq), per_spec(seq), per_spec(seq), per_spec(seq),
        ] + [full(c) for c in consts],
        out_specs=tok_spec(NSA_W),
        out_shape=jax.ShapeDtypeStruct((npair, nbh * seq, NSA_W), F32),
        scratch_shapes=[
            pltpu.VMEM((npair, consts[-1].shape[0], NSA_KV_GROUPS * qb), F32),
            pltpu.VMEM((npair, rows, LANES), F32),
            pltpu.VMEM((npair, rows, 2 * LANES), F32),
            pltpu.VMEM((npair, rows, LANES), F32),
            pltpu.VMEM((SEL_CH, LANES), BF16),
            pltpu.VMEM((npair, rows, 2 * LANES), BF16),
            pltpu.VMEM((npair, seq // SEL_CH + 1, rows, SEL_CH), F32),
            pltpu.VMEM((npair, rows, LANES), F32),
        ],
        compiler_params=_cparams(("parallel", "arbitrary")),
        name="nsa",
    )(tok(qpad), tok(gts), per(kcmp), per(vcmp), per(ksl.reshape(batch, seq, KV_W)),
      per(vsl.reshape(batch, seq, KV_W)), per(kwn.reshape(batch, seq, KV_W)),
      per(vwn.reshape(batch, seq, KV_W)), *consts)
    return out.reshape(batch * seq, NSA_W)


def _nsa_constants(seq):
    qb = SEL_BLOCK
    nc_pad = seq // CMP_STRIDE
    nsb = seq // qb
    assert nsb <= qb
    nsb_pad = max(nsb, 8)
    cs = np.arange(nc_pad) * CMP_STRIDE
    ce = cs + CMP_LEN
    bs = np.arange(nsb_pad) * qb
    be = bs + qb
    ov = np.clip(np.minimum(ce[None, :], be[:, None]) - np.maximum(cs[None, :], bs[:, None]), 0, None)
    impt = ov / CMP_LEN
    pos = np.arange(seq)
    lane = np.arange(LANES)[None, :]
    ktab = np.where(lane == pos[:, None] // qb, NEG_BIG, 0.0)
    kdiag = np.where(lane == qb + np.arange(qb)[:, None], NEG_BIG, 0.0)
    last = WIN_KEYS // qb - 1
    x = np.arange(WIN_KEYS + last * qb)[:, None]
    kb, j = x // qb, x % qb
    kwtab = np.where((((kb == 0) | (kb > last)) & (lane == 0)) | ((kb == 1) & (lane == j))
                     | ((kb == last) & (lane == qb + j)), NEG_BIG, 0.0)
    r = (np.arange(NSA_HEADS * qb) % qb)[:, None]
    upper = (lane >= qb) & (lane - qb > r)
    qc = np.where(upper, 1.0, 0.0)
    qw = np.where(upper | ((lane < qb) & (lane <= r)), 1.0, 0.0)
    eye = np.eye(nsb_pad, LANES)
    col = np.arange(GEXP_W)[None, :]
    src = np.arange(LANES)[:, None]
    rep = (src == (col % NSA_W) // HEAD_DIM * 3 + col // NSA_W) & (src < 3 * NSA_HEADS)
    gexp = np.concatenate([rep, rep], axis=0)
    return (jnp.asarray(ktab, BF16), jnp.asarray(kdiag, BF16), jnp.asarray(kwtab, BF16),
            jnp.asarray(qc, F32), jnp.asarray(qw, BF16), jnp.asarray(eye, F32), jnp.asarray(gexp, BF16),
            jnp.asarray(impt, F32))


def _split3(x):
    hi = x.astype(BF16)
    r1 = x - hi.astype(F32)
    mid = r1.astype(BF16)
    lo = (r1 - mid.astype(F32)).astype(BF16)
    return hi, mid, lo


def _gla_kernel(q_ref, k_ref, v_ref, a_ref, r_ref, w6_ref, bup_ref, ng_ref, o_ref, st_ref):
    pair = range(q_ref.shape[0])
    blk = q_ref.shape[1]
    nsub = blk // GLA_SUB

    @pl.when(pl.program_id(1) == 0)
    def _():
        st_ref[...] = jnp.zeros(st_ref.shape, F32)

    ri = lax.broadcasted_iota(jnp.int32, (blk, blk), 0)
    ci = lax.broadcasted_iota(jnp.int32, (blk, blk), 1)
    ltri = jnp.where(ci <= ri, 1.0, 0.0).astype(BF16)
    ltri3 = jnp.concatenate([ltri, ltri, ltri], axis=1)
    lane_rep = lax.broadcasted_iota(jnp.int32, (1, LANES), 1) // GLA_RANK
    srow = lax.broadcasted_iota(jnp.int32, st_ref.shape[1:], 0) // GLA_DV
    scol = lax.broadcasted_iota(jnp.int32, st_ref.shape[1:], 1) // GLA_DK
    krow = lax.broadcasted_iota(jnp.int32, (blk, 1), 0)
    lane_qk = lax.broadcasted_iota(jnp.int32, (1, GLA_QK_W), 1) // GLA_DK
    lane_v = lax.broadcasted_iota(jnp.int32, (1, GLA_W), 1) // GLA_DV
    qrow = lax.broadcasted_iota(jnp.int32, (GLA_HEADS * GLA_SUB, 1), 0) % GLA_SUB
    kcol = lax.broadcasted_iota(jnp.int32, (1, blk), 1)
    gi = lax.broadcasted_iota(jnp.int32, (GLA_W, GLA_W), 0) // GLA_DV
    gj = lax.broadcasted_iota(jnp.int32, (GLA_W, GLA_W), 1) // GLA_DV
    gmean = jnp.where(gi == gj, 1.0 / GLA_DV, 0.0).astype(BF16)
    gmean2 = jnp.concatenate([gmean, gmean], axis=0)

    for e in pair:
        a_hi, a_mid, a_lo = _split3(a_ref[e])
        a6 = jnp.where(lane_rep < 3, a_hi, jnp.where(lane_rep < 5, a_mid, a_lo))
        x = _dot(a6, w6_ref[...]) + bup_ref[...]
        g = (jnp.minimum(x, 0.0) - jnp.log1p(jnp.exp(-jnp.abs(x)))) / GLA_TAU
        b = _dot(ltri3, jnp.concatenate(_split3(g), axis=0))
        blast = b[blk - 1:blk, :]
        q = q_ref[e] * (GLA_DK ** -0.5)
        k = k_ref[e]
        v = v_ref[e]
        vb = v.astype(BF16)

        st = st_ref[e]
        o = _dot_nt((q * jnp.exp(b)).astype(BF16), st.astype(BF16))
        khat = k * jnp.exp(blast - b)
        st_ref[e] = st * jnp.exp(blast) + jnp.where(srow == scol, _dot_tn(v, khat), 0.0)

        for c in range(nsub):
            lo = c * GLA_SUB
            ref_b = b[lo:lo + 1, :]
            qt = q[lo:lo + GLA_SUB] * jnp.exp(b[lo:lo + GLA_SUB] - ref_b)
            kt = k * jnp.exp(jnp.where(krow < lo + GLA_SUB, ref_b - b, 0.0))
            qs = jnp.concatenate([jnp.where(lane_qk == h, qt, 0.0) for h in range(GLA_HEADS)], axis=0)
            a = _dot_nt(qs.astype(BF16), kt.astype(BF16))
            a = jnp.where(kcol <= lo + qrow, a, 0.0)
            r = _dot(a.astype(BF16), vb)
            oi = sum(jnp.where(lane_v == h, r[h * GLA_SUB:(h + 1) * GLA_SUB], 0.0) for h in range(GLA_HEADS))
            o_ref[e, lo:lo + GLA_SUB, :] = o[lo:lo + GLA_SUB] + oi

    for e in pair:
        o = o_ref[e]
        oo = o * o
        oo_hi = oo.astype(BF16)
        oo_lo = (oo - oo_hi.astype(F32)).astype(BF16)
        ms = _dot(jnp.concatenate([oo_hi, oo_lo], axis=1), gmean2)
        rr = r_ref[e]
        o_ref[e] = o * lax.rsqrt(ms + NORM_EPS) * ng_ref[...] * (rr * jax.nn.sigmoid(rr))


def _prep_gla_wup(w_up):
    hi = w_up.astype(BF16)
    r1 = w_up - hi.astype(F32)
    mid = r1.astype(BF16)
    lo = (r1 - mid.astype(F32)).astype(BF16)
    w6 = jnp.concatenate([hi, mid, lo, hi, mid, hi], axis=0)
    return jnp.pad(w6, ((0, LANES - w6.shape[0]), (0, 0)))


def _gla(gq, gk, gv, ga, gr, w6, bup, ng, batch, seq):
    blk = min(GLA_BLK, seq)
    nb = seq // blk
    npair = GLA_PAIR if batch % GLA_PAIR == 0 else 1
    nbh = batch // npair
    tok = lambda a: a.reshape(npair, nbh * seq, a.shape[-1])
    spec = lambda w: pl.BlockSpec((npair, blk, w), lambda bh, j: (0, bh * nb + j, 0))
    full = lambda a: pl.BlockSpec(a.shape, lambda bh, j: (0,) * a.ndim)
    out = pl.pallas_call(
        _gla_kernel,
        grid=(nbh, nb),
        in_specs=[spec(GLA_QK_W), spec(GLA_QK_W), spec(GLA_W), spec(LANES), spec(GLA_W),
                  full(w6), full(bup), full(ng)],
        out_specs=spec(GLA_W),
        out_shape=jax.ShapeDtypeStruct((npair, nbh * seq, GLA_W), F32),
        scratch_shapes=[pltpu.VMEM((npair, GLA_W, GLA_QK_W), F32)],
        compiler_params=_cparams(("parallel", "arbitrary")),
        name="gla",
    )(tok(gq), tok(gk), tok(gv), tok(ga), tok(gr), w6, bup, ng)
    return out.reshape(batch * seq, GLA_W)


def _outffn_kernel(x_ref, on_ref, og_ref, os_ref, wo_ref, fg_ref, wg_ref, wu_ref, wd_ref,
                   fin_ref, o_ref, *, final):
    x = x_ref[...]
    x = x + _dot(on_ref[...].astype(BF16), wo_ref[0:NSA_W, :])
    x = x + _dot(og_ref[...].astype(BF16), wo_ref[NSA_W:NSA_W + GLA_W, :])
    x = x + _dot(os_ref[...].astype(BF16), wo_ref[NSA_W + GLA_W:, :])
    hn = (x * lax.rsqrt(jnp.mean(x * x, axis=-1, keepdims=True) + NORM_EPS) * fg_ref[...]).astype(BF16)
    gt = _dot(hn, wg_ref[...])
    up = _dot(hn, wu_ref[...])
    y = x + _dot((gt * jax.nn.sigmoid(gt) * up).astype(BF16), wd_ref[...])
    if final:
        y = y * lax.rsqrt(jnp.mean(y * y, axis=-1, keepdims=True) + NORM_EPS) * fin_ref[...]
    o_ref[...] = y


def _outffn(x2, o_nsa, o_gla, o_sg, wo, fgain, wg, wu, wd, fin, tm, final):
    T, D = x2.shape
    row = lambda w: pl.BlockSpec((tm, w), lambda i: (i, 0))
    const = lambda a: pl.BlockSpec(a.shape, lambda i: (0,) * a.ndim, pipeline_mode=pl.Buffered(1))
    return pl.pallas_call(
        functools.partial(_outffn_kernel, final=final),
        grid=(T // tm,),
        in_specs=[row(D), row(NSA_W), row(GLA_W), row(SG_W), const(wo), const(fgain),
                  const(wg), const(wu), const(wd), const(fin)],
        out_specs=row(D),
        out_shape=jax.ShapeDtypeStruct((T, D), F32),
        compiler_params=_cparams(("parallel",)),
        name="outproj_ffn",
    )(x2, o_nsa, o_gla, o_sg, wo, fgain, wg, wu, wd, fin)


def kernel(x, attn_norm, w_in, cmp_pos_k, cmp_w1_k, cmp_w2_k, cmp_pos_v, cmp_w1_v, cmp_w2_v,
           gla_w_up, gla_b_up, gla_norm, sg_ln_g, sg_ln_b, sg_w, sg_b, w_out,
           ffn_norm, w_gate, w_up, w_down, final_norm):
    batch, seq, d_model = x.shape
    depth = w_in.shape[0]
    T = batch * seq
    assert seq % GLA_BLK == 0 and seq >= WIN_KEYS and seq % SEL_CH == 0
    tm_in = 512
    tm_ffn = 512

    cos_t, sin_t = _rope_tables(jnp.arange(seq))
    nrow = seq // CMP_STRIDE
    cos_c, sin_c = _rope_tables(jnp.arange(nrow) * CMP_STRIDE + CMP_LEN - 1)
    nsa_consts = _nsa_constants(seq)

    x2 = x.reshape(T, d_model)
    for l in range(depth):
        sg_params = (sg_ln_g[l][None, :], sg_ln_b[l][None, :], sg_w[l],
                     jnp.repeat(sg_b[l].T, SG_CH, axis=1))
        (o_sg, qpad, kc, vc, ksl, vsl, kwn, vwn, gts, gq, gk, gv, ga, gr) = _inproj(
            x2, attn_norm[l][None, :], _prep_w_in(w_in[l]), cos_t, sin_t, sg_params, seq, tm_in)
        kcmp, vcmp = _compress(
            kc, vc, _prep_cmp_weights(cmp_w1_k[l], cmp_w2_k[l], cmp_pos_k[l]),
            _prep_cmp_weights(cmp_w1_v[l], cmp_w2_v[l], cmp_pos_v[l]), cos_c, sin_c, batch, seq)
        o_nsa = _nsa(qpad, gts, kcmp, vcmp, ksl, vsl, kwn, vwn, nsa_consts, batch, seq)
        o_gla = _gla(gq, gk, gv, ga, gr, _prep_gla_wup(gla_w_up[l]), gla_b_up[l][None, :],
                     jnp.tile(gla_norm[l], GLA_HEADS)[None, :], batch, seq)
        x2 = _outffn(x2, o_nsa, o_gla, o_sg, w_out[l].astype(BF16), ffn_norm[l][None, :],
                     w_gate[l].astype(BF16), w_up[l].astype(BF16), w_down[l].astype(BF16),
                     final_norm[None, :], tm_ffn, final=(l == depth - 1))
    return x2.reshape(batch, seq, d_model)
```

```python
import functools
import math

import numpy as np
import jax
import jax.numpy as jnp
from jax import lax
from jax.experimental import pallas as pl
from jax.experimental.pallas import tpu as pltpu

HEAD_DIM = 64
NSA_HEADS = 8
NSA_KV_GROUPS = 2
NSA_HPG = NSA_HEADS // NSA_KV_GROUPS
CMP_LEN = 32
CMP_STRIDE = 16
CMP_HIDDEN = 256
SEL_BLOCK = 64
SEL_TOPK = 16
WINDOW = 512
GLA_HEADS = 4
GLA_DK = 32
GLA_DV = 64
GLA_RANK = 16
GLA_TAU = 16.0
SG_GROUPS = 4
SG_CH = 64
SG_CHUNK = 128
NSA_W = NSA_HEADS * HEAD_DIM
GLA_W = GLA_HEADS * GLA_DV
SG_W = SG_GROUPS * SG_CH
KV_W = NSA_KV_GROUPS * HEAD_DIM
IN_SIZES = (NSA_W, KV_W, KV_W, KV_W, KV_W, KV_W, KV_W, NSA_HEADS * 3,
            GLA_HEADS * GLA_DK, GLA_HEADS * GLA_DK, GLA_W, GLA_RANK, GLA_W, 2 * SG_W)
ROPE_THETA = 10000.0
NORM_EPS = 1e-6

LANES = 128
NEG_BIG = -1e30
VMEM_LIMIT = 56 * 1024 * 1024

QPAD_W = NSA_HEADS * LANES
GEXP_W = 3 * NSA_W
GLA_QK_W = GLA_HEADS * GLA_DK
GLA_BLK = 256
GLA_SUB = 16
SEL_CH = 1024
WIN_KEYS = WINDOW + 2 * SEL_BLOCK
NSA_ROW_PARTS = 1
RANK_UNROLL = 4
NSA_PAIR = 2
GLA_PAIR = 4

ROW_TILE = 512
GLA_A_COPIES = 6
F32 = jnp.float32
BF16 = jnp.bfloat16


def _cparams(sem):
    return pltpu.CompilerParams(dimension_semantics=sem, vmem_limit_bytes=VMEM_LIMIT)


def _dot(a, b, precision=None):
    return jnp.dot(a, b, preferred_element_type=F32, precision=precision)


def _dot_nt(a, b, precision=None):
    return lax.dot_general(a, b, (((1,), (1,)), ((), ())),
                           preferred_element_type=F32, precision=precision)


def _dot_tn(a, b, precision=None):
    return lax.dot_general(a, b, (((0,), (0,)), ((), ())),
                           preferred_element_type=F32, precision=precision)


def _gelu_tanh(x):
    c = math.sqrt(2.0 / math.pi)
    return 0.5 * x * (1.0 + jnp.tanh(c * (x + 0.044715 * (x * x * x))))


def _rope_lanes(x, cos, sin_signed):
    n = x.shape[-1]
    lane = lax.broadcasted_iota(jnp.int32, x.shape, 1)
    first_half = (lane % HEAD_DIM) < (HEAD_DIM // 2)
    partner = jnp.where(first_half,
                        pltpu.roll(x, n - HEAD_DIM // 2, 1),
                        pltpu.roll(x, HEAD_DIM // 2, 1))
    return x * cos + partner * sin_signed


_INPROJ_OUT = (
    ("sg", SG_W, F32),
    ("qpad", QPAD_W, BF16), ("kc", KV_W, F32), ("vc", KV_W, F32),
    ("ksl", KV_W, BF16), ("vsl", KV_W, BF16), ("kwn", KV_W, BF16), ("vwn", KV_W, BF16),
    ("gts", LANES, F32), ("gq", GLA_QK_W, F32), ("gk", GLA_QK_W, F32),
    ("gv", GLA_W, F32), ("ga", LANES, F32), ("gr", GLA_W, F32))
_INPROJ_W = sum(w for _, w, _ in _INPROJ_OUT) + SG_W


def _spatial_gate(uv, lg, lb, wts, bias, lane_g):
    a = _gelu_tanh(uv)
    u = a[:, :SG_W]
    v = a[:, SG_W:]
    mu = jnp.mean(v, axis=-1, keepdims=True)
    var = jnp.mean(jnp.square(v - mu), axis=-1, keepdims=True)
    vn = ((v - mu) * lax.rsqrt(var + NORM_EPS) * lg + lb).astype(BF16)
    s = bias
    for g in range(SG_GROUPS):
        s = s + jnp.where(lane_g == g, _dot(wts[g], vn), 0.0)
    return u * s


def _inproj_kernel(x_ref, g_ref, w_ref, cos_ref, sin_ref, lg_ref, lb_ref, ws_ref, sb_ref, *out_refs):
    x = x_ref[...]
    hn = x * lax.rsqrt(jnp.mean(x * x, axis=-1, keepdims=True) + NORM_EPS) * g_ref[...]
    z = _dot(hn.astype(BF16), w_ref[...])
    cos = cos_ref[...]
    sin = sin_ref[...]
    off = 0
    for (name, width, dtype), o_ref in zip(_INPROJ_OUT, out_refs):
        if name in ("qpad", "ksl", "kwn"):
            for j in range(width // LANES):
                blk = _rope_lanes(z[:, off + j * LANES: off + (j + 1) * LANES], cos, sin)
                if name == "qpad":
                    blk = blk * (HEAD_DIM ** -0.5)
                o_ref[:, j * LANES:(j + 1) * LANES] = blk.astype(dtype)
        elif name == "sg":
            n = ws_ref.shape[1]
            ri = lax.broadcasted_iota(jnp.int32, (n, n), 0)
            ci = lax.broadcasted_iota(jnp.int32, (n, n), 1)
            lane_g = lax.broadcasted_iota(jnp.int32, (1, SG_W), 1) // SG_CH
            wts = [jnp.where(ci <= ri, ws_ref[g], 0.0).astype(BF16) for g in range(SG_GROUPS)]
            for c in range(x.shape[0] // n):
                rs = slice(c * n, (c + 1) * n)
                o_ref[rs, :] = _spatial_gate(z[rs, off:off + 2 * SG_W], lg_ref[...], lb_ref[...], wts,
                                             sb_ref[...], lane_g)
        else:
            o_ref[...] = z[:, off:off + width].astype(dtype)
        off += 2 * SG_W if name == "sg" else width


def _inproj(x2, gain, w_p, cos_t, sin_t, sg_params, seq, tm):
    T, D = x2.shape
    nper = seq // tm
    assert tm % SG_CHUNK == 0
    out_shape = [jax.ShapeDtypeStruct((T, w), dt) for _, w, dt in _INPROJ_OUT]
    out_specs = [pl.BlockSpec((tm, w), lambda i: (i, 0)) for _, w, _ in _INPROJ_OUT]
    full = lambda a: pl.BlockSpec(a.shape, lambda i: (0,) * a.ndim)
    return pl.pallas_call(
        _inproj_kernel,
        grid=(T // tm,),
        in_specs=[
            pl.BlockSpec((tm, D), lambda i: (i, 0)),
            pl.BlockSpec((1, D), lambda i: (0, 0)),
            pl.BlockSpec((D, _INPROJ_W), lambda i: (0, 0), pipeline_mode=pl.Buffered(1)),
            pl.BlockSpec((tm, LANES), lambda i: (i % nper, 0)),
            pl.BlockSpec((tm, LANES), lambda i: (i % nper, 0)),
        ] + [full(a) for a in sg_params],
        out_specs=out_specs,
        out_shape=out_shape,
        compiler_params=_cparams(("parallel",)),
        name="inproj",
    )(x2, gain, w_p, cos_t, sin_t, *sg_params)


def _prep_w_in(w_in):
    offs = np.cumsum((0,) + IN_SIZES)
    seg = [w_in[:, offs[k]:offs[k + 1]] for k in range(len(IN_SIZES))]
    (wq, wkc, wvc, wksl, wvsl, wkwn, wvwn, wg, wgq, wgk, wgv, wga, wgr, wuv) = seg
    D = w_in.shape[0]
    zeros64 = jnp.zeros((D, HEAD_DIM), w_in.dtype)
    qcols = []
    for h in range(NSA_HEADS):
        wh = wq[:, h * HEAD_DIM:(h + 1) * HEAD_DIM]
        qcols += [wh, zeros64] if h // NSA_HPG == 0 else [zeros64, wh]
    gcols = [jnp.pad(wg, ((0, 0), (0, LANES - wg.shape[1])))]
    wga_p = jnp.pad(jnp.concatenate([wga] * GLA_A_COPIES, axis=1), ((0, 0), (0, LANES - GLA_A_COPIES * GLA_RANK)))
    return jnp.concatenate([wuv] + qcols + [wkc, wvc, wksl, wvsl, wkwn, wvwn] + gcols
                           + [wgq, wgk, wgv, wga_p, wgr], axis=1).astype(BF16)


def _rope_tables(pos):
    half = HEAD_DIM // 2
    inv = 1.0 / (ROPE_THETA ** (jnp.arange(half, dtype=F32) / half))
    ang = pos.astype(F32)[:, None] * inv[None, :]
    cos = jnp.cos(ang)
    sin = jnp.sin(ang)
    cos_t = jnp.concatenate([cos, cos, cos, cos], axis=1)
    sin_t = jnp.concatenate([-sin, sin, -sin, sin], axis=1)
    return cos_t, sin_t


def _cmp_kernel(xk_ref, xv_ref, wek_ref, wev_ref, w1k_ref, w1v_ref, pk_ref, pv_ref,
                w2k_ref, w2v_ref, cos_ref, sin_ref, ok_ref, ov_ref):
    nrow = xk_ref.shape[1]
    row = lax.broadcasted_iota(jnp.int32, (nrow, LANES), 0)

    def compress(x_ref, we_ref, w1_ref, p_ref, w2_ref):
        h = _dot(x_ref[0].astype(BF16), we_ref[...])
        posb = _dot(p_ref[...].astype(BF16), w1_ref[...])[0:1, :]
        y = jnp.zeros((nrow, LANES), F32)
        for g in range(NSA_KV_GROUPS):
            a = h[:, g * CMP_HIDDEN:(g + 1) * CMP_HIDDEN]
            b = h[:, (NSA_KV_GROUPS + g) * CMP_HIDDEN:(NSA_KV_GROUPS + g + 1) * CMP_HIDDEN]
            hid = a + pltpu.roll(b, nrow - 1, 0) + posb
            y = y + _dot(_gelu_tanh(hid).astype(BF16), w2_ref[g])
        return y

    yk = _rope_lanes(compress(xk_ref, wek_ref, w1k_ref, pk_ref, w2k_ref), cos_ref[...], sin_ref[...])
    yv = compress(xv_ref, wev_ref, w1v_ref, pv_ref, w2v_ref)
    keep = row < nrow - 1
    ok_ref[0] = jnp.where(keep, yk, 0.0)
    ov_ref[0] = jnp.where(keep, yv, 0.0)


def _prep_cmp_weights(w1, w2, pos):
    half = CMP_LEN // CMP_STRIDE
    H = w1.shape[1]
    w1r = w1.reshape(half, CMP_STRIDE, HEAD_DIM, H)
    cols = []
    for a in range(half):
        for g in range(NSA_KV_GROUPS):
            blk = jnp.zeros((CMP_STRIDE, NSA_KV_GROUPS, HEAD_DIM, H), w1.dtype)
            blk = blk.at[:, g].set(w1r[a])
            cols.append(blk.reshape(CMP_STRIDE * KV_W, H))
    wexp = jnp.concatenate(cols, axis=1).astype(BF16)
    w2p = jnp.stack([jnp.pad(w2, ((0, 0), (g * HEAD_DIM, KV_W - (g + 1) * HEAD_DIM)))
                     for g in range(NSA_KV_GROUPS)]).astype(BF16)
    posf = jnp.pad(pos.reshape(1, CMP_LEN * HEAD_DIM), ((0, 7), (0, 0)))
    return wexp, w1.astype(BF16), posf, w2p


def _compress(kc, vc, wk, wv, cos_c, sin_c, batch, seq):
    nrow = seq // CMP_STRIDE
    xk = kc.reshape(batch, nrow, CMP_STRIDE * KV_W)
    xv = vc.reshape(batch, nrow, CMP_STRIDE * KV_W)
    wek, w1k, pk, w2k = wk
    wev, w1v, pv, w2v = wv
    full = lambda a: pl.BlockSpec(a.shape, lambda b: (0,) * a.ndim)
    xspec = pl.BlockSpec((1, nrow, CMP_STRIDE * KV_W), lambda b: (b, 0, 0))
    ospec = pl.BlockSpec((1, nrow, KV_W), lambda b: (b, 0, 0))
    return pl.pallas_call(
        _cmp_kernel,
        grid=(batch,),
        in_specs=[xspec, xspec, full(wek), full(wev), full(w1k), full(w1v), full(pk), full(pv),
                  full(w2k), full(w2v), full(cos_c), full(sin_c)],
        out_specs=[ospec, ospec],
        out_shape=[jax.ShapeDtypeStruct((batch, nrow, KV_W), F32)] * 2,
        compiler_params=_cparams(("parallel",)),
        name="compress",
    )(xk, xv, wek, wev, w1k, w1v, pk, pv, w2k, w2v, cos_c, sin_c)


def _unpad_heads(acc):
    qb = SEL_BLOCK
    lane = lax.broadcasted_iota(jnp.int32, (qb, LANES), 1)
    low = lane < HEAD_DIM
    outs = []
    for j in range(NSA_HEADS // 2):
        a = acc[(2 * j) * qb:(2 * j + 1) * qb]
        b = acc[(2 * j + 1) * qb:(2 * j + 2) * qb]
        if (2 * j) // NSA_HPG == 0:
            outs.append(jnp.where(low, a, pltpu.roll(b, HEAD_DIM, 1)))
        else:
            outs.append(jnp.where(low, pltpu.roll(a, HEAD_DIM, 1), b))
    return jnp.concatenate(outs, axis=1)


def _nsa_kernel(q_ref, g_ref, kc_ref, vc_ref, ksl_ref, vsl_ref, kwn_ref, vwn_ref,
                ktab_ref, kdiag_ref, kwtab_ref, qc_ref, qw_ref, eye_ref, gexp_ref, impt_ref, o_ref,
                imp_scr, m_scr, acc_scr, win_scr, kd_scr, qs_scr, sc_scr, cmp_scr):
    qb = SEL_BLOCK
    rows = NSA_HEADS * qb
    pair = range(q_ref.shape[0])
    i = pl.program_id(1)
    t = i * qb + lax.broadcasted_iota(jnp.int32, (rows, 1), 0) % qb
    ones = jnp.ones((SEL_CH, LANES), BF16)
    parts = [slice(p * rows // NSA_ROW_PARTS, (p + 1) * rows // NSA_ROW_PARTS) for p in range(NSA_ROW_PARTS)]
    qbf = [jnp.concatenate([q_ref[e, :, h * LANES:(h + 1) * LANES] for h in range(NSA_HEADS)], axis=0)
           for e in pair]

    nblk_w = WIN_KEYS // qb
    ib = jnp.minimum(i, nblk_w - 1)
    w0 = pl.multiple_of((i - ib) * qb, qb)
    toff = pl.multiple_of((nblk_w - 1 - ib) * qb, qb)
    kwmask = kwtab_ref[pl.ds(toff, WIN_KEYS), :]
    for e in pair:
        kw = jnp.concatenate([kwn_ref[e, 0, pl.ds(w0, WIN_KEYS), :], kwmask], axis=1)
        vw = jnp.concatenate([vwn_ref[e, 0, pl.ds(w0, WIN_KEYS), :], ones[0:WIN_KEYS]], axis=1)
        q_win = jnp.concatenate([qbf[e], qw_ref[...]], axis=1)
        for rs in parts:
            sw = _dot_nt(q_win[rs], kw)
            ew = jnp.exp(sw - jnp.max(sw, axis=1, keepdims=True)).astype(BF16)
            ow = _dot(ew, vw)
            win_scr[e, rs, :] = ow[:, 0:LANES] / ow[:, LANES:2 * LANES]

    ncmp = kc_ref.shape[2]
    cend = lax.broadcasted_iota(jnp.int32, (1, ncmp), 1) * CMP_STRIDE + (CMP_LEN - 1)
    vis = cend <= t
    imap = impt_ref[...].astype(BF16)
    imap2 = jnp.concatenate([imap, imap], axis=1)
    for e in pair:
        s = _dot_nt(qbf[e], kc_ref[e, 0].astype(BF16))
        sm = jnp.where(vis, s, NEG_BIG)
        mx = jnp.max(sm, axis=1, keepdims=True)
        ex = jnp.where(vis, jnp.exp(sm - mx), 0.0)
        d = jnp.sum(ex, axis=1, keepdims=True)
        p = ex / jnp.where(d > 0, d, 1.0)
        cmp_scr[e] = _dot(p.astype(BF16), vc_ref[e, 0].astype(BF16))
        psum = jnp.concatenate(
            [sum(p[(g * NSA_HPG + h) * qb:(g * NSA_HPG + h + 1) * qb] for h in range(NSA_HPG))
             for g in range(NSA_KV_GROUPS)], axis=0)
        p_hi = psum.astype(BF16)
        p_lo = (psum - p_hi.astype(F32)).astype(BF16)
        imp_scr[e] = jnp.abs(_dot_nt(imap2, jnp.concatenate([p_hi, p_lo], axis=1)))

    nsbp = imp_scr.shape[1]
    keys = [pltpu.bitcast(imp_scr[e], jnp.int32) for e in pair]
    n_id = lax.broadcasted_iota(jnp.int32, (nsbp, 1), 0)
    n_full = lax.broadcasted_iota(jnp.int32, (nsbp, LANES), 0)
    valid = n_id <= i
    forced = (n_id == 0) | (n_id == i) | (n_id == i - 1)
    nforced = 1 + (i >= 1).astype(jnp.int32) + (i >= 2).astype(jnp.int32)

    def rank_body(j, cnts):
        cnts = list(cnts)
        for u in range(RANK_UNROLL):
            m = 1 + RANK_UNROLL * j + u
            tadj = jnp.right_shift(n_full - 1 - m, 31)
            for e in pair:
                rowk = pltpu.bitcast(imp_scr[e, pl.ds(jnp.minimum(m, nsbp - 1), 1), :], jnp.int32)
                rowk = jnp.where(m <= i - 2, rowk, -1)
                cnts[e] = cnts[e] + jnp.right_shift(rowk - keys[e] + tadj, 31)
        return tuple(cnts)

    ntrip = (jnp.maximum(i - 2, 0) + RANK_UNROLL - 1) // RANK_UNROLL
    cnts = lax.fori_loop(0, ntrip, rank_body, tuple(jnp.zeros(keys[0].shape, jnp.int32) for _ in pair))
    budget = SEL_TOPK - nforced - ntrip * RANK_UNROLL

    for e in pair:
        sel = valid & (forced | (cnts[e] < budget))
        notsel_t = jnp.where(sel, 0.0, 1.0)
        notsel = _dot_tn(notsel_t, eye_ref[...])
        notsel = jnp.concatenate([notsel[g * qb:(g + 1) * qb]
                                  for g in range(NSA_KV_GROUPS) for _ in range(NSA_HPG)], axis=0)
        qs_scr[e] = jnp.concatenate([qbf[e], (notsel + qc_ref[...]).astype(BF16)], axis=1)
    m_scr[...] = jnp.full(m_scr.shape, NEG_BIG, F32)
    acc_scr[...] = jnp.zeros(acc_scr.shape, F32)
    half = SEL_CH // 2
    nhalf = i // (half // qb) + 1
    lead = nhalf % 2
    nfull = nhalf // 2
    base = lead * half
    k0_last = pl.multiple_of(jnp.maximum(base + (nfull - 1) * SEL_CH, 0), half)
    off = pl.multiple_of(i * qb - k0_last, qb)
    kd_scr[...] = ktab_ref[pl.ds(k0_last, SEL_CH), :]
    kd_scr[pl.ds(off, qb), :] = kd_scr[pl.ds(off, qb), :] + kdiag_ref[...]

    def scores(slot, k0, size, kmask):
        for e in pair:
            k = jnp.concatenate([ksl_ref[e, 0, pl.ds(k0, size), :], kmask], axis=1)
            for rs in parts:
                sc = _dot_nt(qs_scr[e, rs, :], k)
                sc_scr[e, slot, rs, 0:size] = sc
                mx = m_scr[e, rs, :]
                for j in range(size // LANES):
                    mx = jnp.maximum(mx, sc[:, j * LANES:(j + 1) * LANES])
                m_scr[e, rs, :] = mx

    def values(slot, k0, size):
        for e in pair:
            v = jnp.concatenate([vsl_ref[e, 0, pl.ds(k0, size), :], ones[0:size]], axis=1)
            for rs in parts:
                mb = m_scr[e, rs, :]
                pr = jnp.concatenate([jnp.exp(sc_scr[e, slot, rs, j * LANES:(j + 1) * LANES] - mb)
                                      for j in range(size // LANES)], axis=1).astype(BF16)
                acc_scr[e, rs, :] += _dot(pr, v)

    @pl.when(lead == 1)
    def _():
        scores(0, 0, half, jnp.where(nfull == 0, kd_scr[0:half, :], ktab_ref[0:half, :]))

    def score_body(c, carry):
        k0 = pl.multiple_of(base + c * SEL_CH, half)
        scores(1 + c, k0, SEL_CH, jnp.where(c == nfull - 1, kd_scr[...], ktab_ref[pl.ds(k0, SEL_CH), :]))
        return carry

    lax.fori_loop(0, nfull, score_body, 0)
    for e in pair:
        m_scr[e] = jnp.broadcast_to(jnp.max(m_scr[e], axis=1, keepdims=True), (rows, LANES))

    @pl.when(lead == 1)
    def _():
        values(0, 0, half)

    def value_body(c, carry):
        values(1 + c, pl.multiple_of(base + c * SEL_CH, half), SEL_CH)
        return carry

    lax.fori_loop(0, nfull, value_body, 0)

    for e in pair:
        o_slc = acc_scr[e, :, 0:LANES] / acc_scr[e, :, LANES:2 * LANES]
        g_hi = jax.nn.sigmoid(g_ref[e])
        g_lo = g_hi - g_hi.astype(BF16).astype(F32)
        gate = _dot(jnp.concatenate([g_hi.astype(BF16), g_lo.astype(BF16)], axis=1), gexp_ref[...])
        o_ref[e] = (gate[:, 0:NSA_W] * _unpad_heads(cmp_scr[e])
                    + gate[:, NSA_W:2 * NSA_W] * _unpad_heads(o_slc)
                    + gate[:, 2 * NSA_W:3 * NSA_W] * _unpad_heads(win_scr[e]))


def _nsa(qpad, gts, kcmp, vcmp, ksl, vsl, kwn, vwn, consts, batch, seq):
    qb = SEL_BLOCK
    nq = seq // qb
    rows = NSA_HEADS * qb
    npair = NSA_PAIR if batch % NSA_PAIR == 0 else 1
    nbh = batch // npair
    tok = lambda a: a.reshape(npair, nbh * seq, a.shape[-1])
    per = lambda a: a.reshape(npair, nbh, a.shape[-2], KV_W)
    tok_spec = lambda w: pl.BlockSpec((npair, qb, w), lambda bh, i: (0, bh * nq + i, 0))
    per_spec = lambda n: pl.BlockSpec((npair, 1, n, KV_W), lambda bh, i: (0, bh, 0, 0))
    full = lambda a: pl.BlockSpec(a.shape, lambda bh, i: (0,) * a.ndim)
    out = pl.pallas_call(
        _nsa_kernel,
        grid=(nbh, nq),
        in_specs=[
            tok_spec(QPAD_W), tok_spec(LANES),
            per_spec(kcmp.shape[1]), per_spec(vcmp.shape[1]),
            per_spec(seq), per_spec(seq), per_spec(seq), per_spec(seq),
        ] + [full(c) for c in consts],
        out_specs=tok_spec(NSA_W),
        out_shape=jax.ShapeDtypeStruct((npair, nbh * seq, NSA_W), F32),
        scratch_shapes=[
            pltpu.VMEM((npair, consts[-1].shape[0], NSA_KV_GROUPS * qb), F32),
            pltpu.VMEM((npair, rows, LANES), F32),
            pltpu.VMEM((npair, rows, 2 * LANES), F32),
            pltpu.VMEM((npair, rows, LANES), F32),
            pltpu.VMEM((SEL_CH, LANES), BF16),
            pltpu.VMEM((npair, rows, 2 * LANES), BF16),
            pltpu.VMEM((npair, seq // SEL_CH + 1, rows, SEL_CH), F32),
            pltpu.VMEM((npair, rows, LANES), F32),
        ],
        compiler_params=_cparams(("parallel", "arbitrary")),
        name="nsa",
    )(tok(qpad), tok(gts), per(kcmp), per(vcmp), per(ksl.reshape(batch, seq, KV_W)),
      per(vsl.reshape(batch, seq, KV_W)), per(kwn.reshape(batch, seq, KV_W)),
      per(vwn.reshape(batch, seq, KV_W)), *consts)
    return out.reshape(batch * seq, NSA_W)


def _nsa_constants(seq):
    qb = SEL_BLOCK
    nc_pad = seq // CMP_STRIDE
    nsb = seq // qb
    assert nsb <= qb
    nsb_pad = max(nsb, 8)
    cs = np.arange(nc_pad) * CMP_STRIDE
    ce = cs + CMP_LEN
    bs = np.arange(nsb_pad) * qb
    be = bs + qb
    ov = np.clip(np.minimum(ce[None, :], be[:, None]) - np.maximum(cs[None, :], bs[:, None]), 0, None)
    impt = ov / CMP_LEN
    pos = np.arange(seq)
    lane = np.arange(LANES)[None, :]
    ktab = np.where(lane == pos[:, None] // qb, NEG_BIG, 0.0)
    kdiag = np.where(lane == qb + np.arange(qb)[:, None], NEG_BIG, 0.0)
    last = WIN_KEYS // qb - 1
    x = np.arange(WIN_KEYS + last * qb)[:, None]
    kb, j = x // qb, x % qb
    kwtab = np.where((((kb == 0) | (kb > last)) & (lane == 0)) | ((kb == 1) & (lane == j))
                     | ((kb == last) & (lane == qb + j)), NEG_BIG, 0.0)
    r = (np.arange(NSA_HEADS * qb) % qb)[:, None]
    upper = (lane >= qb) & (lane - qb > r)
    qc = np.where(upper, 1.0, 0.0)
    qw = np.where(upper | ((lane < qb) & (lane <= r)), 1.0, 0.0)
    eye = np.eye(nsb_pad, LANES)
    col = np.arange(GEXP_W)[None, :]
    src = np.arange(LANES)[:, None]
    rep = (src == (col % NSA_W) // HEAD_DIM * 3 + col // NSA_W) & (src < 3 * NSA_HEADS)
    gexp = np.concatenate([rep, rep], axis=0)
    return (jnp.asarray(ktab, BF16), jnp.asarray(kdiag, BF16), jnp.asarray(kwtab, BF16),
            jnp.asarray(qc, F32), jnp.asarray(qw, BF16), jnp.asarray(eye, F32), jnp.asarray(gexp, BF16),
            jnp.asarray(impt, F32))


def _split3(x):
    hi = x.astype(BF16)
    r1 = x - hi.astype(F32)
    mid = r1.astype(BF16)
    lo = (r1 - mid.astype(F32)).astype(BF16)
    return hi, mid, lo


def _gla_kernel(q_ref, k_ref, v_ref, a_ref, r_ref, w6_ref, bup_ref, ng_ref, o_ref, st_ref):
    pair = range(q_ref.shape[0])
    blk = q_ref.shape[1]
    nsub = blk // GLA_SUB

    @pl.when(pl.program_id(1) == 0)
    def _():
        st_ref[...] = jnp.zeros(st_ref.shape, F32)

    ri = lax.broadcasted_iota(jnp.int32, (blk, blk), 0)
    ci = lax.broadcasted_iota(jnp.int32, (blk, blk), 1)
    ltri = jnp.where(ci <= ri, 1.0, 0.0).astype(BF16)
    ltri3 = jnp.concatenate([ltri, ltri, ltri], axis=1)
    lane_rep = lax.broadcasted_iota(jnp.int32, (1, LANES), 1) // GLA_RANK
    srow = lax.broadcasted_iota(jnp.int32, st_ref.shape[1:], 0) // GLA_DV
    scol = lax.broadcasted_iota(jnp.int32, st_ref.shape[1:], 1) // GLA_DK
    krow = lax.broadcasted_iota(jnp.int32, (blk, 1), 0)
    lane_qk = lax.broadcasted_iota(jnp.int32, (1, GLA_QK_W), 1) // GLA_DK
    lane_v = lax.broadcasted_iota(jnp.int32, (1, GLA_W), 1) // GLA_DV
    qrow = lax.broadcasted_iota(jnp.int32, (GLA_HEADS * GLA_SUB, 1), 0) % GLA_SUB
    kcol = lax.broadcasted_iota(jnp.int32, (1, blk), 1)
    gi = lax.broadcasted_iota(jnp.int32, (GLA_W, GLA_W), 0) // GLA_DV
    gj = lax.broadcasted_iota(jnp.int32, (GLA_W, GLA_W), 1) // GLA_DV
    gmean = jnp.where(gi == gj, 1.0 / GLA_DV, 0.0).astype(BF16)
    gmean2 = jnp.concatenate([gmean, gmean], axis=0)

    for e in pair:
        a_hi, a_mid, a_lo = _split3(a_ref[e])
        a6 = jnp.where(lane_rep < 3, a_hi, jnp.where(lane_rep < 5, a_mid, a_lo))
        x = _dot(a6, w6_ref[...]) + bup_ref[...]
        g = (jnp.minimum(x, 0.0) - jnp.log1p(jnp.exp(-jnp.abs(x)))) / GLA_TAU
        b = _dot(ltri3, jnp.concatenate(_split3(g), axis=0))
        blast = b[blk - 1:blk, :]
        q = q_ref[e] * (GLA_DK ** -0.5)
        k = k_ref[e]
        v = v_ref[e]
        vb = v.astype(BF16)

        st = st_ref[e]
        o = _dot_nt((q * jnp.exp(b)).astype(BF16), st.astype(BF16))
        khat = k * jnp.exp(blast - b)
        st_ref[e] = st * jnp.exp(blast) + jnp.where(srow == scol, _dot_tn(v, khat), 0.0)

        for c in range(nsub):
            lo = c * GLA_SUB
            ref_b = b[lo:lo + 1, :]
            qt = q[lo:lo + GLA_SUB] * jnp.exp(b[lo:lo + GLA_SUB] - ref_b)
            kt = k * jnp.exp(jnp.where(krow < lo + GLA_SUB, ref_b - b, 0.0))
            qs = jnp.concatenate([jnp.where(lane_qk == h, qt, 0.0) for h in range(GLA_HEADS)], axis=0)
            a = _dot_nt(qs.astype(BF16), kt.astype(BF16))
            a = jnp.where(kcol <= lo + qrow, a, 0.0)
            r = _dot(a.astype(BF16), vb)
            oi = sum(jnp.where(lane_v == h, r[h * GLA_SUB:(h + 1) * GLA_SUB], 0.0) for h in range(GLA_HEADS))
            o_ref[e, lo:lo + GLA_SUB, :] = o[lo:lo + GLA_SUB] + oi

    for e in pair:
        o = o_ref[e]
        oo = o * o
        oo_hi = oo.astype(BF16)
        oo_lo = (oo - oo_hi.astype(F32)).astype(BF16)
        ms = _dot(jnp.concatenate([oo_hi, oo_lo], axis=1), gmean2)
        rr = r_ref[e]
        o_ref[e] = o * lax.rsqrt(ms + NORM_EPS) * ng_ref[...] * (rr * jax.nn.sigmoid(rr))


def _prep_gla_wup(w_up):
    hi = w_up.astype(BF16)
    r1 = w_up - hi.astype(F32)
    mid = r1.astype(BF16)
    lo = (r1 - mid.astype(F32)).astype(BF16)
    w6 = jnp.concatenate([hi, mid, lo, hi, mid, hi], axis=0)
    return jnp.pad(w6, ((0, LANES - w6.shape[0]), (0, 0)))


def _gla(gq, gk, gv, ga, gr, w6, bup, ng, batch, seq):
    blk = min(GLA_BLK, seq)
    nb = seq // blk
    npair = GLA_PAIR if batch % GLA_PAIR == 0 else 1
    nbh = batch // npair
    tok = lambda a: a.reshape(npair, nbh * seq, a.shape[-1])
    spec = lambda w: pl.BlockSpec((npair, blk, w), lambda bh, j: (0, bh * nb + j, 0))
    full = lambda a: pl.BlockSpec(a.shape, lambda bh, j: (0,) * a.ndim)
    out = pl.pallas_call(
        _gla_kernel,
        grid=(nbh, nb),
        in_specs=[spec(GLA_QK_W), spec(GLA_QK_W), spec(GLA_W), spec(LANES), spec(GLA_W),
                  full(w6), full(bup), full(ng)],
        out_specs=spec(GLA_W),
        out_shape=jax.ShapeDtypeStruct((npair, nbh * seq, GLA_W), F32),
        scratch_shapes=[pltpu.VMEM((npair, GLA_W, GLA_QK_W), F32)],
        compiler_params=_cparams(("parallel", "arbitrary")),
        name="gla",
    )(tok(gq), tok(gk), tok(gv), tok(ga), tok(gr), w6, bup, ng)
    return out.reshape(batch * seq, GLA_W)


def _outffn_kernel(x_ref, on_ref, og_ref, os_ref, wo_ref, fg_ref, wg_ref, wu_ref, wd_ref,
                   fin_ref, o_ref, *, final):
    x = x_ref[...]
    x = x + _dot(on_ref[...].astype(BF16), wo_ref[0:NSA_W, :])
    x = x + _dot(og_ref[...].astype(BF16), wo_ref[NSA_W:NSA_W + GLA_W, :])
    x = x + _dot(os_ref[...].astype(BF16), wo_ref[NSA_W + GLA_W:, :])
    hn = (x * lax.rsqrt(jnp.mean(x * x, axis=-1, keepdims=True) + NORM_EPS) * fg_ref[...]).astype(BF16)
    gt = _dot(hn, wg_ref[...])
    up = _dot(hn, wu_ref[...])
    y = x + _dot((gt * jax.nn.sigmoid(gt) * up).astype(BF16), wd_ref[...])
    if final:
        y = y * lax.rsqrt(jnp.mean(y * y, axis=-1, keepdims=True) + NORM_EPS) * fin_ref[...]
    o_ref[...] = y


def _outffn(x2, o_nsa, o_gla, o_sg, wo, fgain, wg, wu, wd, fin, tm, final):
    T, D = x2.shape
    row = lambda w: pl.BlockSpec((tm, w), lambda i: (i, 0))
    const = lambda a: pl.BlockSpec(a.shape, lambda i: (0,) * a.ndim, pipeline_mode=pl.Buffered(1))
    return pl.pallas_call(
        functools.partial(_outffn_kernel, final=final),
        grid=(T // tm,),
        in_specs=[row(D), row(NSA_W), row(GLA_W), row(SG_W), const(wo), const(fgain),
                  const(wg), const(wu), const(wd), const(fin)],
        out_specs=row(D),
        out_shape=jax.ShapeDtypeStruct((T, D), F32),
        compiler_params=_cparams(("parallel",)),
        name="outproj_ffn",
    )(x2, o_nsa, o_gla, o_sg, wo, fgain, wg, wu, wd, fin)


def kernel(x, attn_norm, w_in, cmp_pos_k, cmp_w1_k, cmp_w2_k, cmp_pos_v, cmp_w1_v, cmp_w2_v,
           gla_w_up, gla_b_up, gla_norm, sg_ln_g, sg_ln_b, sg_w, sg_b, w_out,
           ffn_norm, w_gate, w_up, w_down, final_norm):
    batch, seq, d_model = x.shape
    depth = w_in.shape[0]
    T = batch * seq
    assert seq % GLA_BLK == 0 and seq >= WIN_KEYS and seq % SEL_CH == 0
    tm_in = tm_ffn = min(ROW_TILE, seq)

    cos_t, sin_t = _rope_tables(jnp.arange(seq))
    nrow = seq // CMP_STRIDE
    cos_c, sin_c = _rope_tables(jnp.arange(nrow) * CMP_STRIDE + CMP_LEN - 1)
    nsa_consts = _nsa_constants(seq)

    x2 = x.reshape(T, d_model)
    for l in range(depth):
        sg_params = (sg_ln_g[l][None, :], sg_ln_b[l][None, :], sg_w[l],
                     jnp.repeat(sg_b[l].T, SG_CH, axis=1))
        (o_sg, qpad, kc, vc, ksl, vsl, kwn, vwn, gts, gq, gk, gv, ga, gr) = _inproj(
            x2, attn_norm[l][None, :], _prep_w_in(w_in[l]), cos_t, sin_t, sg_params, seq, tm_in)
        kcmp, vcmp = _compress(
            kc, vc, _prep_cmp_weights(cmp_w1_k[l], cmp_w2_k[l], cmp_pos_k[l]),
            _prep_cmp_weights(cmp_w1_v[l], cmp_w2_v[l], cmp_pos_v[l]), cos_c, sin_c, batch, seq)
        o_nsa = _nsa(qpad, gts, kcmp, vcmp, ksl, vsl, kwn, vwn, nsa_consts, batch, seq)
        o_gla = _gla(gq, gk, gv, ga, gr, _prep_gla_wup(gla_w_up[l]), gla_b_up[l][None, :],
                     jnp.tile(gla_norm[l], GLA_HEADS)[None, :], batch, seq)
        x2 = _outffn(x2, o_nsa, o_gla, o_sg, w_out[l].astype(BF16), ffn_norm[l][None, :],
                     w_gate[l].astype(BF16), w_up[l].astype(BF16), w_down[l].astype(BF16),
                     final_norm[None, :], tm_ffn, final=(l == depth - 1))
    return x2.reshape(batch, seq, d_model)
```

```python
import functools
import math

import numpy as np
import jax
import jax.numpy as jnp
from jax import lax
from jax.experimental import pallas as pl
from jax.experimental.pallas import tpu as pltpu

HEAD_DIM = 64
NSA_HEADS = 8
NSA_KV_GROUPS = 2
NSA_HPG = NSA_HEADS // NSA_KV_GROUPS
CMP_LEN = 32
CMP_STRIDE = 16
CMP_HIDDEN = 256
SEL_BLOCK = 64
SEL_TOPK = 16
WINDOW = 512
GLA_HEADS = 4
GLA_DK = 32
GLA_DV = 64
GLA_RANK = 16
GLA_TAU = 16.0
SG_GROUPS = 4
SG_CH = 64
SG_CHUNK = 128
NSA_W = NSA_HEADS * HEAD_DIM
GLA_W = GLA_HEADS * GLA_DV
SG_W = SG_GROUPS * SG_CH
KV_W = NSA_KV_GROUPS * HEAD_DIM
IN_SIZES = (NSA_W, KV_W, KV_W, KV_W, KV_W, KV_W, KV_W, NSA_HEADS * 3,
            GLA_HEADS * GLA_DK, GLA_HEADS * GLA_DK, GLA_W, GLA_RANK, GLA_W, 2 * SG_W)
ROPE_THETA = 10000.0
NORM_EPS = 1e-6

LANES = 128
NEG_BIG = -1e30
VMEM_LIMIT = 56 * 1024 * 1024

QPAD_W = NSA_HEADS * LANES
GEXP_W = 3 * NSA_W
GLA_QK_W = GLA_HEADS * GLA_DK
GLA_BLK = 256
GLA_SUB = 16
SEL_CH = 1024
WIN_KEYS = WINDOW + 2 * SEL_BLOCK
NSA_ROW_PARTS = 1
NSA_STEP_BLOCKS = 4
RANK_UNROLL = 4
NSA_PAIR = 2
GLA_PAIR = 4

ROW_TILE = 512
GLA_A_COPIES = 6
F32 = jnp.float32
BF16 = jnp.bfloat16


def _cparams(sem):
    return pltpu.CompilerParams(dimension_semantics=sem, vmem_limit_bytes=VMEM_LIMIT)


def _dot(a, b, precision=None):
    return jnp.dot(a, b, preferred_element_type=F32, precision=precision)


def _dot_nt(a, b, precision=None):
    return lax.dot_general(a, b, (((1,), (1,)), ((), ())),
                           preferred_element_type=F32, precision=precision)


def _dot_tn(a, b, precision=None):
    return lax.dot_general(a, b, (((0,), (0,)), ((), ())),
                           preferred_element_type=F32, precision=precision)


def _gelu_tanh(x):
    c = math.sqrt(2.0 / math.pi)
    return 0.5 * x * (1.0 + jnp.tanh(c * (x + 0.044715 * (x * x * x))))


def _rope_lanes(x, cos, sin_signed):
    n = x.shape[-1]
    lane = lax.broadcasted_iota(jnp.int32, x.shape, 1)
    first_half = (lane % HEAD_DIM) < (HEAD_DIM // 2)
    partner = jnp.where(first_half,
                        pltpu.roll(x, n - HEAD_DIM // 2, 1),
                        pltpu.roll(x, HEAD_DIM // 2, 1))
    return x * cos + partner * sin_signed


_INPROJ_OUT = (
    ("sg", SG_W, F32),
    ("qpad", QPAD_W, BF16), ("kc", KV_W, F32), ("vc", KV_W, F32),
    ("ksl", KV_W, BF16), ("vsl", KV_W, BF16), ("kwn", KV_W, BF16), ("vwn", KV_W, BF16),
    ("gts", LANES, F32), ("gq", GLA_QK_W, F32), ("gk", GLA_QK_W, F32),
    ("gv", GLA_W, F32), ("ga", LANES, F32), ("gr", GLA_W, F32))
_INPROJ_W = sum(w for _, w, _ in _INPROJ_OUT) + SG_W


def _spatial_gate(uv, lg, lb, wts, bias, lane_g):
    a = _gelu_tanh(uv)
    u = a[:, :SG_W]
    v = a[:, SG_W:]
    mu = jnp.mean(v, axis=-1, keepdims=True)
    var = jnp.mean(jnp.square(v - mu), axis=-1, keepdims=True)
    vn = ((v - mu) * lax.rsqrt(var + NORM_EPS) * lg + lb).astype(BF16)
    s = bias
    for g in range(SG_GROUPS):
        s = s + jnp.where(lane_g == g, _dot(wts[g], vn), 0.0)
    return u * s


def _inproj_kernel(x_ref, g_ref, w_ref, cos_ref, sin_ref, lg_ref, lb_ref, ws_ref, sb_ref, *out_refs):
    x = x_ref[...]
    hn = x * lax.rsqrt(jnp.mean(x * x, axis=-1, keepdims=True) + NORM_EPS) * g_ref[...]
    z = _dot(hn.astype(BF16), w_ref[...])
    cos = cos_ref[...]
    sin = sin_ref[...]
    off = 0
    for (name, width, dtype), o_ref in zip(_INPROJ_OUT, out_refs):
        if name in ("qpad", "ksl", "kwn"):
            for j in range(width // LANES):
                blk = _rope_lanes(z[:, off + j * LANES: off + (j + 1) * LANES], cos, sin)
                if name == "qpad":
                    blk = blk * (HEAD_DIM ** -0.5)
                o_ref[:, j * LANES:(j + 1) * LANES] = blk.astype(dtype)
        elif name == "sg":
            n = ws_ref.shape[1]
            ri = lax.broadcasted_iota(jnp.int32, (n, n), 0)
            ci = lax.broadcasted_iota(jnp.int32, (n, n), 1)
            lane_g = lax.broadcasted_iota(jnp.int32, (1, SG_W), 1) // SG_CH
            wts = [jnp.where(ci <= ri, ws_ref[g], 0.0).astype(BF16) for g in range(SG_GROUPS)]
            for c in range(x.shape[0] // n):
                rs = slice(c * n, (c + 1) * n)
                o_ref[rs, :] = _spatial_gate(z[rs, off:off + 2 * SG_W], lg_ref[...], lb_ref[...], wts,
                                             sb_ref[...], lane_g)
        else:
            o_ref[...] = z[:, off:off + width].astype(dtype)
        off += 2 * SG_W if name == "sg" else width


def _inproj(x2, gain, w_p, cos_t, sin_t, sg_params, seq, tm):
    T, D = x2.shape
    nper = seq // tm
    assert tm % SG_CHUNK == 0
    out_shape = [jax.ShapeDtypeStruct((T, w), dt) for _, w, dt in _INPROJ_OUT]
    out_specs = [pl.BlockSpec((tm, w), lambda i: (i, 0)) for _, w, _ in _INPROJ_OUT]
    full = lambda a: pl.BlockSpec(a.shape, lambda i: (0,) * a.ndim)
    return pl.pallas_call(
        _inproj_kernel,
        grid=(T // tm,),
        in_specs=[
            pl.BlockSpec((tm, D), lambda i: (i, 0)),
            pl.BlockSpec((1, D), lambda i: (0, 0)),
            pl.BlockSpec((D, _INPROJ_W), lambda i: (0, 0), pipeline_mode=pl.Buffered(1)),
            pl.BlockSpec((tm, LANES), lambda i: (i % nper, 0)),
            pl.BlockSpec((tm, LANES), lambda i: (i % nper, 0)),
        ] + [full(a) for a in sg_params],
        out_specs=out_specs,
        out_shape=out_shape,
        compiler_params=_cparams(("parallel",)),
        name="inproj",
    )(x2, gain, w_p, cos_t, sin_t, *sg_params)


def _prep_w_in(w_in):
    offs = np.cumsum((0,) + IN_SIZES)
    seg = [w_in[:, offs[k]:offs[k + 1]] for k in range(len(IN_SIZES))]
    (wq, wkc, wvc, wksl, wvsl, wkwn, wvwn, wg, wgq, wgk, wgv, wga, wgr, wuv) = seg
    D = w_in.shape[0]
    zeros64 = jnp.zeros((D, HEAD_DIM), w_in.dtype)
    qcols = []
    for h in range(NSA_HEADS):
        wh = wq[:, h * HEAD_DIM:(h + 1) * HEAD_DIM]
        qcols += [wh, zeros64] if h // NSA_HPG == 0 else [zeros64, wh]
    gcols = [jnp.pad(wg, ((0, 0), (0, LANES - wg.shape[1])))]
    wga_p = jnp.pad(jnp.concatenate([wga] * GLA_A_COPIES, axis=1), ((0, 0), (0, LANES - GLA_A_COPIES * GLA_RANK)))
    return jnp.concatenate([wuv] + qcols + [wkc, wvc, wksl, wvsl, wkwn, wvwn] + gcols
                           + [wgq, wgk, wgv, wga_p, wgr], axis=1).astype(BF16)


def _rope_tables(pos):
    half = HEAD_DIM // 2
    inv = 1.0 / (ROPE_THETA ** (jnp.arange(half, dtype=F32) / half))
    ang = pos.astype(F32)[:, None] * inv[None, :]
    cos = jnp.cos(ang)
    sin = jnp.sin(ang)
    cos_t = jnp.concatenate([cos, cos, cos, cos], axis=1)
    sin_t = jnp.concatenate([-sin, sin, -sin, sin], axis=1)
    return cos_t, sin_t


def _cmp_kernel(xk_ref, xv_ref, wek_ref, wev_ref, w1k_ref, w1v_ref, pk_ref, pv_ref,
                w2k_ref, w2v_ref, cos_ref, sin_ref, ok_ref, ov_ref):
    nrow = xk_ref.shape[1]
    row = lax.broadcasted_iota(jnp.int32, (nrow, LANES), 0)

    def compress(x_ref, we_ref, w1_ref, p_ref, w2_ref):
        h = _dot(x_ref[0].astype(BF16), we_ref[...])
        posb = _dot(p_ref[...].astype(BF16), w1_ref[...])[0:1, :]
        y = jnp.zeros((nrow, LANES), F32)
        for g in range(NSA_KV_GROUPS):
            a = h[:, g * CMP_HIDDEN:(g + 1) * CMP_HIDDEN]
            b = h[:, (NSA_KV_GROUPS + g) * CMP_HIDDEN:(NSA_KV_GROUPS + g + 1) * CMP_HIDDEN]
            hid = a + pltpu.roll(b, nrow - 1, 0) + posb
            y = y + _dot(_gelu_tanh(hid).astype(BF16), w2_ref[g])
        return y

    yk = _rope_lanes(compress(xk_ref, wek_ref, w1k_ref, pk_ref, w2k_ref), cos_ref[...], sin_ref[...])
    yv = compress(xv_ref, wev_ref, w1v_ref, pv_ref, w2v_ref)
    keep = row < nrow - 1
    ok_ref[0] = jnp.where(keep, yk, 0.0)
    ov_ref[0] = jnp.where(keep, yv, 0.0)


def _prep_cmp_weights(w1, w2, pos):
    half = CMP_LEN // CMP_STRIDE
    H = w1.shape[1]
    w1r = w1.reshape(half, CMP_STRIDE, HEAD_DIM, H)
    cols = []
    for a in range(half):
        for g in range(NSA_KV_GROUPS):
            blk = jnp.zeros((CMP_STRIDE, NSA_KV_GROUPS, HEAD_DIM, H), w1.dtype)
            blk = blk.at[:, g].set(w1r[a])
            cols.append(blk.reshape(CMP_STRIDE * KV_W, H))
    wexp = jnp.concatenate(cols, axis=1).astype(BF16)
    w2p = jnp.stack([jnp.pad(w2, ((0, 0), (g * HEAD_DIM, KV_W - (g + 1) * HEAD_DIM)))
                     for g in range(NSA_KV_GROUPS)]).astype(BF16)
    posf = jnp.pad(pos.reshape(1, CMP_LEN * HEAD_DIM), ((0, 7), (0, 0)))
    return wexp, w1.astype(BF16), posf, w2p


def _compress(kc, vc, wk, wv, cos_c, sin_c, batch, seq):
    nrow = seq // CMP_STRIDE
    xk = kc.reshape(batch, nrow, CMP_STRIDE * KV_W)
    xv = vc.reshape(batch, nrow, CMP_STRIDE * KV_W)
    wek, w1k, pk, w2k = wk
    wev, w1v, pv, w2v = wv
    full = lambda a: pl.BlockSpec(a.shape, lambda b: (0,) * a.ndim)
    xspec = pl.BlockSpec((1, nrow, CMP_STRIDE * KV_W), lambda b: (b, 0, 0))
    ospec = pl.BlockSpec((1, nrow, KV_W), lambda b: (b, 0, 0))
    return pl.pallas_call(
        _cmp_kernel,
        grid=(batch,),
        in_specs=[xspec, xspec, full(wek), full(wev), full(w1k), full(w1v), full(pk), full(pv),
                  full(w2k), full(w2v), full(cos_c), full(sin_c)],
        out_specs=[ospec, ospec],
        out_shape=[jax.ShapeDtypeStruct((batch, nrow, KV_W), F32)] * 2,
        compiler_params=_cparams(("parallel",)),
        name="compress",
    )(xk, xv, wek, wev, w1k, w1v, pk, pv, w2k, w2v, cos_c, sin_c)


def _unpad_heads(acc):
    qb = SEL_BLOCK
    lane = lax.broadcasted_iota(jnp.int32, (qb, LANES), 1)
    low = lane < HEAD_DIM
    outs = []
    for j in range(NSA_HEADS // 2):
        a = acc[(2 * j) * qb:(2 * j + 1) * qb]
        b = acc[(2 * j + 1) * qb:(2 * j + 2) * qb]
        if (2 * j) // NSA_HPG == 0:
            outs.append(jnp.where(low, a, pltpu.roll(b, HEAD_DIM, 1)))
        else:
            outs.append(jnp.where(low, pltpu.roll(a, HEAD_DIM, 1), b))
    return jnp.concatenate(outs, axis=1)


def _nsa_kernel(*refs):
    for j in range(NSA_STEP_BLOCKS):
        _nsa_block(j, *refs)


def _nsa_block(j, q_ref, g_ref, kc_ref, vc_ref, ksl_ref, vsl_ref, kwn_ref, vwn_ref,
               ktab_ref, kdiag_ref, kwtab_ref, qc_ref, qw_ref, eye_ref, gexp_ref, impt_ref, o_ref,
               imp_scr, m_scr, acc_scr, win_scr, kd_scr, qs_scr, sc_scr, cmp_scr):
    qb = SEL_BLOCK
    rows = NSA_HEADS * qb
    pair = range(q_ref.shape[0])
    i = pl.program_id(1) * NSA_STEP_BLOCKS + j
    tr = slice(j * qb, (j + 1) * qb)
    t = i * qb + lax.broadcasted_iota(jnp.int32, (rows, 1), 0) % qb
    ones = jnp.ones((SEL_CH, LANES), BF16)
    parts = [slice(p * rows // NSA_ROW_PARTS, (p + 1) * rows // NSA_ROW_PARTS) for p in range(NSA_ROW_PARTS)]
    qbf = [jnp.concatenate([q_ref[e, tr, h * LANES:(h + 1) * LANES] for h in range(NSA_HEADS)], axis=0)
           for e in pair]

    nblk_w = WIN_KEYS // qb
    ib = jnp.minimum(i, nblk_w - 1)
    w0 = pl.multiple_of((i - ib) * qb, qb)
    toff = pl.multiple_of((nblk_w - 1 - ib) * qb, qb)
    kwmask = kwtab_ref[pl.ds(toff, WIN_KEYS), :]
    for e in pair:
        kw = jnp.concatenate([kwn_ref[e, 0, pl.ds(w0, WIN_KEYS), :], kwmask], axis=1)
        vw = jnp.concatenate([vwn_ref[e, 0, pl.ds(w0, WIN_KEYS), :], ones[0:WIN_KEYS]], axis=1)
        q_win = jnp.concatenate([qbf[e], qw_ref[...]], axis=1)
        for rs in parts:
            sw = _dot_nt(q_win[rs], kw)
            ew = jnp.exp(sw - jnp.max(sw, axis=1, keepdims=True)).astype(BF16)
            ow = _dot(ew, vw)
            win_scr[e, rs, :] = ow[:, 0:LANES] / ow[:, LANES:2 * LANES]

    ncmp = kc_ref.shape[2]
    cend = lax.broadcasted_iota(jnp.int32, (1, ncmp), 1) * CMP_STRIDE + (CMP_LEN - 1)
    vis = cend <= t
    imap = impt_ref[...].astype(BF16)
    imap2 = jnp.concatenate([imap, imap], axis=1)
    for e in pair:
        s = _dot_nt(qbf[e], kc_ref[e, 0].astype(BF16))
        sm = jnp.where(vis, s, NEG_BIG)
        mx = jnp.max(sm, axis=1, keepdims=True)
        ex = jnp.where(vis, jnp.exp(sm - mx), 0.0)
        d = jnp.sum(ex, axis=1, keepdims=True)
        p = ex / jnp.where(d > 0, d, 1.0)
        cmp_scr[e] = _dot(p.astype(BF16), vc_ref[e, 0].astype(BF16))
        psum = jnp.concatenate(
            [sum(p[(g * NSA_HPG + h) * qb:(g * NSA_HPG + h + 1) * qb] for h in range(NSA_HPG))
             for g in range(NSA_KV_GROUPS)], axis=0)
        p_hi = psum.astype(BF16)
        p_lo = (psum - p_hi.astype(F32)).astype(BF16)
        imp_scr[e] = jnp.abs(_dot_nt(imap2, jnp.concatenate([p_hi, p_lo], axis=1)))

    nsbp = imp_scr.shape[1]
    keys = [pltpu.bitcast(imp_scr[e], jnp.int32) for e in pair]
    n_id = lax.broadcasted_iota(jnp.int32, (nsbp, 1), 0)
    n_full = lax.broadcasted_iota(jnp.int32, (nsbp, LANES), 0)
    valid = n_id <= i
    forced = (n_id == 0) | (n_id == i) | (n_id == i - 1)
    nforced = 1 + (i >= 1).astype(jnp.int32) + (i >= 2).astype(jnp.int32)

    def rank_body(j, cnts):
        cnts = list(cnts)
        for u in range(RANK_UNROLL):
            m = 1 + RANK_UNROLL * j + u
            tadj = jnp.right_shift(n_full - 1 - m, 31)
            for e in pair:
                rowk = pltpu.bitcast(imp_scr[e, pl.ds(jnp.minimum(m, nsbp - 1), 1), :], jnp.int32)
                rowk = jnp.where(m <= i - 2, rowk, -1)
                cnts[e] = cnts[e] + jnp.right_shift(rowk - keys[e] + tadj, 31)
        return tuple(cnts)

    ntrip = (jnp.maximum(i - 2, 0) + RANK_UNROLL - 1) // RANK_UNROLL
    cnts = lax.fori_loop(0, ntrip, rank_body, tuple(jnp.zeros(keys[0].shape, jnp.int32) for _ in pair))
    budget = SEL_TOPK - nforced - ntrip * RANK_UNROLL

    for e in pair:
        sel = valid & (forced | (cnts[e] < budget))
        notsel_t = jnp.where(sel, 0.0, 1.0)
        notsel = _dot_tn(notsel_t, eye_ref[...])
        notsel = jnp.concatenate([notsel[g * qb:(g + 1) * qb]
                                  for g in range(NSA_KV_GROUPS) for _ in range(NSA_HPG)], axis=0)
        qs_scr[e] = jnp.concatenate([qbf[e], (notsel + qc_ref[...]).astype(BF16)], axis=1)
    m_scr[...] = jnp.full(m_scr.shape, NEG_BIG, F32)
    acc_scr[...] = jnp.zeros(acc_scr.shape, F32)
    half = SEL_CH // 2
    nhalf = i // (half // qb) + 1
    lead = nhalf % 2
    nfull = nhalf // 2
    base = lead * half
    k0_last = pl.multiple_of(jnp.maximum(base + (nfull - 1) * SEL_CH, 0), half)
    off = pl.multiple_of(i * qb - k0_last, qb)
    kd_scr[...] = ktab_ref[pl.ds(k0_last, SEL_CH), :]
    kd_scr[pl.ds(off, qb), :] = kd_scr[pl.ds(off, qb), :] + kdiag_ref[...]

    def scores(slot, k0, size, kmask):
        for e in pair:
            k = jnp.concatenate([ksl_ref[e, 0, pl.ds(k0, size), :], kmask], axis=1)
            for rs in parts:
                sc = _dot_nt(qs_scr[e, rs, :], k)
                sc_scr[e, slot, rs, 0:size] = sc
                mx = m_scr[e, rs, :]
                for j in range(size // LANES):
                    mx = jnp.maximum(mx, sc[:, j * LANES:(j + 1) * LANES])
                m_scr[e, rs, :] = mx

    def values(slot, k0, size):
        for e in pair:
            v = jnp.concatenate([vsl_ref[e, 0, pl.ds(k0, size), :], ones[0:size]], axis=1)
            for rs in parts:
                mb = m_scr[e, rs, :]
                pr = jnp.concatenate([jnp.exp(sc_scr[e, slot, rs, j * LANES:(j + 1) * LANES] - mb)
                                      for j in range(size // LANES)], axis=1).astype(BF16)
                acc_scr[e, rs, :] += _dot(pr, v)

    @pl.when(lead == 1)
    def _():
        scores(0, 0, half, jnp.where(nfull == 0, kd_scr[0:half, :], ktab_ref[0:half, :]))

    def score_body(c, carry):
        k0 = pl.multiple_of(base + c * SEL_CH, half)
        scores(1 + c, k0, SEL_CH, jnp.where(c == nfull - 1, kd_scr[...], ktab_ref[pl.ds(k0, SEL_CH), :]))
        return carry

    lax.fori_loop(0, nfull, score_body, 0)
    for e in pair:
        m_scr[e] = jnp.broadcast_to(jnp.max(m_scr[e], axis=1, keepdims=True), (rows, LANES))

    @pl.when(lead == 1)
    def _():
        values(0, 0, half)

    def value_body(c, carry):
        values(1 + c, pl.multiple_of(base + c * SEL_CH, half), SEL_CH)
        return carry

    lax.fori_loop(0, nfull, value_body, 0)

    for e in pair:
        o_slc = acc_scr[e, :, 0:LANES] / acc_scr[e, :, LANES:2 * LANES]
        g_hi = jax.nn.sigmoid(g_ref[e, tr, :])
        g_lo = g_hi - g_hi.astype(BF16).astype(F32)
        gate = _dot(jnp.concatenate([g_hi.astype(BF16), g_lo.astype(BF16)], axis=1), gexp_ref[...])
        o_ref[e, tr, :] = (gate[:, 0:NSA_W] * _unpad_heads(cmp_scr[e])
                    + gate[:, NSA_W:2 * NSA_W] * _unpad_heads(o_slc)
                    + gate[:, 2 * NSA_W:3 * NSA_W] * _unpad_heads(win_scr[e]))


def _nsa(qpad, gts, kcmp, vcmp, ksl, vsl, kwn, vwn, consts, batch, seq):
    qb = SEL_BLOCK
    nq = seq // qb
    rows = NSA_HEADS * qb
    npair = NSA_PAIR if batch % NSA_PAIR == 0 else 1
    nbh = batch // npair
    tok = lambda a: a.reshape(npair, nbh * seq, a.shape[-1])
    per = lambda a: a.reshape(npair, nbh, a.shape[-2], KV_W)
    assert nq % NSA_STEP_BLOCKS == 0
    nst = nq // NSA_STEP_BLOCKS
    tok_spec = lambda w: pl.BlockSpec((npair, NSA_STEP_BLOCKS * qb, w), lambda bh, i: (0, bh * nst + i, 0))
    per_spec = lambda n: pl.BlockSpec((npair, 1, n, KV_W), lambda bh, i: (0, bh, 0, 0))
    full = lambda a: pl.BlockSpec(a.shape, lambda bh, i: (0,) * a.ndim)
    out = pl.pallas_call(
        _nsa_kernel,
        grid=(nbh, nst),
        in_specs=[
            tok_spec(QPAD_W), tok_spec(LANES),
            per_spec(kcmp.shape[1]), per_spec(vcmp.shape[1]),
            per_spec(seq), per_spec(seq), per_spec(seq), per_spec(seq),
        ] + [full(c) for c in consts],
        out_specs=tok_spec(NSA_W),
        out_shape=jax.ShapeDtypeStruct((npair, nbh * seq, NSA_W), F32),
        scratch_shapes=[
            pltpu.VMEM((npair, consts[-1].shape[0], NSA_KV_GROUPS * qb), F32),
            pltpu.VMEM((npair, rows, LANES), F32),
            pltpu.VMEM((npair, rows, 2 * LANES), F32),
            pltpu.VMEM((npair, rows, LANES), F32),
            pltpu.VMEM((SEL_CH, LANES), BF16),
            pltpu.VMEM((npair, rows, 2 * LANES), BF16),
            pltpu.VMEM((npair, seq // SEL_CH + 1, rows, SEL_CH), F32),
            pltpu.VMEM((npair, rows, LANES), F32),
        ],
        compiler_params=_cparams(("parallel", "arbitrary")),
        name="nsa",
    )(tok(qpad), tok(gts), per(kcmp), per(vcmp), per(ksl.reshape(batch, seq, KV_W)),
      per(vsl.reshape(batch, seq, KV_W)), per(kwn.reshape(batch, seq, KV_W)),
      per(vwn.reshape(batch, seq, KV_W)), *consts)
    return out.reshape(batch * seq, NSA_W)


def _nsa_constants(seq):
    qb = SEL_BLOCK
    nc_pad = seq // CMP_STRIDE
    nsb = seq // qb
    assert nsb <= qb
    nsb_pad = max(nsb, 8)
    cs = np.arange(nc_pad) * CMP_STRIDE
    ce = cs + CMP_LEN
    bs = np.arange(nsb_pad) * qb
    be = bs + qb
    ov = np.clip(np.minimum(ce[None, :], be[:, None]) - np.maximum(cs[None, :], bs[:, None]), 0, None)
    impt = ov / CMP_LEN
    pos = np.arange(seq)
    lane = np.arange(LANES)[None, :]
    ktab = np.where(lane == pos[:, None] // qb, NEG_BIG, 0.0)
    kdiag = np.where(lane == qb + np.arange(qb)[:, None], NEG_BIG, 0.0)
    last = WIN_KEYS // qb - 1
    x = np.arange(WIN_KEYS + last * qb)[:, None]
    kb, j = x // qb, x % qb
    kwtab = np.where((((kb == 0) | (kb > last)) & (lane == 0)) | ((kb == 1) & (lane == j))
                     | ((kb == last) & (lane == qb + j)), NEG_BIG, 0.0)
    r = (np.arange(NSA_HEADS * qb) % qb)[:, None]
    upper = (lane >= qb) & (lane - qb > r)
    qc = np.where(upper, 1.0, 0.0)
    qw = np.where(upper | ((lane < qb) & (lane <= r)), 1.0, 0.0)
    eye = np.eye(nsb_pad, LANES)
    col = np.arange(GEXP_W)[None, :]
    src = np.arange(LANES)[:, None]
    rep = (src == (col % NSA_W) // HEAD_DIM * 3 + col // NSA_W) & (src < 3 * NSA_HEADS)
    gexp = np.concatenate([rep, rep], axis=0)
    return (jnp.asarray(ktab, BF16), jnp.asarray(kdiag, BF16), jnp.asarray(kwtab, BF16),
            jnp.asarray(qc, F32), jnp.asarray(qw, BF16), jnp.asarray(eye, F32), jnp.asarray(gexp, BF16),
            jnp.asarray(impt, F32))


def _split3(x):
    hi = x.astype(BF16)
    r1 = x - hi.astype(F32)
    mid = r1.astype(BF16)
    lo = (r1 - mid.astype(F32)).astype(BF16)
    return hi, mid, lo


def _gla_kernel(q_ref, k_ref, v_ref, a_ref, r_ref, w6_ref, bup_ref, ng_ref, o_ref, st_ref):
    pair = range(q_ref.shape[0])
    blk = q_ref.shape[1]
    nsub = blk // GLA_SUB

    @pl.when(pl.program_id(1) == 0)
    def _():
        st_ref[...] = jnp.zeros(st_ref.shape, F32)

    ri = lax.broadcasted_iota(jnp.int32, (blk, blk), 0)
    ci = lax.broadcasted_iota(jnp.int32, (blk, blk), 1)
    ltri = jnp.where(ci <= ri, 1.0, 0.0).astype(BF16)
    ltri3 = jnp.concatenate([ltri, ltri, ltri], axis=1)
    lane_rep = lax.broadcasted_iota(jnp.int32, (1, LANES), 1) // GLA_RANK
    srow = lax.broadcasted_iota(jnp.int32, st_ref.shape[1:], 0) // GLA_DV
    scol = lax.broadcasted_iota(jnp.int32, st_ref.shape[1:], 1) // GLA_DK
    krow = lax.broadcasted_iota(jnp.int32, (blk, 1), 0)
    lane_qk = lax.broadcasted_iota(jnp.int32, (1, GLA_QK_W), 1) // GLA_DK
    lane_v = lax.broadcasted_iota(jnp.int32, (1, GLA_W), 1) // GLA_DV
    qrow = lax.broadcasted_iota(jnp.int32, (GLA_HEADS * GLA_SUB, 1), 0) % GLA_SUB
    kcol = lax.broadcasted_iota(jnp.int32, (1, blk), 1)
    gi = lax.broadcasted_iota(jnp.int32, (GLA_W, GLA_W), 0) // GLA_DV
    gj = lax.broadcasted_iota(jnp.int32, (GLA_W, GLA_W), 1) // GLA_DV
    gmean = jnp.where(gi == gj, 1.0 / GLA_DV, 0.0).astype(BF16)
    gmean2 = jnp.concatenate([gmean, gmean], axis=0)

    for e in pair:
        a_hi, a_mid, a_lo = _split3(a_ref[e])
        a6 = jnp.where(lane_rep < 3, a_hi, jnp.where(lane_rep < 5, a_mid, a_lo))
        x = _dot(a6, w6_ref[...]) + bup_ref[...]
        g = (jnp.minimum(x, 0.0) - jnp.log1p(jnp.exp(-jnp.abs(x)))) / GLA_TAU
        b = _dot(ltri3, jnp.concatenate(_split3(g), axis=0))
        blast = b[blk - 1:blk, :]
        q = q_ref[e] * (GLA_DK ** -0.5)
        k = k_ref[e]
        v = v_ref[e]
        vb = v.astype(BF16)

        st = st_ref[e]
        o = _dot_nt((q * jnp.exp(b)).astype(BF16), st.astype(BF16))
        khat = k * jnp.exp(blast - b)
        st_ref[e] = st * jnp.exp(blast) + jnp.where(srow == scol, _dot_tn(v, khat), 0.0)

        for c in range(nsub):
            lo = c * GLA_SUB
            ref_b = b[lo:lo + 1, :]
            qt = q[lo:lo + GLA_SUB] * jnp.exp(b[lo:lo + GLA_SUB] - ref_b)
            kt = k * jnp.exp(jnp.where(krow < lo + GLA_SUB, ref_b - b, 0.0))
            qs = jnp.concatenate([jnp.where(lane_qk == h, qt, 0.0) for h in range(GLA_HEADS)], axis=0)
            a = _dot_nt(qs.astype(BF16), kt.astype(BF16))
            a = jnp.where(kcol <= lo + qrow, a, 0.0)
            r = _dot(a.astype(BF16), vb)
            oi = sum(jnp.where(lane_v == h, r[h * GLA_SUB:(h + 1) * GLA_SUB], 0.0) for h in range(GLA_HEADS))
            o_ref[e, lo:lo + GLA_SUB, :] = o[lo:lo + GLA_SUB] + oi

    for e in pair:
        o = o_ref[e]
        oo = o * o
        oo_hi = oo.astype(BF16)
        oo_lo = (oo - oo_hi.astype(F32)).astype(BF16)
        ms = _dot(jnp.concatenate([oo_hi, oo_lo], axis=1), gmean2)
        rr = r_ref[e]
        o_ref[e] = o * lax.rsqrt(ms + NORM_EPS) * ng_ref[...] * (rr * jax.nn.sigmoid(rr))


def _prep_gla_wup(w_up):
    hi = w_up.astype(BF16)
    r1 = w_up - hi.astype(F32)
    mid = r1.astype(BF16)
    lo = (r1 - mid.astype(F32)).astype(BF16)
    w6 = jnp.concatenate([hi, mid, lo, hi, mid, hi], axis=0)
    return jnp.pad(w6, ((0, LANES - w6.shape[0]), (0, 0)))


def _gla(gq, gk, gv, ga, gr, w6, bup, ng, batch, seq):
    blk = min(GLA_BLK, seq)
    nb = seq // blk
    npair = GLA_PAIR if batch % GLA_PAIR == 0 else 1
    nbh = batch // npair
    tok = lambda a: a.reshape(npair, nbh * seq, a.shape[-1])
    spec = lambda w: pl.BlockSpec((npair, blk, w), lambda bh, j: (0, bh * nb + j, 0))
    full = lambda a: pl.BlockSpec(a.shape, lambda bh, j: (0,) * a.ndim)
    out = pl.pallas_call(
        _gla_kernel,
        grid=(nbh, nb),
        in_specs=[spec(GLA_QK_W), spec(GLA_QK_W), spec(GLA_W), spec(LANES), spec(GLA_W),
                  full(w6), full(bup), full(ng)],
        out_specs=spec(GLA_W),
        out_shape=jax.ShapeDtypeStruct((npair, nbh * seq, GLA_W), F32),
        scratch_shapes=[pltpu.VMEM((npair, GLA_W, GLA_QK_W), F32)],
        compiler_params=_cparams(("parallel", "arbitrary")),
        name="gla",
    )(tok(gq), tok(gk), tok(gv), tok(ga), tok(gr), w6, bup, ng)
    return out.reshape(batch * seq, GLA_W)


def _outffn_kernel(x_ref, on_ref, og_ref, os_ref, wo_ref, fg_ref, wg_ref, wu_ref, wd_ref,
                   fin_ref, o_ref, *, final):
    x = x_ref[...]
    x = x + _dot(on_ref[...].astype(BF16), wo_ref[0:NSA_W, :])
    x = x + _dot(og_ref[...].astype(BF16), wo_ref[NSA_W:NSA_W + GLA_W, :])
    x = x + _dot(os_ref[...].astype(BF16), wo_ref[NSA_W + GLA_W:, :])
    hn = (x * lax.rsqrt(jnp.mean(x * x, axis=-1, keepdims=True) + NORM_EPS) * fg_ref[...]).astype(BF16)
    gt = _dot(hn, wg_ref[...])
    up = _dot(hn, wu_ref[...])
    y = x + _dot((gt * jax.nn.sigmoid(gt) * up).astype(BF16), wd_ref[...])
    if final:
        y = y * lax.rsqrt(jnp.mean(y * y, axis=-1, keepdims=True) + NORM_EPS) * fin_ref[...]
    o_ref[...] = y


def _outffn(x2, o_nsa, o_gla, o_sg, wo, fgain, wg, wu, wd, fin, tm, final):
    T, D = x2.shape
    row = lambda w: pl.BlockSpec((tm, w), lambda i: (i, 0))
    const = lambda a: pl.BlockSpec(a.shape, lambda i: (0,) * a.ndim, pipeline_mode=pl.Buffered(1))
    return pl.pallas_call(
        functools.partial(_outffn_kernel, final=final),
        grid=(T // tm,),
        in_specs=[row(D), row(NSA_W), row(GLA_W), row(SG_W), const(wo), const(fgain),
                  const(wg), const(wu), const(wd), const(fin)],
        out_specs=row(D),
        out_shape=jax.ShapeDtypeStruct((T, D), F32),
        compiler_params=_cparams(("parallel",)),
        name="outproj_ffn",
    )(x2, o_nsa, o_gla, o_sg, wo, fgain, wg, wu, wd, fin)


def kernel(x, attn_norm, w_in, cmp_pos_k, cmp_w1_k, cmp_w2_k, cmp_pos_v, cmp_w1_v, cmp_w2_v,
           gla_w_up, gla_b_up, gla_norm, sg_ln_g, sg_ln_b, sg_w, sg_b, w_out,
           ffn_norm, w_gate, w_up, w_down, final_norm):
    batch, seq, d_model = x.shape
    depth = w_in.shape[0]
    T = batch * seq
    assert seq % GLA_BLK == 0 and seq >= WIN_KEYS and seq % SEL_CH == 0
    tm_in = tm_ffn = min(ROW_TILE, seq)

    cos_t, sin_t = _rope_tables(jnp.arange(seq))
    nrow = seq // CMP_STRIDE
    cos_c, sin_c = _rope_tables(jnp.arange(nrow) * CMP_STRIDE + CMP_LEN - 1)
    nsa_consts = _nsa_constants(seq)

    x2 = x.reshape(T, d_model)
    for l in range(depth):
        sg_params = (sg_ln_g[l][None, :], sg_ln_b[l][None, :], sg_w[l],
                     jnp.repeat(sg_b[l].T, SG_CH, axis=1))
        (o_sg, qpad, kc, vc, ksl, vsl, kwn, vwn, gts, gq, gk, gv, ga, gr) = _inproj(
            x2, attn_norm[l][None, :], _prep_w_in(w_in[l]), cos_t, sin_t, sg_params, seq, tm_in)
        kcmp, vcmp = _compress(
            kc, vc, _prep_cmp_weights(cmp_w1_k[l], cmp_w2_k[l], cmp_pos_k[l]),
            _prep_cmp_weights(cmp_w1_v[l], cmp_w2_v[l], cmp_pos_v[l]), cos_c, sin_c, batch, seq)
        o_nsa = _nsa(qpad, gts, kcmp, vcmp, ksl, vsl, kwn, vwn, nsa_consts, batch, seq)
        o_gla = _gla(gq, gk, gv, ga, gr, _prep_gla_wup(gla_w_up[l]), gla_b_up[l][None, :],
                     jnp.tile(gla_norm[l], GLA_HEADS)[None, :], batch, seq)
        x2 = _outffn(x2, o_nsa, o_gla, o_sg, w_out[l].astype(BF16), ffn_norm[l][None, :],
                     w_gate[l].astype(BF16), w_up[l].astype(BF16), w_down[l].astype(BF16),
                     final_norm[None, :], tm_ffn, final=(l == depth - 1))
    return x2.reshape(batch, seq, d_model)
```

```python
import functools
import math

import numpy as np
import jax
import jax.numpy as jnp
from jax import lax
from jax.experimental import pallas as pl
from jax.experimental.pallas import tpu as pltpu

HEAD_DIM = 64
NSA_HEADS = 8
NSA_KV_GROUPS = 2
NSA_HPG = NSA_HEADS // NSA_KV_GROUPS
CMP_LEN = 32
CMP_STRIDE = 16
CMP_HIDDEN = 256
SEL_BLOCK = 64
SEL_TOPK = 16
WINDOW = 512
GLA_HEADS = 4
GLA_DK = 32
GLA_DV = 64
GLA_RANK = 16
GLA_TAU = 16.0
SG_GROUPS = 4
SG_CH = 64
SG_CHUNK = 128
NSA_W = NSA_HEADS * HEAD_DIM
GLA_W = GLA_HEADS * GLA_DV
SG_W = SG_GROUPS * SG_CH
KV_W = NSA_KV_GROUPS * HEAD_DIM
IN_SIZES = (NSA_W, KV_W, KV_W, KV_W, KV_W, KV_W, KV_W, NSA_HEADS * 3,
            GLA_HEADS * GLA_DK, GLA_HEADS * GLA_DK, GLA_W, GLA_RANK, GLA_W, 2 * SG_W)
ROPE_THETA = 10000.0
NORM_EPS = 1e-6

LANES = 128
NEG_BIG = -1e30
VMEM_LIMIT = 56 * 1024 * 1024

QPAD_W = NSA_HEADS * LANES
GEXP_W = 3 * NSA_W
GLA_QK_W = GLA_HEADS * GLA_DK
GLA_BLK = 256
GLA_SUB = 16
SEL_CH = 1024
WIN_KEYS = WINDOW + 2 * SEL_BLOCK
NSA_ROW_PARTS = 1
NSA_STEP_BLOCKS = 4
RANK_UNROLL = 4
NSA_PAIR = 2
GLA_PAIR = 4

ROW_TILE = 512
GLA_A_COPIES = 6
F32 = jnp.float32
BF16 = jnp.bfloat16


def _cparams(sem):
    return pltpu.CompilerParams(dimension_semantics=sem, vmem_limit_bytes=VMEM_LIMIT)


def _dot(a, b, precision=None):
    return jnp.dot(a, b, preferred_element_type=F32, precision=precision)


def _dot_nt(a, b, precision=None):
    return lax.dot_general(a, b, (((1,), (1,)), ((), ())),
                           preferred_element_type=F32, precision=precision)


def _dot_tn(a, b, precision=None):
    return lax.dot_general(a, b, (((0,), (0,)), ((), ())),
                           preferred_element_type=F32, precision=precision)


def _gelu_tanh(x):
    c = math.sqrt(2.0 / math.pi)
    return 0.5 * x * (1.0 + jnp.tanh(c * (x + 0.044715 * (x * x * x))))


def _rope_lanes(x, cos, sin_signed):
    n = x.shape[-1]
    lane = lax.broadcasted_iota(jnp.int32, x.shape, 1)
    first_half = (lane % HEAD_DIM) < (HEAD_DIM // 2)
    partner = jnp.where(first_half,
                        pltpu.roll(x, n - HEAD_DIM // 2, 1),
                        pltpu.roll(x, HEAD_DIM // 2, 1))
    return x * cos + partner * sin_signed


_INPROJ_OUT = (
    ("sg", SG_W, F32),
    ("qpad", QPAD_W, BF16), ("kc", KV_W, F32), ("vc", KV_W, F32),
    ("ksl", KV_W, BF16), ("vsl", KV_W, BF16), ("kwn", KV_W, BF16), ("vwn", KV_W, BF16),
    ("gts", LANES, F32), ("gq", GLA_QK_W, F32), ("gk", GLA_QK_W, F32),
    ("gv", GLA_W, F32), ("ga", LANES, F32), ("gr", GLA_W, F32))
_INPROJ_W = sum(w for _, w, _ in _INPROJ_OUT) + SG_W


def _spatial_gate(uv, lg, lb, wts, bias, lane_g):
    a = _gelu_tanh(uv)
    u = a[:, :SG_W]
    v = a[:, SG_W:]
    mu = jnp.mean(v, axis=-1, keepdims=True)
    var = jnp.mean(jnp.square(v - mu), axis=-1, keepdims=True)
    vn = ((v - mu) * lax.rsqrt(var + NORM_EPS) * lg + lb).astype(BF16)
    s = bias
    for g in range(SG_GROUPS):
        s = s + jnp.where(lane_g == g, _dot(wts[g], vn), 0.0)
    return u * s


def _inproj_kernel(x_ref, g_ref, w_ref, cos_ref, sin_ref, lg_ref, lb_ref, ws_ref, sb_ref, *refs):
    out_refs, grp_scr = refs[:-1], refs[-1]
    x = x_ref[...]
    hn = x * lax.rsqrt(jnp.mean(x * x, axis=-1, keepdims=True) + NORM_EPS) * g_ref[...]
    z = _dot(hn.astype(BF16), w_ref[...])
    cos = cos_ref[...]
    sin = sin_ref[...]
    off = 0
    for (name, width, dtype), o_ref in zip(_INPROJ_OUT, out_refs):
        if name in ("qpad", "ksl", "kwn"):
            for j in range(width // LANES):
                blk = _rope_lanes(z[:, off + j * LANES: off + (j + 1) * LANES], cos, sin)
                if name == "qpad":
                    blk = blk * (HEAD_DIM ** -0.5)
                o_ref[:, j * LANES:(j + 1) * LANES] = blk.astype(dtype)
        elif name in ("kc", "vc"):
            grp_scr[...] = z[:, off:off + width]
            for r in range(CMP_STRIDE):
                o_ref[:, r * width:(r + 1) * width] = grp_scr[pl.ds(r, x.shape[0] // CMP_STRIDE, stride=CMP_STRIDE), :]
        elif name == "sg":
            n = ws_ref.shape[1]
            ri = lax.broadcasted_iota(jnp.int32, (n, n), 0)
            ci = lax.broadcasted_iota(jnp.int32, (n, n), 1)
            lane_g = lax.broadcasted_iota(jnp.int32, (1, SG_W), 1) // SG_CH
            wts = [jnp.where(ci <= ri, ws_ref[g], 0.0).astype(BF16) for g in range(SG_GROUPS)]
            for c in range(x.shape[0] // n):
                rs = slice(c * n, (c + 1) * n)
                o_ref[rs, :] = _spatial_gate(z[rs, off:off + 2 * SG_W], lg_ref[...], lb_ref[...], wts,
                                             sb_ref[...], lane_g)
        else:
            o_ref[...] = z[:, off:off + width].astype(dtype)
        off += 2 * SG_W if name == "sg" else width


def _inproj(x2, gain, w_p, cos_t, sin_t, sg_params, seq, tm):
    T, D = x2.shape
    nper = seq // tm
    assert tm % SG_CHUNK == 0
    regroup = lambda n: CMP_STRIDE if n in ("kc", "vc") else 1
    out_shape = [jax.ShapeDtypeStruct((T // regroup(n), w * regroup(n)), dt) for n, w, dt in _INPROJ_OUT]
    out_specs = [pl.BlockSpec((tm // regroup(n), w * regroup(n)), lambda i: (i, 0)) for n, w, _ in _INPROJ_OUT]
    full = lambda a: pl.BlockSpec(a.shape, lambda i: (0,) * a.ndim)
    return pl.pallas_call(
        _inproj_kernel,
        grid=(T // tm,),
        in_specs=[
            pl.BlockSpec((tm, D), lambda i: (i, 0)),
            pl.BlockSpec((1, D), lambda i: (0, 0)),
            pl.BlockSpec((D, _INPROJ_W), lambda i: (0, 0), pipeline_mode=pl.Buffered(1)),
            pl.BlockSpec((tm, LANES), lambda i: (i % nper, 0)),
            pl.BlockSpec((tm, LANES), lambda i: (i % nper, 0)),
        ] + [full(a) for a in sg_params],
        out_specs=out_specs,
        out_shape=out_shape,
        scratch_shapes=[pltpu.VMEM((tm, KV_W), F32)],
        compiler_params=_cparams(("parallel",)),
        name="inproj",
    )(x2, gain, w_p, cos_t, sin_t, *sg_params)


def _prep_w_in(w_in):
    offs = np.cumsum((0,) + IN_SIZES)
    seg = [w_in[:, offs[k]:offs[k + 1]] for k in range(len(IN_SIZES))]
    (wq, wkc, wvc, wksl, wvsl, wkwn, wvwn, wg, wgq, wgk, wgv, wga, wgr, wuv) = seg
    D = w_in.shape[0]
    zeros64 = jnp.zeros((D, HEAD_DIM), w_in.dtype)
    qcols = []
    for h in range(NSA_HEADS):
        wh = wq[:, h * HEAD_DIM:(h + 1) * HEAD_DIM]
        qcols += [wh, zeros64] if h // NSA_HPG == 0 else [zeros64, wh]
    gcols = [jnp.pad(wg, ((0, 0), (0, LANES - wg.shape[1])))]
    wga_p = jnp.pad(jnp.concatenate([wga] * GLA_A_COPIES, axis=1), ((0, 0), (0, LANES - GLA_A_COPIES * GLA_RANK)))
    return jnp.concatenate([wuv] + qcols + [wkc, wvc, wksl, wvsl, wkwn, wvwn] + gcols
                           + [wgq, wgk, wgv, wga_p, wgr], axis=1).astype(BF16)


def _rope_tables(pos):
    half = HEAD_DIM // 2
    inv = 1.0 / (ROPE_THETA ** (jnp.arange(half, dtype=F32) / half))
    ang = pos.astype(F32)[:, None] * inv[None, :]
    cos = jnp.cos(ang)
    sin = jnp.sin(ang)
    cos_t = jnp.concatenate([cos, cos, cos, cos], axis=1)
    sin_t = jnp.concatenate([-sin, sin, -sin, sin], axis=1)
    return cos_t, sin_t


def _cmp_kernel(xk_ref, xv_ref, wek_ref, wev_ref, w1k_ref, w1v_ref, pk_ref, pv_ref,
                w2k_ref, w2v_ref, cos_ref, sin_ref, ok_ref, ov_ref):
    nrow = xk_ref.shape[1]
    row = lax.broadcasted_iota(jnp.int32, (nrow, LANES), 0)

    def compress(x_ref, we_ref, w1_ref, p_ref, w2_ref):
        h = _dot(x_ref[0].astype(BF16), we_ref[...])
        posb = _dot(p_ref[...].astype(BF16), w1_ref[...])[0:1, :]
        y = jnp.zeros((nrow, LANES), F32)
        for g in range(NSA_KV_GROUPS):
            a = h[:, g * CMP_HIDDEN:(g + 1) * CMP_HIDDEN]
            b = h[:, (NSA_KV_GROUPS + g) * CMP_HIDDEN:(NSA_KV_GROUPS + g + 1) * CMP_HIDDEN]
            hid = a + pltpu.roll(b, nrow - 1, 0) + posb
            y = y + _dot(_gelu_tanh(hid).astype(BF16), w2_ref[g])
        return y

    yk = _rope_lanes(compress(xk_ref, wek_ref, w1k_ref, pk_ref, w2k_ref), cos_ref[...], sin_ref[...])
    yv = compress(xv_ref, wev_ref, w1v_ref, pv_ref, w2v_ref)
    keep = row < nrow - 1
    ok_ref[0] = jnp.where(keep, yk, 0.0)
    ov_ref[0] = jnp.where(keep, yv, 0.0)


def _prep_cmp_weights(w1, w2, pos):
    half = CMP_LEN // CMP_STRIDE
    H = w1.shape[1]
    w1r = w1.reshape(half, CMP_STRIDE, HEAD_DIM, H)
    cols = []
    for a in range(half):
        for g in range(NSA_KV_GROUPS):
            blk = jnp.zeros((CMP_STRIDE, NSA_KV_GROUPS, HEAD_DIM, H), w1.dtype)
            blk = blk.at[:, g].set(w1r[a])
            cols.append(blk.reshape(CMP_STRIDE * KV_W, H))
    wexp = jnp.concatenate(cols, axis=1).astype(BF16)
    w2p = jnp.stack([jnp.pad(w2, ((0, 0), (g * HEAD_DIM, KV_W - (g + 1) * HEAD_DIM)))
                     for g in range(NSA_KV_GROUPS)]).astype(BF16)
    posf = jnp.pad(pos.reshape(1, CMP_LEN * HEAD_DIM), ((0, 7), (0, 0)))
    return wexp, w1.astype(BF16), posf, w2p


def _compress(kc, vc, wk, wv, cos_c, sin_c, batch, seq):
    nrow = seq // CMP_STRIDE
    xk = kc.reshape(batch, nrow, CMP_STRIDE * KV_W)
    xv = vc.reshape(batch, nrow, CMP_STRIDE * KV_W)
    wek, w1k, pk, w2k = wk
    wev, w1v, pv, w2v = wv
    full = lambda a: pl.BlockSpec(a.shape, lambda b: (0,) * a.ndim)
    xspec = pl.BlockSpec((1, nrow, CMP_STRIDE * KV_W), lambda b: (b, 0, 0))
    ospec = pl.BlockSpec((1, nrow, KV_W), lambda b: (b, 0, 0))
    return pl.pallas_call(
        _cmp_kernel,
        grid=(batch,),
        in_specs=[xspec, xspec, full(wek), full(wev), full(w1k), full(w1v), full(pk), full(pv),
                  full(w2k), full(w2v), full(cos_c), full(sin_c)],
        out_specs=[ospec, ospec],
        out_shape=[jax.ShapeDtypeStruct((batch, nrow, KV_W), F32)] * 2,
        compiler_params=_cparams(("parallel",)),
        name="compress",
    )(xk, xv, wek, wev, w1k, w1v, pk, pv, w2k, w2v, cos_c, sin_c)


def _unpad_heads(acc):
    qb = SEL_BLOCK
    lane = lax.broadcasted_iota(jnp.int32, (qb, LANES), 1)
    low = lane < HEAD_DIM
    outs = []
    for j in range(NSA_HEADS // 2):
        a = acc[(2 * j) * qb:(2 * j + 1) * qb]
        b = acc[(2 * j + 1) * qb:(2 * j + 2) * qb]
        if (2 * j) // NSA_HPG == 0:
            outs.append(jnp.where(low, a, pltpu.roll(b, HEAD_DIM, 1)))
        else:
            outs.append(jnp.where(low, pltpu.roll(a, HEAD_DIM, 1), b))
    return jnp.concatenate(outs, axis=1)


def _nsa_kernel(*refs):
    for j in range(NSA_STEP_BLOCKS):
        _nsa_block(j, *refs)


def _nsa_block(j, q_ref, g_ref, kc_ref, vc_ref, ksl_ref, vsl_ref, kwn_ref, vwn_ref,
               ktab_ref, kdiag_ref, kwtab_ref, qc_ref, qw_ref, eye_ref, gexp_ref, impt_ref, o_ref,
               imp_scr, m_scr, acc_scr, win_scr, kd_scr, qs_scr, sc_scr, cmp_scr):
    qb = SEL_BLOCK
    rows = NSA_HEADS * qb
    pair = range(q_ref.shape[0])
    i = pl.program_id(1) * NSA_STEP_BLOCKS + j
    tr = slice(j * qb, (j + 1) * qb)
    t = i * qb + lax.broadcasted_iota(jnp.int32, (rows, 1), 0) % qb
    ones = jnp.ones((SEL_CH, LANES), BF16)
    parts = [slice(p * rows // NSA_ROW_PARTS, (p + 1) * rows // NSA_ROW_PARTS) for p in range(NSA_ROW_PARTS)]
    qbf = [jnp.concatenate([q_ref[e, tr, h * LANES:(h + 1) * LANES] for h in range(NSA_HEADS)], axis=0)
           for e in pair]

    nblk_w = WIN_KEYS // qb
    ib = jnp.minimum(i, nblk_w - 1)
    w0 = pl.multiple_of((i - ib) * qb, qb)
    toff = pl.multiple_of((nblk_w - 1 - ib) * qb, qb)
    kwmask = kwtab_ref[pl.ds(toff, WIN_KEYS), :]
    for e in pair:
        kw = jnp.concatenate([kwn_ref[e, 0, pl.ds(w0, WIN_KEYS), :], kwmask], axis=1)
        vw = jnp.concatenate([vwn_ref[e, 0, pl.ds(w0, WIN_KEYS), :], ones[0:WIN_KEYS]], axis=1)
        q_win = jnp.concatenate([qbf[e], qw_ref[...]], axis=1)
        for rs in parts:
            sw = _dot_nt(q_win[rs], kw)
            ew = jnp.exp(sw - jnp.max(sw, axis=1, keepdims=True)).astype(BF16)
            ow = _dot(ew, vw)
            win_scr[e, rs, :] = ow[:, 0:LANES] / ow[:, LANES:2 * LANES]

    ncmp = kc_ref.shape[2]
    cend = lax.broadcasted_iota(jnp.int32, (1, ncmp), 1) * CMP_STRIDE + (CMP_LEN - 1)
    vis = cend <= t
    imap = impt_ref[...].astype(BF16)
    imap2 = jnp.concatenate([imap, imap], axis=1)
    for e in pair:
        s = _dot_nt(qbf[e], kc_ref[e, 0].astype(BF16))
        sm = jnp.where(vis, s, NEG_BIG)
        mx = jnp.max(sm, axis=1, keepdims=True)
        ex = jnp.where(vis, jnp.exp(sm - mx), 0.0)
        d = jnp.sum(ex, axis=1, keepdims=True)
        p = ex / jnp.where(d > 0, d, 1.0)
        cmp_scr[e] = _dot(p.astype(BF16), vc_ref[e, 0].astype(BF16))
        psum = jnp.concatenate(
            [sum(p[(g * NSA_HPG + h) * qb:(g * NSA_HPG + h + 1) * qb] for h in range(NSA_HPG))
             for g in range(NSA_KV_GROUPS)], axis=0)
        p_hi = psum.astype(BF16)
        p_lo = (psum - p_hi.astype(F32)).astype(BF16)
        imp_scr[e] = jnp.abs(_dot_nt(imap2, jnp.concatenate([p_hi, p_lo], axis=1)))

    nsbp = imp_scr.shape[1]
    keys = [pltpu.bitcast(imp_scr[e], jnp.int32) for e in pair]
    n_id = lax.broadcasted_iota(jnp.int32, (nsbp, 1), 0)
    n_full = lax.broadcasted_iota(jnp.int32, (nsbp, LANES), 0)
    valid = n_id <= i
    forced = (n_id == 0) | (n_id == i) | (n_id == i - 1)
    nforced = 1 + (i >= 1).astype(jnp.int32) + (i >= 2).astype(jnp.int32)

    def rank_body(j, cnts):
        cnts = list(cnts)
        for u in range(RANK_UNROLL):
            m = 1 + RANK_UNROLL * j + u
            tadj = jnp.right_shift(n_full - 1 - m, 31)
            for e in pair:
                rowk = pltpu.bitcast(imp_scr[e, pl.ds(jnp.minimum(m, nsbp - 1), 1), :], jnp.int32)
                rowk = jnp.where(m <= i - 2, rowk, -1)
                cnts[e] = cnts[e] + jnp.right_shift(rowk - keys[e] + tadj, 31)
        return tuple(cnts)

    ntrip = (jnp.maximum(i - 2, 0) + RANK_UNROLL - 1) // RANK_UNROLL
    cnts = lax.fori_loop(0, ntrip, rank_body, tuple(jnp.zeros(keys[0].shape, jnp.int32) for _ in pair))
    budget = SEL_TOPK - nforced - ntrip * RANK_UNROLL

    for e in pair:
        sel = valid & (forced | (cnts[e] < budget))
        notsel_t = jnp.where(sel, 0.0, 1.0)
        notsel = _dot_tn(notsel_t, eye_ref[...])
        notsel = jnp.concatenate([notsel[g * qb:(g + 1) * qb]
                                  for g in range(NSA_KV_GROUPS) for _ in range(NSA_HPG)], axis=0)
        qs_scr[e] = jnp.concatenate([qbf[e], (notsel + qc_ref[...]).astype(BF16)], axis=1)
    m_scr[...] = jnp.full(m_scr.shape, NEG_BIG, F32)
    acc_scr[...] = jnp.zeros(acc_scr.shape, F32)
    half = SEL_CH // 2
    nhalf = i // (half // qb) + 1
    lead = nhalf % 2
    nfull = nhalf // 2
    base = lead * half
    k0_last = pl.multiple_of(jnp.maximum(base + (nfull - 1) * SEL_CH, 0), half)
    off = pl.multiple_of(i * qb - k0_last, qb)
    kd_scr[...] = ktab_ref[pl.ds(k0_last, SEL_CH), :]
    kd_scr[pl.ds(off, qb), :] = kd_scr[pl.ds(off, qb), :] + kdiag_ref[...]

    def scores(slot, k0, size, kmask):
        for e in pair:
            k = jnp.concatenate([ksl_ref[e, 0, pl.ds(k0, size), :], kmask], axis=1)
            for rs in parts:
                sc = _dot_nt(qs_scr[e, rs, :], k)
                sc_scr[e, slot, rs, 0:size] = sc
                mx = m_scr[e, rs, :]
                for j in range(size // LANES):
                    mx = jnp.maximum(mx, sc[:, j * LANES:(j + 1) * LANES])
                m_scr[e, rs, :] = mx

    def values(slot, k0, size):
        for e in pair:
            v = jnp.concatenate([vsl_ref[e, 0, pl.ds(k0, size), :], ones[0:size]], axis=1)
            for rs in parts:
                mb = m_scr[e, rs, :]
                pr = jnp.concatenate([jnp.exp(sc_scr[e, slot, rs, j * LANES:(j + 1) * LANES] - mb)
                                      for j in range(size // LANES)], axis=1).astype(BF16)
                acc_scr[e, rs, :] += _dot(pr, v)

    @pl.when(lead == 1)
    def _():
        scores(0, 0, half, jnp.where(nfull == 0, kd_scr[0:half, :], ktab_ref[0:half, :]))

    def score_body(c, carry):
        k0 = pl.multiple_of(base + c * SEL_CH, half)
        scores(1 + c, k0, SEL_CH, jnp.where(c == nfull - 1, kd_scr[...], ktab_ref[pl.ds(k0, SEL_CH), :]))
        return carry

    lax.fori_loop(0, nfull, score_body, 0)
    for e in pair:
        m_scr[e] = jnp.broadcast_to(jnp.max(m_scr[e], axis=1, keepdims=True), (rows, LANES))

    @pl.when(lead == 1)
    def _():
        values(0, 0, half)

    def value_body(c, carry):
        values(1 + c, pl.multiple_of(base + c * SEL_CH, half), SEL_CH)
        return carry

    lax.fori_loop(0, nfull, value_body, 0)

    for e in pair:
        o_slc = acc_scr[e, :, 0:LANES] / acc_scr[e, :, LANES:2 * LANES]
        g_hi = jax.nn.sigmoid(g_ref[e, tr, :])
        g_lo = g_hi - g_hi.astype(BF16).astype(F32)
        gate = _dot(jnp.concatenate([g_hi.astype(BF16), g_lo.astype(BF16)], axis=1), gexp_ref[...])
        o_ref[e, tr, :] = (gate[:, 0:NSA_W] * _unpad_heads(cmp_scr[e])
                    + gate[:, NSA_W:2 * NSA_W] * _unpad_heads(o_slc)
                    + gate[:, 2 * NSA_W:3 * NSA_W] * _unpad_heads(win_scr[e]))


def _nsa(qpad, gts, kcmp, vcmp, ksl, vsl, kwn, vwn, consts, batch, seq):
    qb = SEL_BLOCK
    nq = seq // qb
    rows = NSA_HEADS * qb
    npair = NSA_PAIR if batch % NSA_PAIR == 0 else 1
    nbh = batch // npair
    tok = lambda a: a.reshape(npair, nbh * seq, a.shape[-1])
    per = lambda a: a.reshape(npair, nbh, a.shape[-2], KV_W)
    assert nq % NSA_STEP_BLOCKS == 0
    nst = nq // NSA_STEP_BLOCKS
    tok_spec = lambda w: pl.BlockSpec((npair, NSA_STEP_BLOCKS * qb, w), lambda bh, i: (0, bh * nst + i, 0))
    per_spec = lambda n: pl.BlockSpec((npair, 1, n, KV_W), lambda bh, i: (0, bh, 0, 0))
    full = lambda a: pl.BlockSpec(a.shape, lambda bh, i: (0,) * a.ndim)
    out = pl.pallas_call(
        _nsa_kernel,
        grid=(nbh, nst),
        in_specs=[
            tok_spec(QPAD_W), tok_spec(LANES),
            per_spec(kcmp.shape[1]), per_spec(vcmp.shape[1]),
            per_spec(seq), per_spec(seq), per_spec(seq), per_spec(seq),
        ] + [full(c) for c in consts],
        out_specs=tok_spec(NSA_W),
        out_shape=jax.ShapeDtypeStruct((npair, nbh * seq, NSA_W), F32),
        scratch_shapes=[
            pltpu.VMEM((npair, consts[-1].shape[0], NSA_KV_GROUPS * qb), F32),
            pltpu.VMEM((npair, rows, LANES), F32),
            pltpu.VMEM((npair, rows, 2 * LANES), F32),
            pltpu.VMEM((npair, rows, LANES), F32),
            pltpu.VMEM((SEL_CH, LANES), BF16),
            pltpu.VMEM((npair, rows, 2 * LANES), BF16),
            pltpu.VMEM((npair, seq // SEL_CH + 1, rows, SEL_CH), F32),
            pltpu.VMEM((npair, rows, LANES), F32),
        ],
        compiler_params=_cparams(("parallel", "arbitrary")),
        name="nsa",
    )(tok(qpad), tok(gts), per(kcmp), per(vcmp), per(ksl.reshape(batch, seq, KV_W)),
      per(vsl.reshape(batch, seq, KV_W)), per(kwn.reshape(batch, seq, KV_W)),
      per(vwn.reshape(batch, seq, KV_W)), *consts)
    return out.reshape(batch * seq, NSA_W)


def _nsa_constants(seq):
    qb = SEL_BLOCK
    nc_pad = seq // CMP_STRIDE
    nsb = seq // qb
    assert nsb <= qb
    nsb_pad = max(nsb, 8)
    cs = np.arange(nc_pad) * CMP_STRIDE
    ce = cs + CMP_LEN
    bs = np.arange(nsb_pad) * qb
    be = bs + qb
    ov = np.clip(np.minimum(ce[None, :], be[:, None]) - np.maximum(cs[None, :], bs[:, None]), 0, None)
    impt = ov / CMP_LEN
    pos = np.arange(seq)
    lane = np.arange(LANES)[None, :]
    ktab = np.where(lane == pos[:, None] // qb, NEG_BIG, 0.0)
    kdiag = np.where(lane == qb + np.arange(qb)[:, None], NEG_BIG, 0.0)
    last = WIN_KEYS // qb - 1
    x = np.arange(WIN_KEYS + last * qb)[:, None]
    kb, j = x // qb, x % qb
    kwtab = np.where((((kb == 0) | (kb > last)) & (lane == 0)) | ((kb == 1) & (lane == j))
                     | ((kb == last) & (lane == qb + j)), NEG_BIG, 0.0)
    r = (np.arange(NSA_HEADS * qb) % qb)[:, None]
    upper = (lane >= qb) & (lane - qb > r)
    qc = np.where(upper, 1.0, 0.0)
    qw = np.where(upper | ((lane < qb) & (lane <= r)), 1.0, 0.0)
    eye = np.eye(nsb_pad, LANES)
    col = np.arange(GEXP_W)[None, :]
    src = np.arange(LANES)[:, None]
    rep = (src == (col % NSA_W) // HEAD_DIM * 3 + col // NSA_W) & (src < 3 * NSA_HEADS)
    gexp = np.concatenate([rep, rep], axis=0)
    return (jnp.asarray(ktab, BF16), jnp.asarray(kdiag, BF16), jnp.asarray(kwtab, BF16),
            jnp.asarray(qc, F32), jnp.asarray(qw, BF16), jnp.asarray(eye, F32), jnp.asarray(gexp, BF16),
            jnp.asarray(impt, F32))


def _split3(x):
    hi = x.astype(BF16)
    r1 = x - hi.astype(F32)
    mid = r1.astype(BF16)
    lo = (r1 - mid.astype(F32)).astype(BF16)
    return hi, mid, lo


def _gla_kernel(q_ref, k_ref, v_ref, a_ref, r_ref, w6_ref, bup_ref, ng_ref, o_ref, st_ref):
    pair = range(q_ref.shape[0])
    blk = q_ref.shape[1]
    nsub = blk // GLA_SUB

    @pl.when(pl.program_id(1) == 0)
    def _():
        st_ref[...] = jnp.zeros(st_ref.shape, F32)

    ri = lax.broadcasted_iota(jnp.int32, (blk, blk), 0)
    ci = lax.broadcasted_iota(jnp.int32, (blk, blk), 1)
    ltri = jnp.where(ci <= ri, 1.0, 0.0).astype(BF16)
    ltri3 = jnp.concatenate([ltri, ltri, ltri], axis=1)
    lane_rep = lax.broadcasted_iota(jnp.int32, (1, LANES), 1) // GLA_RANK
    srow = lax.broadcasted_iota(jnp.int32, st_ref.shape[1:], 0) // GLA_DV
    scol = lax.broadcasted_iota(jnp.int32, st_ref.shape[1:], 1) // GLA_DK
    krow = lax.broadcasted_iota(jnp.int32, (blk, 1), 0)
    lane_qk = lax.broadcasted_iota(jnp.int32, (1, GLA_QK_W), 1) // GLA_DK
    lane_v = lax.broadcasted_iota(jnp.int32, (1, GLA_W), 1) // GLA_DV
    qrow = lax.broadcasted_iota(jnp.int32, (GLA_HEADS * GLA_SUB, 1), 0) % GLA_SUB
    kcol = lax.broadcasted_iota(jnp.int32, (1, blk), 1)
    gi = lax.broadcasted_iota(jnp.int32, (GLA_W, GLA_W), 0) // GLA_DV
    gj = lax.broadcasted_iota(jnp.int32, (GLA_W, GLA_W), 1) // GLA_DV
    gmean = jnp.where(gi == gj, 1.0 / GLA_DV, 0.0).astype(BF16)
    gmean2 = jnp.concatenate([gmean, gmean], axis=0)

    for e in pair:
        a_hi, a_mid, a_lo = _split3(a_ref[e])
        a6 = jnp.where(lane_rep < 3, a_hi, jnp.where(lane_rep < 5, a_mid, a_lo))
        x = _dot(a6, w6_ref[...]) + bup_ref[...]
        g = (jnp.minimum(x, 0.0) - jnp.log1p(jnp.exp(-jnp.abs(x)))) / GLA_TAU
        b = _dot(ltri3, jnp.concatenate(_split3(g), axis=0))
        blast = b[blk - 1:blk, :]
        q = q_ref[e] * (GLA_DK ** -0.5)
        k = k_ref[e]
        v = v_ref[e]
        vb = v.astype(BF16)

        st = st_ref[e]
        o = _dot_nt((q * jnp.exp(b)).astype(BF16), st.astype(BF16))
        khat = k * jnp.exp(blast - b)
        st_ref[e] = st * jnp.exp(blast) + jnp.where(srow == scol, _dot_tn(v, khat), 0.0)

        for c in range(nsub):
            lo = c * GLA_SUB
            ref_b = b[lo:lo + 1, :]
            qt = q[lo:lo + GLA_SUB] * jnp.exp(b[lo:lo + GLA_SUB] - ref_b)
            kt = k * jnp.exp(jnp.where(krow < lo + GLA_SUB, ref_b - b, 0.0))
            qs = jnp.concatenate([jnp.where(lane_qk == h, qt, 0.0) for h in range(GLA_HEADS)], axis=0)
            a = _dot_nt(qs.astype(BF16), kt.astype(BF16))
            a = jnp.where(kcol <= lo + qrow, a, 0.0)
            r = _dot(a.astype(BF16), vb)
            oi = sum(jnp.where(lane_v == h, r[h * GLA_SUB:(h + 1) * GLA_SUB], 0.0) for h in range(GLA_HEADS))
            o_ref[e, lo:lo + GLA_SUB, :] = o[lo:lo + GLA_SUB] + oi

    for e in pair:
        o = o_ref[e]
        oo = o * o
        oo_hi = oo.astype(BF16)
        oo_lo = (oo - oo_hi.astype(F32)).astype(BF16)
        ms = _dot(jnp.concatenate([oo_hi, oo_lo], axis=1), gmean2)
        rr = r_ref[e]
        o_ref[e] = o * lax.rsqrt(ms + NORM_EPS) * ng_ref[...] * (rr * jax.nn.sigmoid(rr))


def _prep_gla_wup(w_up):
    hi = w_up.astype(BF16)
    r1 = w_up - hi.astype(F32)
    mid = r1.astype(BF16)
    lo = (r1 - mid.astype(F32)).astype(BF16)
    w6 = jnp.concatenate([hi, mid, lo, hi, mid, hi], axis=0)
    return jnp.pad(w6, ((0, LANES - w6.shape[0]), (0, 0)))


def _gla(gq, gk, gv, ga, gr, w6, bup, ng, batch, seq):
    blk = min(GLA_BLK, seq)
    nb = seq // blk
    npair = GLA_PAIR if batch % GLA_PAIR == 0 else 1
    nbh = batch // npair
    tok = lambda a: a.reshape(npair, nbh * seq, a.shape[-1])
    spec = lambda w: pl.BlockSpec((npair, blk, w), lambda bh, j: (0, bh * nb + j, 0))
    full = lambda a: pl.BlockSpec(a.shape, lambda bh, j: (0,) * a.ndim)
    out = pl.pallas_call(
        _gla_kernel,
        grid=(nbh, nb),
        in_specs=[spec(GLA_QK_W), spec(GLA_QK_W), spec(GLA_W), spec(LANES), spec(GLA_W),
                  full(w6), full(bup), full(ng)],
        out_specs=spec(GLA_W),
        out_shape=jax.ShapeDtypeStruct((npair, nbh * seq, GLA_W), F32),
        scratch_shapes=[pltpu.VMEM((npair, GLA_W, GLA_QK_W), F32)],
        compiler_params=_cparams(("parallel", "arbitrary")),
        name="gla",
    )(tok(gq), tok(gk), tok(gv), tok(ga), tok(gr), w6, bup, ng)
    return out.reshape(batch * seq, GLA_W)


def _outffn_kernel(x_ref, on_ref, og_ref, os_ref, wo_ref, fg_ref, wg_ref, wu_ref, wd_ref,
                   fin_ref, o_ref, *, final):
    x = x_ref[...]
    x = x + _dot(on_ref[...].astype(BF16), wo_ref[0:NSA_W, :])
    x = x + _dot(og_ref[...].astype(BF16), wo_ref[NSA_W:NSA_W + GLA_W, :])
    x = x + _dot(os_ref[...].astype(BF16), wo_ref[NSA_W + GLA_W:, :])
    hn = (x * lax.rsqrt(jnp.mean(x * x, axis=-1, keepdims=True) + NORM_EPS) * fg_ref[...]).astype(BF16)
    gt = _dot(hn, wg_ref[...])
    up = _dot(hn, wu_ref[...])
    y = x + _dot((gt * jax.nn.sigmoid(gt) * up).astype(BF16), wd_ref[...])
    if final:
        y = y * lax.rsqrt(jnp.mean(y * y, axis=-1, keepdims=True) + NORM_EPS) * fin_ref[...]
    o_ref[...] = y


def _outffn(x2, o_nsa, o_gla, o_sg, wo, fgain, wg, wu, wd, fin, tm, final):
    T, D = x2.shape
    row = lambda w: pl.BlockSpec((tm, w), lambda i: (i, 0))
    const = lambda a: pl.BlockSpec(a.shape, lambda i: (0,) * a.ndim, pipeline_mode=pl.Buffered(1))
    return pl.pallas_call(
        functools.partial(_outffn_kernel, final=final),
        grid=(T // tm,),
        in_specs=[row(D), row(NSA_W), row(GLA_W), row(SG_W), const(wo), const(fgain),
                  const(wg), const(wu), const(wd), const(fin)],
        out_specs=row(D),
        out_shape=jax.ShapeDtypeStruct((T, D), F32),
        compiler_params=_cparams(("parallel",)),
        name="outproj_ffn",
    )(x2, o_nsa, o_gla, o_sg, wo, fgain, wg, wu, wd, fin)


def kernel(x, attn_norm, w_in, cmp_pos_k, cmp_w1_k, cmp_w2_k, cmp_pos_v, cmp_w1_v, cmp_w2_v,
           gla_w_up, gla_b_up, gla_norm, sg_ln_g, sg_ln_b, sg_w, sg_b, w_out,
           ffn_norm, w_gate, w_up, w_down, final_norm):
    batch, seq, d_model = x.shape
    depth = w_in.shape[0]
    T = batch * seq
    assert seq % GLA_BLK == 0 and seq >= WIN_KEYS and seq % SEL_CH == 0
    tm_in = tm_ffn = min(ROW_TILE, seq)

    cos_t, sin_t = _rope_tables(jnp.arange(seq))
    nrow = seq // CMP_STRIDE
    cos_c, sin_c = _rope_tables(jnp.arange(nrow) * CMP_STRIDE + CMP_LEN - 1)
    nsa_consts = _nsa_constants(seq)

    x2 = x.reshape(T, d_model)
    for l in range(depth):
        sg_params = (sg_ln_g[l][None, :], sg_ln_b[l][None, :], sg_w[l],
                     jnp.repeat(sg_b[l].T, SG_CH, axis=1))
        (o_sg, qpad, kc, vc, ksl, vsl, kwn, vwn, gts, gq, gk, gv, ga, gr) = _inproj(
            x2, attn_norm[l][None, :], _prep_w_in(w_in[l]), cos_t, sin_t, sg_params, seq, tm_in)
        kcmp, vcmp = _compress(
            kc, vc, _prep_cmp_weights(cmp_w1_k[l], cmp_w2_k[l], cmp_pos_k[l]),
            _prep_cmp_weights(cmp_w1_v[l], cmp_w2_v[l], cmp_pos_v[l]), cos_c, sin_c, batch, seq)
        o_nsa = _nsa(qpad, gts, kcmp, vcmp, ksl, vsl, kwn, vwn, nsa_consts, batch, seq)
        o_gla = _gla(gq, gk, gv, ga, gr, _prep_gla_wup(gla_w_up[l]), gla_b_up[l][None, :],
                     jnp.tile(gla_norm[l], GLA_HEADS)[None, :], batch, seq)
        x2 = _outffn(x2, o_nsa, o_gla, o_sg, w_out[l].astype(BF16), ffn_norm[l][None, :],
                     w_gate[l].astype(BF16), w_up[l].astype(BF16), w_down[l].astype(BF16),
                     final_norm[None, :], tm_ffn, final=(l == depth - 1))
    return x2.reshape(batch, seq, d_model)
```

```python
import functools
import math

import numpy as np
import jax
import jax.numpy as jnp
from jax import lax
from jax.experimental import pallas as pl
from jax.experimental.pallas import tpu as pltpu

HEAD_DIM = 64
NSA_HEADS = 8
NSA_KV_GROUPS = 2
NSA_HPG = NSA_HEADS // NSA_KV_GROUPS
CMP_LEN = 32
CMP_STRIDE = 16
CMP_HIDDEN = 256
SEL_BLOCK = 64
SEL_TOPK = 16
WINDOW = 512
GLA_HEADS = 4
GLA_DK = 32
GLA_DV = 64
GLA_RANK = 16
GLA_TAU = 16.0
SG_GROUPS = 4
SG_CH = 64
SG_CHUNK = 128
NSA_W = NSA_HEADS * HEAD_DIM
GLA_W = GLA_HEADS * GLA_DV
SG_W = SG_GROUPS * SG_CH
KV_W = NSA_KV_GROUPS * HEAD_DIM
IN_SIZES = (NSA_W, KV_W, KV_W, KV_W, KV_W, KV_W, KV_W, NSA_HEADS * 3,
            GLA_HEADS * GLA_DK, GLA_HEADS * GLA_DK, GLA_W, GLA_RANK, GLA_W, 2 * SG_W)
ROPE_THETA = 10000.0
NORM_EPS = 1e-6

LANES = 128
NEG_BIG = -1e30
VMEM_LIMIT = 56 * 1024 * 1024

QPAD_W = NSA_HEADS * LANES
GEXP_W = 3 * NSA_W
GLA_QK_W = GLA_HEADS * GLA_DK
GLA_BLK = 256
GLA_SUB = 16
SEL_CH = 1024
WIN_KEYS = WINDOW + 2 * SEL_BLOCK
NSA_ROW_PARTS = 1
NSA_STEP_BLOCKS = 4
RANK_UNROLL = 4
NSA_PAIR = 2
GLA_PAIR = 4

ROW_TILE = 512
GLA_A_COPIES = 6
F32 = jnp.float32
BF16 = jnp.bfloat16


def _cparams(sem):
    return pltpu.CompilerParams(dimension_semantics=sem, vmem_limit_bytes=VMEM_LIMIT)


def _dot(a, b, precision=None):
    return jnp.dot(a, b, preferred_element_type=F32, precision=precision)


def _dot_nt(a, b, precision=None):
    return lax.dot_general(a, b, (((1,), (1,)), ((), ())),
                           preferred_element_type=F32, precision=precision)


def _dot_tn(a, b, precision=None):
    return lax.dot_general(a, b, (((0,), (0,)), ((), ())),
                           preferred_element_type=F32, precision=precision)


def _gelu_tanh(x):
    c = math.sqrt(2.0 / math.pi)
    return 0.5 * x * (1.0 + jnp.tanh(c * (x + 0.044715 * (x * x * x))))


def _rope_lanes(x, cos, sin_signed):
    n = x.shape[-1]
    lane = lax.broadcasted_iota(jnp.int32, x.shape, 1)
    first_half = (lane % HEAD_DIM) < (HEAD_DIM // 2)
    partner = jnp.where(first_half,
                        pltpu.roll(x, n - HEAD_DIM // 2, 1),
                        pltpu.roll(x, HEAD_DIM // 2, 1))
    return x * cos + partner * sin_signed


_INPROJ_OUT = (
    ("sg", SG_W, F32),
    ("qpad", QPAD_W, BF16), ("kc", KV_W, F32), ("vc", KV_W, F32),
    ("ksl", KV_W, BF16), ("vsl", KV_W, BF16), ("kwn", KV_W, BF16), ("vwn", KV_W, BF16),
    ("gts", LANES, F32), ("gq", GLA_QK_W, F32), ("gk", GLA_QK_W, F32),
    ("gv", GLA_W, F32), ("ga", LANES, F32), ("gr", GLA_W, F32))
_INPROJ_W = sum(w for _, w, _ in _INPROJ_OUT) + SG_W


def _spatial_gate(uv, lg, lb, wts, bias, lane_g):
    a = _gelu_tanh(uv)
    u = a[:, :SG_W]
    v = a[:, SG_W:]
    mu = jnp.mean(v, axis=-1, keepdims=True)
    var = jnp.mean(jnp.square(v - mu), axis=-1, keepdims=True)
    vn = ((v - mu) * lax.rsqrt(var + NORM_EPS) * lg + lb).astype(BF16)
    s = bias
    for g in range(SG_GROUPS):
        s = s + jnp.where(lane_g == g, _dot(wts[g], vn), 0.0)
    return u * s


def _inproj_kernel(x_ref, g_ref, w_ref, cos_ref, sin_ref, lg_ref, lb_ref, ws_ref, sb_ref, *refs):
    out_refs, grp_scr = refs[:-1], refs[-1]
    x = x_ref[...]
    hn = x * lax.rsqrt(jnp.mean(x * x, axis=-1, keepdims=True) + NORM_EPS) * g_ref[...]
    z = _dot(hn.astype(BF16), w_ref[...])
    cos = cos_ref[...]
    sin = sin_ref[...]
    off = 0
    for (name, width, dtype), o_ref in zip(_INPROJ_OUT, out_refs):
        if name in ("qpad", "ksl", "kwn"):
            for j in range(width // LANES):
                blk = _rope_lanes(z[:, off + j * LANES: off + (j + 1) * LANES], cos, sin)
                if name == "qpad":
                    blk = blk * (HEAD_DIM ** -0.5)
                o_ref[:, j * LANES:(j + 1) * LANES] = blk.astype(dtype)
        elif name in ("kc", "vc"):
            grp_scr[...] = z[:, off:off + width]
            for r in range(CMP_STRIDE):
                o_ref[:, r * width:(r + 1) * width] = grp_scr[pl.ds(r, x.shape[0] // CMP_STRIDE, stride=CMP_STRIDE), :]
        elif name == "sg":
            n = ws_ref.shape[1]
            ri = lax.broadcasted_iota(jnp.int32, (n, n), 0)
            ci = lax.broadcasted_iota(jnp.int32, (n, n), 1)
            lane_g = lax.broadcasted_iota(jnp.int32, (1, SG_W), 1) // SG_CH
            wts = [jnp.where(ci <= ri, ws_ref[g], 0.0).astype(BF16) for g in range(SG_GROUPS)]
            for c in range(x.shape[0] // n):
                rs = slice(c * n, (c + 1) * n)
                o_ref[rs, :] = _spatial_gate(z[rs, off:off + 2 * SG_W], lg_ref[...], lb_ref[...], wts,
                                             sb_ref[...], lane_g)
        else:
            o_ref[...] = z[:, off:off + width].astype(dtype)
        off += 2 * SG_W if name == "sg" else width


def _inproj(x2, gain, w_p, cos_t, sin_t, sg_params, seq, tm):
    T, D = x2.shape
    nper = seq // tm
    assert tm % SG_CHUNK == 0
    regroup = lambda n: CMP_STRIDE if n in ("kc", "vc") else 1
    out_shape = [jax.ShapeDtypeStruct((T // regroup(n), w * regroup(n)), dt) for n, w, dt in _INPROJ_OUT]
    out_specs = [pl.BlockSpec((tm // regroup(n), w * regroup(n)), lambda i: (i, 0)) for n, w, _ in _INPROJ_OUT]
    full = lambda a: pl.BlockSpec(a.shape, lambda i: (0,) * a.ndim)
    return pl.pallas_call(
        _inproj_kernel,
        grid=(T // tm,),
        in_specs=[
            pl.BlockSpec((tm, D), lambda i: (i, 0)),
            pl.BlockSpec((1, D), lambda i: (0, 0)),
            pl.BlockSpec((D, _INPROJ_W), lambda i: (0, 0), pipeline_mode=pl.Buffered(1)),
            pl.BlockSpec((tm, LANES), lambda i: (i % nper, 0)),
            pl.BlockSpec((tm, LANES), lambda i: (i % nper, 0)),
        ] + [full(a) for a in sg_params],
        out_specs=out_specs,
        out_shape=out_shape,
        scratch_shapes=[pltpu.VMEM((tm, KV_W), F32)],
        compiler_params=_cparams(("parallel",)),
        name="inproj",
    )(x2, gain, w_p, cos_t, sin_t, *sg_params)


def _prep_w_in(w_in):
    offs = np.cumsum((0,) + IN_SIZES)
    seg = [w_in[:, offs[k]:offs[k + 1]] for k in range(len(IN_SIZES))]
    (wq, wkc, wvc, wksl, wvsl, wkwn, wvwn, wg, wgq, wgk, wgv, wga, wgr, wuv) = seg
    D = w_in.shape[0]
    zeros64 = jnp.zeros((D, HEAD_DIM), w_in.dtype)
    qcols = []
    for h in range(NSA_HEADS):
        wh = wq[:, h * HEAD_DIM:(h + 1) * HEAD_DIM]
        qcols += [wh, zeros64] if h // NSA_HPG == 0 else [zeros64, wh]
    gcols = [jnp.pad(wg, ((0, 0), (0, LANES - wg.shape[1])))]
    wga_p = jnp.pad(jnp.concatenate([wga] * GLA_A_COPIES, axis=1), ((0, 0), (0, LANES - GLA_A_COPIES * GLA_RANK)))
    return jnp.concatenate([wuv] + qcols + [wkc, wvc, wksl, wvsl, wkwn, wvwn] + gcols
                           + [wgq, wgk, wgv, wga_p, wgr], axis=1).astype(BF16)


def _rope_tables(pos):
    half = HEAD_DIM // 2
    inv = 1.0 / (ROPE_THETA ** (jnp.arange(half, dtype=F32) / half))
    ang = pos.astype(F32)[:, None] * inv[None, :]
    cos = jnp.cos(ang)
    sin = jnp.sin(ang)
    cos_t = jnp.concatenate([cos, cos, cos, cos], axis=1)
    sin_t = jnp.concatenate([-sin, sin, -sin, sin], axis=1)
    return cos_t, sin_t


def _cmp_kernel(xk_ref, xv_ref, wek_ref, wev_ref, w1k_ref, w1v_ref, pk_ref, pv_ref,
                w2k_ref, w2v_ref, cos_ref, sin_ref, ok_ref, ov_ref):
    nrow = xk_ref.shape[1]
    row = lax.broadcasted_iota(jnp.int32, (nrow, LANES), 0)

    def compress(x_ref, we_ref, w1_ref, p_ref, w2_ref):
        h = _dot(x_ref[0].astype(BF16), we_ref[...])
        posb = _dot(p_ref[...].astype(BF16), w1_ref[...])[0:1, :]
        y = jnp.zeros((nrow, LANES), F32)
        for g in range(NSA_KV_GROUPS):
            a = h[:, g * CMP_HIDDEN:(g + 1) * CMP_HIDDEN]
            b = h[:, (NSA_KV_GROUPS + g) * CMP_HIDDEN:(NSA_KV_GROUPS + g + 1) * CMP_HIDDEN]
            hid = a + pltpu.roll(b, nrow - 1, 0) + posb
            y = y + _dot(_gelu_tanh(hid).astype(BF16), w2_ref[g])
        return y

    yk = _rope_lanes(compress(xk_ref, wek_ref, w1k_ref, pk_ref, w2k_ref), cos_ref[...], sin_ref[...])
    yv = compress(xv_ref, wev_ref, w1v_ref, pv_ref, w2v_ref)
    keep = row < nrow - 1
    ok_ref[0] = jnp.where(keep, yk, 0.0)
    ov_ref[0] = jnp.where(keep, yv, 0.0)


def _prep_cmp_weights(w1, w2, pos):
    half = CMP_LEN // CMP_STRIDE
    H = w1.shape[1]
    w1r = w1.reshape(half, CMP_STRIDE, HEAD_DIM, H)
    cols = []
    for a in range(half):
        for g in range(NSA_KV_GROUPS):
            blk = jnp.zeros((CMP_STRIDE, NSA_KV_GROUPS, HEAD_DIM, H), w1.dtype)
            blk = blk.at[:, g].set(w1r[a])
            cols.append(blk.reshape(CMP_STRIDE * KV_W, H))
    wexp = jnp.concatenate(cols, axis=1).astype(BF16)
    w2p = jnp.stack([jnp.pad(w2, ((0, 0), (g * HEAD_DIM, KV_W - (g + 1) * HEAD_DIM)))
                     for g in range(NSA_KV_GROUPS)]).astype(BF16)
    posf = jnp.pad(pos.reshape(1, CMP_LEN * HEAD_DIM), ((0, 7), (0, 0)))
    return wexp, w1.astype(BF16), posf, w2p


def _compress(kc, vc, wk, wv, cos_c, sin_c, batch, seq):
    nrow = seq // CMP_STRIDE
    xk = kc.reshape(batch, nrow, CMP_STRIDE * KV_W)
    xv = vc.reshape(batch, nrow, CMP_STRIDE * KV_W)
    wek, w1k, pk, w2k = wk
    wev, w1v, pv, w2v = wv
    full = lambda a: pl.BlockSpec(a.shape, lambda b: (0,) * a.ndim)
    xspec = pl.BlockSpec((1, nrow, CMP_STRIDE * KV_W), lambda b: (b, 0, 0))
    ospec = pl.BlockSpec((1, nrow, KV_W), lambda b: (b, 0, 0))
    return pl.pallas_call(
        _cmp_kernel,
        grid=(batch,),
        in_specs=[xspec, xspec, full(wek), full(wev), full(w1k), full(w1v), full(pk), full(pv),
                  full(w2k), full(w2v), full(cos_c), full(sin_c)],
        out_specs=[ospec, ospec],
        out_shape=[jax.ShapeDtypeStruct((batch, nrow, KV_W), F32)] * 2,
        compiler_params=_cparams(("parallel",)),
        name="compress",
    )(xk, xv, wek, wev, w1k, w1v, pk, pv, w2k, w2v, cos_c, sin_c)


def _unpad_heads(acc):
    qb = SEL_BLOCK
    lane = lax.broadcasted_iota(jnp.int32, (qb, LANES), 1)
    low = lane < HEAD_DIM
    outs = []
    for j in range(NSA_HEADS // 2):
        a = acc[(2 * j) * qb:(2 * j + 1) * qb]
        b = acc[(2 * j + 1) * qb:(2 * j + 2) * qb]
        if (2 * j) // NSA_HPG == 0:
            outs.append(jnp.where(low, a, pltpu.roll(b, HEAD_DIM, 1)))
        else:
            outs.append(jnp.where(low, pltpu.roll(a, HEAD_DIM, 1), b))
    return jnp.concatenate(outs, axis=1)


def _nsa_kernel(*refs):
    for j in range(NSA_STEP_BLOCKS):
        _nsa_block(j, *refs)


def _nsa_block(j, q_ref, g_ref, kc_ref, vc_ref, ksl_ref, vsl_ref, kwn_ref, vwn_ref,
               ktab_ref, kdiag_ref, kwtab_ref, qc_ref, qw_ref, eye_ref, gexp_ref, impt_ref, o_ref,
               imp_scr, m_scr, acc_scr, win_scr, kd_scr, qs_scr, sc_scr, cmp_scr):
    qb = SEL_BLOCK
    rows = NSA_HEADS * qb
    pair = range(q_ref.shape[0])
    i = pl.program_id(1) * NSA_STEP_BLOCKS + j
    tr = slice(j * qb, (j + 1) * qb)
    t = i * qb + lax.broadcasted_iota(jnp.int32, (rows, 1), 0) % qb
    ones = jnp.ones((SEL_CH, LANES), BF16)
    parts = [slice(p * rows // NSA_ROW_PARTS, (p + 1) * rows // NSA_ROW_PARTS) for p in range(NSA_ROW_PARTS)]
    qbf = [jnp.concatenate([q_ref[e, tr, h * LANES:(h + 1) * LANES] for h in range(NSA_HEADS)], axis=0)
           for e in pair]

    nblk_w = WIN_KEYS // qb
    ib = jnp.minimum(i, nblk_w - 1)
    w0 = pl.multiple_of((i - ib) * qb, qb)
    toff = pl.multiple_of((nblk_w - 1 - ib) * qb, qb)
    kwmask = kwtab_ref[pl.ds(toff, WIN_KEYS), :]
    for e in pair:
        kw = jnp.concatenate([kwn_ref[e, 0, pl.ds(w0, WIN_KEYS), :], kwmask], axis=1)
        vw = jnp.concatenate([vwn_ref[e, 0, pl.ds(w0, WIN_KEYS), :], ones[0:WIN_KEYS]], axis=1)
        q_win = jnp.concatenate([qbf[e], qw_ref[...]], axis=1)
        for rs in parts:
            sw = _dot_nt(q_win[rs], kw)
            ew = jnp.exp(sw - jnp.max(sw, axis=1, keepdims=True)).astype(BF16)
            ow = _dot(ew, vw)
            win_scr[e, rs, :] = ow[:, 0:LANES] / ow[:, LANES:2 * LANES]

    ncmp = kc_ref.shape[2]
    cend = lax.broadcasted_iota(jnp.int32, (1, ncmp), 1) * CMP_STRIDE + (CMP_LEN - 1)
    vis = cend <= t
    imap = impt_ref[...].astype(BF16)
    imap2 = jnp.concatenate([imap, imap], axis=1)
    for e in pair:
        s = _dot_nt(qbf[e], kc_ref[e, 0].astype(BF16))
        sm = jnp.where(vis, s, NEG_BIG)
        mx = jnp.max(sm, axis=1, keepdims=True)
        ex = jnp.where(vis, jnp.exp(sm - mx), 0.0)
        d = jnp.sum(ex, axis=1, keepdims=True)
        p = ex / jnp.where(d > 0, d, 1.0)
        cmp_scr[e] = _dot(p.astype(BF16), vc_ref[e, 0].astype(BF16))
        psum = jnp.concatenate(
            [sum(p[(g * NSA_HPG + h) * qb:(g * NSA_HPG + h + 1) * qb] for h in range(NSA_HPG))
             for g in range(NSA_KV_GROUPS)], axis=0)
        p_hi = psum.astype(BF16)
        p_lo = (psum - p_hi.astype(F32)).astype(BF16)
        imp_scr[e] = _dot_nt(imap2, jnp.concatenate([p_hi, p_lo], axis=1))

    nsbp = imp_scr.shape[1]
    imps = [imp_scr[e] for e in pair]
    n_id = lax.broadcasted_iota(jnp.int32, (nsbp, 1), 0)
    valid = n_id <= i
    forced = (n_id == 0) | (n_id == i) | (n_id == i - 1)
    nforced = 1 + (i >= 1).astype(jnp.int32) + (i >= 2).astype(jnp.int32)

    def rank_body(j, cnts):
        cnts = list(cnts)
        for u in range(RANK_UNROLL):
            m = 1 + RANK_UNROLL * j + u
            tie = jnp.where(m < n_id, 1.0, 0.0)
            for e in pair:
                rowm = imp_scr[e, pl.ds(jnp.minimum(m, nsbp - 1), 1), :]
                rowm = jnp.where(m <= i - 2, rowm, NEG_BIG)
                cnts[e] = cnts[e] + jnp.where(rowm > imps[e], 1.0, 0.0) + jnp.where(rowm == imps[e], tie, 0.0)
        return tuple(cnts)

    ntrip = (jnp.maximum(i - 2, 0) + RANK_UNROLL - 1) // RANK_UNROLL
    cnts = lax.fori_loop(0, ntrip, rank_body, tuple(jnp.zeros(imps[0].shape, F32) for _ in pair))
    budget = (SEL_TOPK - nforced).astype(F32)

    for e in pair:
        sel = valid & (forced | (cnts[e] < budget))
        notsel_t = jnp.where(sel, 0.0, 1.0)
        notsel = _dot_tn(notsel_t, eye_ref[...])
        notsel = jnp.concatenate([notsel[g * qb:(g + 1) * qb]
                                  for g in range(NSA_KV_GROUPS) for _ in range(NSA_HPG)], axis=0)
        qs_scr[e] = jnp.concatenate([qbf[e], (notsel + qc_ref[...]).astype(BF16)], axis=1)
    m_scr[...] = jnp.full(m_scr.shape, NEG_BIG, F32)
    acc_scr[...] = jnp.zeros(acc_scr.shape, F32)
    half = SEL_CH // 2
    nhalf = i // (half // qb) + 1
    lead = nhalf % 2
    nfull = nhalf // 2
    base = lead * half
    k0_last = pl.multiple_of(jnp.maximum(base + (nfull - 1) * SEL_CH, 0), half)
    off = pl.multiple_of(i * qb - k0_last, qb)
    kd_scr[...] = ktab_ref[pl.ds(k0_last, SEL_CH), :]
    kd_scr[pl.ds(off, qb), :] = kd_scr[pl.ds(off, qb), :] + kdiag_ref[...]

    def scores(slot, k0, size, kmask):
        for e in pair:
            k = jnp.concatenate([ksl_ref[e, 0, pl.ds(k0, size), :], kmask], axis=1)
            for rs in parts:
                sc = _dot_nt(qs_scr[e, rs, :], k)
                sc_scr[e, slot, rs, 0:size] = sc
                mx = m_scr[e, rs, :]
                for j in range(size // LANES):
                    mx = jnp.maximum(mx, sc[:, j * LANES:(j + 1) * LANES])
                m_scr[e, rs, :] = mx

    def values(slot, k0, size):
        for e in pair:
            v = jnp.concatenate([vsl_ref[e, 0, pl.ds(k0, size), :], ones[0:size]], axis=1)
            for rs in parts:
                mb = m_scr[e, rs, :]
                pr = jnp.concatenate([jnp.exp(sc_scr[e, slot, rs, j * LANES:(j + 1) * LANES] - mb)
                                      for j in range(size // LANES)], axis=1).astype(BF16)
                acc_scr[e, rs, :] += _dot(pr, v)

    @pl.when(lead == 1)
    def _():
        scores(0, 0, half, jnp.where(nfull == 0, kd_scr[0:half, :], ktab_ref[0:half, :]))

    def score_body(c, carry):
        k0 = pl.multiple_of(base + c * SEL_CH, half)
        scores(1 + c, k0, SEL_CH, jnp.where(c == nfull - 1, kd_scr[...], ktab_ref[pl.ds(k0, SEL_CH), :]))
        return carry

    lax.fori_loop(0, nfull, score_body, 0)
    for e in pair:
        m_scr[e] = jnp.broadcast_to(jnp.max(m_scr[e], axis=1, keepdims=True), (rows, LANES))

    @pl.when(lead == 1)
    def _():
        values(0, 0, half)

    def value_body(c, carry):
        values(1 + c, pl.multiple_of(base + c * SEL_CH, half), SEL_CH)
        return carry

    lax.fori_loop(0, nfull, value_body, 0)

    for e in pair:
        o_slc = acc_scr[e, :, 0:LANES] / acc_scr[e, :, LANES:2 * LANES]
        g_hi = jax.nn.sigmoid(g_ref[e, tr, :])
        g_lo = g_hi - g_hi.astype(BF16).astype(F32)
        gate = _dot(jnp.concatenate([g_hi.astype(BF16), g_lo.astype(BF16)], axis=1), gexp_ref[...])
        o_ref[e, tr, :] = (gate[:, 0:NSA_W] * _unpad_heads(cmp_scr[e])
                    + gate[:, NSA_W:2 * NSA_W] * _unpad_heads(o_slc)
                    + gate[:, 2 * NSA_W:3 * NSA_W] * _unpad_heads(win_scr[e]))


def _nsa(qpad, gts, kcmp, vcmp, ksl, vsl, kwn, vwn, consts, batch, seq):
    qb = SEL_BLOCK
    nq = seq // qb
    rows = NSA_HEADS * qb
    npair = NSA_PAIR if batch % NSA_PAIR == 0 else 1
    nbh = batch // npair
    tok = lambda a: a.reshape(npair, nbh * seq, a.shape[-1])
    per = lambda a: a.reshape(npair, nbh, a.shape[-2], KV_W)
    assert nq % NSA_STEP_BLOCKS == 0
    nst = nq // NSA_STEP_BLOCKS
    tok_spec = lambda w: pl.BlockSpec((npair, NSA_STEP_BLOCKS * qb, w), lambda bh, i: (0, bh * nst + i, 0))
    per_spec = lambda n: pl.BlockSpec((npair, 1, n, KV_W), lambda bh, i: (0, bh, 0, 0))
    full = lambda a: pl.BlockSpec(a.shape, lambda bh, i: (0,) * a.ndim)
    out = pl.pallas_call(
        _nsa_kernel,
        grid=(nbh, nst),
        in_specs=[
            tok_spec(QPAD_W), tok_spec(LANES),
            per_spec(kcmp.shape[1]), per_spec(vcmp.shape[1]),
            per_spec(seq), per_spec(seq), per_spec(seq), per_spec(seq),
        ] + [full(c) for c in consts],
        out_specs=tok_spec(NSA_W),
        out_shape=jax.ShapeDtypeStruct((npair, nbh * seq, NSA_W), F32),
        scratch_shapes=[
            pltpu.VMEM((npair, consts[-1].shape[0], NSA_KV_GROUPS * qb), F32),
            pltpu.VMEM((npair, rows, LANES), F32),
            pltpu.VMEM((npair, rows, 2 * LANES), F32),
            pltpu.VMEM((npair, rows, LANES), F32),
            pltpu.VMEM((SEL_CH, LANES), BF16),
            pltpu.VMEM((npair, rows, 2 * LANES), BF16),
            pltpu.VMEM((npair, seq // SEL_CH + 1, rows, SEL_CH), F32),
            pltpu.VMEM((npair, rows, LANES), F32),
        ],
        compiler_params=_cparams(("parallel", "arbitrary")),
        name="nsa",
    )(tok(qpad), tok(gts), per(kcmp), per(vcmp), per(ksl.reshape(batch, seq, KV_W)),
      per(vsl.reshape(batch, seq, KV_W)), per(kwn.reshape(batch, seq, KV_W)),
      per(vwn.reshape(batch, seq, KV_W)), *consts)
    return out.reshape(batch * seq, NSA_W)


def _nsa_constants(seq):
    qb = SEL_BLOCK
    nc_pad = seq // CMP_STRIDE
    nsb = seq // qb
    assert nsb <= qb
    nsb_pad = max(nsb, 8)
    cs = np.arange(nc_pad) * CMP_STRIDE
    ce = cs + CMP_LEN
    bs = np.arange(nsb_pad) * qb
    be = bs + qb
    ov = np.clip(np.minimum(ce[None, :], be[:, None]) - np.maximum(cs[None, :], bs[:, None]), 0, None)
    impt = ov / CMP_LEN
    pos = np.arange(seq)
    lane = np.arange(LANES)[None, :]
    ktab = np.where(lane == pos[:, None] // qb, NEG_BIG, 0.0)
    kdiag = np.where(lane == qb + np.arange(qb)[:, None], NEG_BIG, 0.0)
    last = WIN_KEYS // qb - 1
    x = np.arange(WIN_KEYS + last * qb)[:, None]
    kb, j = x // qb, x % qb
    kwtab = np.where((((kb == 0) | (kb > last)) & (lane == 0)) | ((kb == 1) & (lane == j))
                     | ((kb == last) & (lane == qb + j)), NEG_BIG, 0.0)
    r = (np.arange(NSA_HEADS * qb) % qb)[:, None]
    upper = (lane >= qb) & (lane - qb > r)
    qc = np.where(upper, 1.0, 0.0)
    qw = np.where(upper | ((lane < qb) & (lane <= r)), 1.0, 0.0)
    eye = np.eye(nsb_pad, LANES)
    col = np.arange(GEXP_W)[None, :]
    src = np.arange(LANES)[:, None]
    rep = (src == (col % NSA_W) // HEAD_DIM * 3 + col // NSA_W) & (src < 3 * NSA_HEADS)
    gexp = np.concatenate([rep, rep], axis=0)
    return (jnp.asarray(ktab, BF16), jnp.asarray(kdiag, BF16), jnp.asarray(kwtab, BF16),
            jnp.asarray(qc, F32), jnp.asarray(qw, BF16), jnp.asarray(eye, F32), jnp.asarray(gexp, BF16),
            jnp.asarray(impt, F32))


def _split3(x):
    hi = x.astype(BF16)
    r1 = x - hi.astype(F32)
    mid = r1.astype(BF16)
    lo = (r1 - mid.astype(F32)).astype(BF16)
    return hi, mid, lo


def _gla_kernel(q_ref, k_ref, v_ref, a_ref, r_ref, w6_ref, bup_ref, ng_ref, o_ref, st_ref):
    pair = range(q_ref.shape[0])
    blk = q_ref.shape[1]
    nsub = blk // GLA_SUB

    @pl.when(pl.program_id(1) == 0)
    def _():
        st_ref[...] = jnp.zeros(st_ref.shape, F32)

    ri = lax.broadcasted_iota(jnp.int32, (blk, blk), 0)
    ci = lax.broadcasted_iota(jnp.int32, (blk, blk), 1)
    ltri = jnp.where(ci <= ri, 1.0, 0.0).astype(BF16)
    ltri3 = jnp.concatenate([ltri, ltri, ltri], axis=1)
    lane_rep = lax.broadcasted_iota(jnp.int32, (1, LANES), 1) // GLA_RANK
    srow = lax.broadcasted_iota(jnp.int32, st_ref.shape[1:], 0) // GLA_DV
    scol = lax.broadcasted_iota(jnp.int32, st_ref.shape[1:], 1) // GLA_DK
    krow = lax.broadcasted_iota(jnp.int32, (blk, 1), 0)
    lane_qk = lax.broadcasted_iota(jnp.int32, (1, GLA_QK_W), 1) // GLA_DK
    lane_v = lax.broadcasted_iota(jnp.int32, (1, GLA_W), 1) // GLA_DV
    qrow = lax.broadcasted_iota(jnp.int32, (GLA_HEADS * GLA_SUB, 1), 0) % GLA_SUB
    kcol = lax.broadcasted_iota(jnp.int32, (1, blk), 1)
    gi = lax.broadcasted_iota(jnp.int32, (GLA_W, GLA_W), 0) // GLA_DV
    gj = lax.broadcasted_iota(jnp.int32, (GLA_W, GLA_W), 1) // GLA_DV
    gmean = jnp.where(gi == gj, 1.0 / GLA_DV, 0.0).astype(BF16)
    gmean2 = jnp.concatenate([gmean, gmean], axis=0)

    for e in pair:
        a_hi, a_mid, a_lo = _split3(a_ref[e])
        a6 = jnp.where(lane_rep < 3, a_hi, jnp.where(lane_rep < 5, a_mid, a_lo))
        x = _dot(a6, w6_ref[...]) + bup_ref[...]
        g = (jnp.minimum(x, 0.0) - jnp.log1p(jnp.exp(-jnp.abs(x)))) / GLA_TAU
        b = _dot(ltri3, jnp.concatenate(_split3(g), axis=0))
        blast = b[blk - 1:blk, :]
        q = q_ref[e] * (GLA_DK ** -0.5)
        k = k_ref[e]
        v = v_ref[e]
        vb = v.astype(BF16)

        st = st_ref[e]
        o = _dot_nt((q * jnp.exp(b)).astype(BF16), st.astype(BF16))
        khat = k * jnp.exp(blast - b)
        st_ref[e] = st * jnp.exp(blast) + jnp.where(srow == scol, _dot_tn(v, khat), 0.0)

        for c in range(nsub):
            lo = c * GLA_SUB
            ref_b = b[lo:lo + 1, :]
            qt = q[lo:lo + GLA_SUB] * jnp.exp(b[lo:lo + GLA_SUB] - ref_b)
            kt = k * jnp.exp(jnp.where(krow < lo + GLA_SUB, ref_b - b, 0.0))
            qs = jnp.concatenate([jnp.where(lane_qk == h, qt, 0.0) for h in range(GLA_HEADS)], axis=0)
            a = _dot_nt(qs.astype(BF16), kt.astype(BF16))
            a = jnp.where(kcol <= lo + qrow, a, 0.0)
            r = _dot(a.astype(BF16), vb)
            oi = sum(jnp.where(lane_v == h, r[h * GLA_SUB:(h + 1) * GLA_SUB], 0.0) for h in range(GLA_HEADS))
            o_ref[e, lo:lo + GLA_SUB, :] = o[lo:lo + GLA_SUB] + oi

    for e in pair:
        o = o_ref[e]
        oo = o * o
        oo_hi = oo.astype(BF16)
        oo_lo = (oo - oo_hi.astype(F32)).astype(BF16)
        ms = _dot(jnp.concatenate([oo_hi, oo_lo], axis=1), gmean2)
        rr = r_ref[e]
        o_ref[e] = o * lax.rsqrt(ms + NORM_EPS) * ng_ref[...] * (rr * jax.nn.sigmoid(rr))


def _prep_gla_wup(w_up):
    hi = w_up.astype(BF16)
    r1 = w_up - hi.astype(F32)
    mid = r1.astype(BF16)
    lo = (r1 - mid.astype(F32)).astype(BF16)
    w6 = jnp.concatenate([hi, mid, lo, hi, mid, hi], axis=0)
    return jnp.pad(w6, ((0, LANES - w6.shape[0]), (0, 0)))


def _gla(gq, gk, gv, ga, gr, w6, bup, ng, batch, seq):
    blk = min(GLA_BLK, seq)
    nb = seq // blk
    npair = GLA_PAIR if batch % GLA_PAIR == 0 else 1
    nbh = batch // npair
    tok = lambda a: a.reshape(npair, nbh * seq, a.shape[-1])
    spec = lambda w: pl.BlockSpec((npair, blk, w), lambda bh, j: (0, bh * nb + j, 0))
    full = lambda a: pl.BlockSpec(a.shape, lambda bh, j: (0,) * a.ndim)
    out = pl.pallas_call(
        _gla_kernel,
        grid=(nbh, nb),
        in_specs=[spec(GLA_QK_W), spec(GLA_QK_W), spec(GLA_W), spec(LANES), spec(GLA_W),
                  full(w6), full(bup), full(ng)],
        out_specs=spec(GLA_W),
        out_shape=jax.ShapeDtypeStruct((npair, nbh * seq, GLA_W), F32),
        scratch_shapes=[pltpu.VMEM((npair, GLA_W, GLA_QK_W), F32)],
        compiler_params=_cparams(("parallel", "arbitrary")),
        name="gla",
    )(tok(gq), tok(gk), tok(gv), tok(ga), tok(gr), w6, bup, ng)
    return out.reshape(batch * seq, GLA_W)


def _outffn_kernel(x_ref, on_ref, og_ref, os_ref, wo_ref, fg_ref, wg_ref, wu_ref, wd_ref,
                   fin_ref, o_ref, *, final):
    x = x_ref[...]
    x = x + _dot(on_ref[...].astype(BF16), wo_ref[0:NSA_W, :])
    x = x + _dot(og_ref[...].astype(BF16), wo_ref[NSA_W:NSA_W + GLA_W, :])
    x = x + _dot(os_ref[...].astype(BF16), wo_ref[NSA_W + GLA_W:, :])
    hn = (x * lax.rsqrt(jnp.mean(x * x, axis=-1, keepdims=True) + NORM_EPS) * fg_ref[...]).astype(BF16)
    gt = _dot(hn, wg_ref[...])
    up = _dot(hn, wu_ref[...])
    y = x + _dot((gt * jax.nn.sigmoid(gt) * up).astype(BF16), wd_ref[...])
    if final:
        y = y * lax.rsqrt(jnp.mean(y * y, axis=-1, keepdims=True) + NORM_EPS) * fin_ref[...]
    o_ref[...] = y


def _outffn(x2, o_nsa, o_gla, o_sg, wo, fgain, wg, wu, wd, fin, tm, final):
    T, D = x2.shape
    row = lambda w: pl.BlockSpec((tm, w), lambda i: (i, 0))
    const = lambda a: pl.BlockSpec(a.shape, lambda i: (0,) * a.ndim, pipeline_mode=pl.Buffered(1))
    return pl.pallas_call(
        functools.partial(_outffn_kernel, final=final),
        grid=(T // tm,),
        in_specs=[row(D), row(NSA_W), row(GLA_W), row(SG_W), const(wo), const(fgain),
                  const(wg), const(wu), const(wd), const(fin)],
        out_specs=row(D),
        out_shape=jax.ShapeDtypeStruct((T, D), F32),
        compiler_params=_cparams(("parallel",)),
        name="outproj_ffn",
    )(x2, o_nsa, o_gla, o_sg, wo, fgain, wg, wu, wd, fin)


def kernel(x, attn_norm, w_in, cmp_pos_k, cmp_w1_k, cmp_w2_k, cmp_pos_v, cmp_w1_v, cmp_w2_v,
           gla_w_up, gla_b_up, gla_norm, sg_ln_g, sg_ln_b, sg_w, sg_b, w_out,
           ffn_norm, w_gate, w_up, w_down, final_norm):
    batch, seq, d_model = x.shape
    depth = w_in.shape[0]
    T = batch * seq
    assert seq % GLA_BLK == 0 and seq >= WIN_KEYS and seq % SEL_CH == 0
    tm_in = tm_ffn = min(ROW_TILE, seq)

    cos_t, sin_t = _rope_tables(jnp.arange(seq))
    nrow = seq // CMP_STRIDE
    cos_c, sin_c = _rope_tables(jnp.arange(nrow) * CMP_STRIDE + CMP_LEN - 1)
    nsa_consts = _nsa_constants(seq)

    x2 = x.reshape(T, d_model)
    for l in range(depth):
        sg_params = (sg_ln_g[l][None, :], sg_ln_b[l][None, :], sg_w[l],
                     jnp.repeat(sg_b[l].T, SG_CH, axis=1))
        (o_sg, qpad, kc, vc, ksl, vsl, kwn, vwn, gts, gq, gk, gv, ga, gr) = _inproj(
            x2, attn_norm[l][None, :], _prep_w_in(w_in[l]), cos_t, sin_t, sg_params, seq, tm_in)
        kcmp, vcmp = _compress(
            kc, vc, _prep_cmp_weights(cmp_w1_k[l], cmp_w2_k[l], cmp_pos_k[l]),
            _prep_cmp_weights(cmp_w1_v[l], cmp_w2_v[l], cmp_pos_v[l]), cos_c, sin_c, batch, seq)
        o_nsa = _nsa(qpad, gts, kcmp, vcmp, ksl, vsl, kwn, vwn, nsa_consts, batch, seq)
        o_gla = _gla(gq, gk, gv, ga, gr, _prep_gla_wup(gla_w_up[l]), gla_b_up[l][None, :],
                     jnp.tile(gla_norm[l], GLA_HEADS)[None, :], batch, seq)
        x2 = _outffn(x2, o_nsa, o_gla, o_sg, w_out[l].astype(BF16), ffn_norm[l][None, :],
                     w_gate[l].astype(BF16), w_up[l].astype(BF16), w_down[l].astype(BF16),
                     final_norm[None, :], tm_ffn, final=(l == depth - 1))
    return x2.reshape(batch, seq, d_model)
```

```python
import functools
import math

import numpy as np
import jax
import jax.numpy as jnp
from jax import lax
from jax.experimental import pallas as pl
from jax.experimental.pallas import tpu as pltpu

HEAD_DIM = 64
NSA_HEADS = 8
NSA_KV_GROUPS = 2
NSA_HPG = NSA_HEADS // NSA_KV_GROUPS
CMP_LEN = 32
CMP_STRIDE = 16
CMP_HIDDEN = 256
SEL_BLOCK = 64
SEL_TOPK = 16
WINDOW = 512
GLA_HEADS = 4
GLA_DK = 32
GLA_DV = 64
GLA_RANK = 16
GLA_TAU = 16.0
SG_GROUPS = 4
SG_CH = 64
SG_CHUNK = 128
NSA_W = NSA_HEADS * HEAD_DIM
GLA_W = GLA_HEADS * GLA_DV
SG_W = SG_GROUPS * SG_CH
KV_W = NSA_KV_GROUPS * HEAD_DIM
IN_SIZES = (NSA_W, KV_W, KV_W, KV_W, KV_W, KV_W, KV_W, NSA_HEADS * 3,
            GLA_HEADS * GLA_DK, GLA_HEADS * GLA_DK, GLA_W, GLA_RANK, GLA_W, 2 * SG_W)
ROPE_THETA = 10000.0
NORM_EPS = 1e-6

LANES = 128
NEG_BIG = -1e30
VMEM_LIMIT = 56 * 1024 * 1024
FUSED_VMEM_LIMIT = 62 * 1024 * 1024

QPAD_W = NSA_HEADS * LANES
GEXP_W = 3 * NSA_W
GLA_QK_W = GLA_HEADS * GLA_DK
GLA_BLK = 256
GLA_SUB = 16
SEL_CH = 1024
WIN_KEYS = WINDOW + 2 * SEL_BLOCK
NSA_ROW_PARTS = 1
NSA_STEP_BLOCKS = 4
RANK_UNROLL = 4
NSA_PAIR = 2
GLA_PAIR = 4

ROW_TILE = 512
GLA_A_COPIES = 6
F32 = jnp.float32
BF16 = jnp.bfloat16


def _cparams(sem):
    return pltpu.CompilerParams(dimension_semantics=sem, vmem_limit_bytes=VMEM_LIMIT)


def _dot(a, b, precision=None):
    return jnp.dot(a, b, preferred_element_type=F32, precision=precision)


def _dot_nt(a, b, precision=None):
    return lax.dot_general(a, b, (((1,), (1,)), ((), ())),
                           preferred_element_type=F32, precision=precision)


def _dot_tn(a, b, precision=None):
    return lax.dot_general(a, b, (((0,), (0,)), ((), ())),
                           preferred_element_type=F32, precision=precision)


def _gelu_tanh(x):
    c = math.sqrt(2.0 / math.pi)
    return 0.5 * x * (1.0 + jnp.tanh(c * (x + 0.044715 * (x * x * x))))


def _rope_lanes(x, cos, sin_signed):
    n = x.shape[-1]
    lane = lax.broadcasted_iota(jnp.int32, x.shape, 1)
    first_half = (lane % HEAD_DIM) < (HEAD_DIM // 2)
    partner = jnp.where(first_half,
                        pltpu.roll(x, n - HEAD_DIM // 2, 1),
                        pltpu.roll(x, HEAD_DIM // 2, 1))
    return x * cos + partner * sin_signed


_INPROJ_OUT = (
    ("sg", SG_W, F32),
    ("qpad", QPAD_W, BF16), ("kc", KV_W, F32), ("vc", KV_W, F32),
    ("ksl", KV_W, BF16), ("vsl", KV_W, BF16), ("kwn", KV_W, BF16), ("vwn", KV_W, BF16),
    ("gts", LANES, F32), ("gq", GLA_QK_W, F32), ("gk", GLA_QK_W, F32),
    ("gv", GLA_W, F32), ("ga", LANES, F32), ("gr", GLA_W, F32))
_INPROJ_W = sum(w for _, w, _ in _INPROJ_OUT) + SG_W


def _spatial_gate(uv, lg, lb, wts, bias, lane_g):
    a = _gelu_tanh(uv)
    u = a[:, :SG_W]
    v = a[:, SG_W:]
    mu = jnp.mean(v, axis=-1, keepdims=True)
    var = jnp.mean(jnp.square(v - mu), axis=-1, keepdims=True)
    vn = ((v - mu) * lax.rsqrt(var + NORM_EPS) * lg + lb).astype(BF16)
    s = bias
    for g in range(SG_GROUPS):
        s = s + jnp.where(lane_g == g, _dot(wts[g], vn), 0.0)
    return u * s


def _inproj_kernel(x_ref, *refs):
    _inproj_body(x_ref[...], *refs)


def _inproj_body(x, g_ref, w_ref, cos_ref, sin_ref, lg_ref, lb_ref, ws_ref, sb_ref, *refs):
    out_refs, grp_scr = refs[:-1], refs[-1]
    hn = x * lax.rsqrt(jnp.mean(x * x, axis=-1, keepdims=True) + NORM_EPS) * g_ref[...]
    z = _dot(hn.astype(BF16), w_ref[...])
    cos = cos_ref[...]
    sin = sin_ref[...]
    off = 0
    for (name, width, dtype), o_ref in zip(_INPROJ_OUT, out_refs):
        if name in ("qpad", "ksl", "kwn"):
            for j in range(width // LANES):
                blk = _rope_lanes(z[:, off + j * LANES: off + (j + 1) * LANES], cos, sin)
                if name == "qpad":
                    blk = blk * (HEAD_DIM ** -0.5)
                o_ref[:, j * LANES:(j + 1) * LANES] = blk.astype(dtype)
        elif name in ("kc", "vc"):
            grp_scr[...] = z[:, off:off + width]
            for r in range(CMP_STRIDE):
                o_ref[:, r * width:(r + 1) * width] = grp_scr[pl.ds(r, x.shape[0] // CMP_STRIDE, stride=CMP_STRIDE), :]
        elif name == "sg":
            n = ws_ref.shape[1]
            ri = lax.broadcasted_iota(jnp.int32, (n, n), 0)
            ci = lax.broadcasted_iota(jnp.int32, (n, n), 1)
            lane_g = lax.broadcasted_iota(jnp.int32, (1, SG_W), 1) // SG_CH
            wts = [jnp.where(ci <= ri, ws_ref[g], 0.0).astype(BF16) for g in range(SG_GROUPS)]
            for c in range(x.shape[0] // n):
                rs = slice(c * n, (c + 1) * n)
                o_ref[rs, :] = _spatial_gate(z[rs, off:off + 2 * SG_W], lg_ref[...], lb_ref[...], wts,
                                             sb_ref[...], lane_g)
        else:
            o_ref[...] = z[:, off:off + width].astype(dtype)
        off += 2 * SG_W if name == "sg" else width


def _inproj(x2, gain, w_p, cos_t, sin_t, sg_params, seq, tm):
    T, D = x2.shape
    nper = seq // tm
    assert tm % SG_CHUNK == 0
    regroup = lambda n: CMP_STRIDE if n in ("kc", "vc") else 1
    out_shape = [jax.ShapeDtypeStruct((T // regroup(n), w * regroup(n)), dt) for n, w, dt in _INPROJ_OUT]
    out_specs = [pl.BlockSpec((tm // regroup(n), w * regroup(n)), lambda i: (i, 0)) for n, w, _ in _INPROJ_OUT]
    full = lambda a: pl.BlockSpec(a.shape, lambda i: (0,) * a.ndim)
    return pl.pallas_call(
        _inproj_kernel,
        grid=(T // tm,),
        in_specs=[
            pl.BlockSpec((tm, D), lambda i: (i, 0)),
            pl.BlockSpec((1, D), lambda i: (0, 0)),
            pl.BlockSpec((D, _INPROJ_W), lambda i: (0, 0), pipeline_mode=pl.Buffered(1)),
            pl.BlockSpec((tm, LANES), lambda i: (i % nper, 0)),
            pl.BlockSpec((tm, LANES), lambda i: (i % nper, 0)),
        ] + [full(a) for a in sg_params],
        out_specs=out_specs,
        out_shape=out_shape,
        scratch_shapes=[pltpu.VMEM((tm, KV_W), F32)],
        compiler_params=_cparams(("parallel",)),
        name="inproj",
    )(x2, gain, w_p, cos_t, sin_t, *sg_params)


def _prep_w_in(w_in):
    offs = np.cumsum((0,) + IN_SIZES)
    seg = [w_in[:, offs[k]:offs[k + 1]] for k in range(len(IN_SIZES))]
    (wq, wkc, wvc, wksl, wvsl, wkwn, wvwn, wg, wgq, wgk, wgv, wga, wgr, wuv) = seg
    D = w_in.shape[0]
    zeros64 = jnp.zeros((D, HEAD_DIM), w_in.dtype)
    qcols = []
    for h in range(NSA_HEADS):
        wh = wq[:, h * HEAD_DIM:(h + 1) * HEAD_DIM]
        qcols += [wh, zeros64] if h // NSA_HPG == 0 else [zeros64, wh]
    gcols = [jnp.pad(wg, ((0, 0), (0, LANES - wg.shape[1])))]
    wga_p = jnp.pad(jnp.concatenate([wga] * GLA_A_COPIES, axis=1), ((0, 0), (0, LANES - GLA_A_COPIES * GLA_RANK)))
    return jnp.concatenate([wuv] + qcols + [wkc, wvc, wksl, wvsl, wkwn, wvwn] + gcols
                           + [wgq, wgk, wgv, wga_p, wgr], axis=1).astype(BF16)


def _rope_tables(pos):
    half = HEAD_DIM // 2
    inv = 1.0 / (ROPE_THETA ** (jnp.arange(half, dtype=F32) / half))
    ang = pos.astype(F32)[:, None] * inv[None, :]
    cos = jnp.cos(ang)
    sin = jnp.sin(ang)
    cos_t = jnp.concatenate([cos, cos, cos, cos], axis=1)
    sin_t = jnp.concatenate([-sin, sin, -sin, sin], axis=1)
    return cos_t, sin_t


def _cmp_kernel(xk_ref, xv_ref, wek_ref, wev_ref, w1k_ref, w1v_ref, pk_ref, pv_ref,
                w2k_ref, w2v_ref, cos_ref, sin_ref, ok_ref, ov_ref):
    nrow = xk_ref.shape[1]
    row = lax.broadcasted_iota(jnp.int32, (nrow, LANES), 0)

    def compress(x_ref, we_ref, w1_ref, p_ref, w2_ref):
        h = _dot(x_ref[0].astype(BF16), we_ref[...])
        posb = _dot(p_ref[...].astype(BF16), w1_ref[...])[0:1, :]
        y = jnp.zeros((nrow, LANES), F32)
        for g in range(NSA_KV_GROUPS):
            a = h[:, g * CMP_HIDDEN:(g + 1) * CMP_HIDDEN]
            b = h[:, (NSA_KV_GROUPS + g) * CMP_HIDDEN:(NSA_KV_GROUPS + g + 1) * CMP_HIDDEN]
            hid = a + pltpu.roll(b, nrow - 1, 0) + posb
            y = y + _dot(_gelu_tanh(hid).astype(BF16), w2_ref[g])
        return y

    yk = _rope_lanes(compress(xk_ref, wek_ref, w1k_ref, pk_ref, w2k_ref), cos_ref[...], sin_ref[...])
    yv = compress(xv_ref, wev_ref, w1v_ref, pv_ref, w2v_ref)
    keep = row < nrow - 1
    ok_ref[0] = jnp.where(keep, yk, 0.0)
    ov_ref[0] = jnp.where(keep, yv, 0.0)


def _prep_cmp_weights(w1, w2, pos):
    half = CMP_LEN // CMP_STRIDE
    H = w1.shape[1]
    w1r = w1.reshape(half, CMP_STRIDE, HEAD_DIM, H)
    cols = []
    for a in range(half):
        for g in range(NSA_KV_GROUPS):
            blk = jnp.zeros((CMP_STRIDE, NSA_KV_GROUPS, HEAD_DIM, H), w1.dtype)
            blk = blk.at[:, g].set(w1r[a])
            cols.append(blk.reshape(CMP_STRIDE * KV_W, H))
    wexp = jnp.concatenate(cols, axis=1).astype(BF16)
    w2p = jnp.stack([jnp.pad(w2, ((0, 0), (g * HEAD_DIM, KV_W - (g + 1) * HEAD_DIM)))
                     for g in range(NSA_KV_GROUPS)]).astype(BF16)
    posf = jnp.pad(pos.reshape(1, CMP_LEN * HEAD_DIM), ((0, 7), (0, 0)))
    return wexp, w1.astype(BF16), posf, w2p


def _compress(kc, vc, wk, wv, cos_c, sin_c, batch, seq):
    nrow = seq // CMP_STRIDE
    xk = kc.reshape(batch, nrow, CMP_STRIDE * KV_W)
    xv = vc.reshape(batch, nrow, CMP_STRIDE * KV_W)
    wek, w1k, pk, w2k = wk
    wev, w1v, pv, w2v = wv
    full = lambda a: pl.BlockSpec(a.shape, lambda b: (0,) * a.ndim)
    xspec = pl.BlockSpec((1, nrow, CMP_STRIDE * KV_W), lambda b: (b, 0, 0))
    ospec = pl.BlockSpec((1, nrow, KV_W), lambda b: (b, 0, 0))
    return pl.pallas_call(
        _cmp_kernel,
        grid=(batch,),
        in_specs=[xspec, xspec, full(wek), full(wev), full(w1k), full(w1v), full(pk), full(pv),
                  full(w2k), full(w2v), full(cos_c), full(sin_c)],
        out_specs=[ospec, ospec],
        out_shape=[jax.ShapeDtypeStruct((batch, nrow, KV_W), F32)] * 2,
        compiler_params=_cparams(("parallel",)),
        name="compress",
    )(xk, xv, wek, wev, w1k, w1v, pk, pv, w2k, w2v, cos_c, sin_c)


def _unpad_heads(acc):
    qb = SEL_BLOCK
    lane = lax.broadcasted_iota(jnp.int32, (qb, LANES), 1)
    low = lane < HEAD_DIM
    outs = []
    for j in range(NSA_HEADS // 2):
        a = acc[(2 * j) * qb:(2 * j + 1) * qb]
        b = acc[(2 * j + 1) * qb:(2 * j + 2) * qb]
        if (2 * j) // NSA_HPG == 0:
            outs.append(jnp.where(low, a, pltpu.roll(b, HEAD_DIM, 1)))
        else:
            outs.append(jnp.where(low, pltpu.roll(a, HEAD_DIM, 1), b))
    return jnp.concatenate(outs, axis=1)


def _nsa_kernel(*refs):
    for j in range(NSA_STEP_BLOCKS):
        _nsa_block(j, *refs)


def _nsa_block(j, q_ref, g_ref, kc_ref, vc_ref, ksl_ref, vsl_ref, kwn_ref, vwn_ref,
               ktab_ref, kdiag_ref, kwtab_ref, qc_ref, qw_ref, eye_ref, gexp_ref, impt_ref, o_ref,
               imp_scr, m_scr, acc_scr, win_scr, kd_scr, qs_scr, sc_scr, cmp_scr):
    qb = SEL_BLOCK
    rows = NSA_HEADS * qb
    pair = range(q_ref.shape[0])
    i = pl.program_id(1) * NSA_STEP_BLOCKS + j
    tr = slice(j * qb, (j + 1) * qb)
    t = i * qb + lax.broadcasted_iota(jnp.int32, (rows, 1), 0) % qb
    ones = jnp.ones((SEL_CH, LANES), BF16)
    parts = [slice(p * rows // NSA_ROW_PARTS, (p + 1) * rows // NSA_ROW_PARTS) for p in range(NSA_ROW_PARTS)]
    qbf = [jnp.concatenate([q_ref[e, tr, h * LANES:(h + 1) * LANES] for h in range(NSA_HEADS)], axis=0)
           for e in pair]

    nblk_w = WIN_KEYS // qb
    ib = jnp.minimum(i, nblk_w - 1)
    w0 = pl.multiple_of((i - ib) * qb, qb)
    toff = pl.multiple_of((nblk_w - 1 - ib) * qb, qb)
    kwmask = kwtab_ref[pl.ds(toff, WIN_KEYS), :]
    for e in pair:
        kw = jnp.concatenate([kwn_ref[e, 0, pl.ds(w0, WIN_KEYS), :], kwmask], axis=1)
        vw = jnp.concatenate([vwn_ref[e, 0, pl.ds(w0, WIN_KEYS), :], ones[0:WIN_KEYS]], axis=1)
        q_win = jnp.concatenate([qbf[e], qw_ref[...]], axis=1)
        for rs in parts:
            sw = _dot_nt(q_win[rs], kw)
            ew = jnp.exp(sw - jnp.max(sw, axis=1, keepdims=True)).astype(BF16)
            ow = _dot(ew, vw)
            win_scr[e, rs, :] = ow[:, 0:LANES] / ow[:, LANES:2 * LANES]

    ncmp = kc_ref.shape[2]
    cend = lax.broadcasted_iota(jnp.int32, (1, ncmp), 1) * CMP_STRIDE + (CMP_LEN - 1)
    vis = cend <= t
    imap = impt_ref[...].astype(BF16)
    imap2 = jnp.concatenate([imap, imap], axis=1)
    for e in pair:
        s = _dot_nt(qbf[e], kc_ref[e, 0].astype(BF16))
        sm = jnp.where(vis, s, NEG_BIG)
        mx = jnp.max(sm, axis=1, keepdims=True)
        ex = jnp.where(vis, jnp.exp(sm - mx), 0.0)
        d = jnp.sum(ex, axis=1, keepdims=True)
        p = ex / jnp.where(d > 0, d, 1.0)
        cmp_scr[e] = _dot(p.astype(BF16), vc_ref[e, 0].astype(BF16))
        psum = jnp.concatenate(
            [sum(p[(g * NSA_HPG + h) * qb:(g * NSA_HPG + h + 1) * qb] for h in range(NSA_HPG))
             for g in range(NSA_KV_GROUPS)], axis=0)
        p_hi = psum.astype(BF16)
        p_lo = (psum - p_hi.astype(F32)).astype(BF16)
        imp_scr[e] = _dot_nt(imap2, jnp.concatenate([p_hi, p_lo], axis=1))

    nsbp = imp_scr.shape[1]
    imps = [imp_scr[e] for e in pair]
    n_id = lax.broadcasted_iota(jnp.int32, (nsbp, 1), 0)
    valid = n_id <= i
    forced = (n_id == 0) | (n_id == i) | (n_id == i - 1)
    nforced = 1 + (i >= 1).astype(jnp.int32) + (i >= 2).astype(jnp.int32)

    def rank_body(j, cnts):
        cnts = list(cnts)
        for u in range(RANK_UNROLL):
            m = 1 + RANK_UNROLL * j + u
            tie = jnp.where(m < n_id, 1.0, 0.0)
            for e in pair:
                rowm = imp_scr[e, pl.ds(jnp.minimum(m, nsbp - 1), 1), :]
                rowm = jnp.where(m <= i - 2, rowm, NEG_BIG)
                cnts[e] = cnts[e] + jnp.where(rowm > imps[e], 1.0, 0.0) + jnp.where(rowm == imps[e], tie, 0.0)
        return tuple(cnts)

    ntrip = (jnp.maximum(i - 2, 0) + RANK_UNROLL - 1) // RANK_UNROLL
    cnts = lax.fori_loop(0, ntrip, rank_body, tuple(jnp.zeros(imps[0].shape, F32) for _ in pair))
    budget = (SEL_TOPK - nforced).astype(F32)

    for e in pair:
        sel = valid & (forced | (cnts[e] < budget))
        notsel_t = jnp.where(sel, 0.0, 1.0)
        notsel = _dot_tn(notsel_t, eye_ref[...])
        notsel = jnp.concatenate([notsel[g * qb:(g + 1) * qb]
                                  for g in range(NSA_KV_GROUPS) for _ in range(NSA_HPG)], axis=0)
        qs_scr[e] = jnp.concatenate([qbf[e], (notsel + qc_ref[...]).astype(BF16)], axis=1)
    m_scr[...] = jnp.full(m_scr.shape, NEG_BIG, F32)
    acc_scr[...] = jnp.zeros(acc_scr.shape, F32)
    half = SEL_CH // 2
    nhalf = i // (half // qb) + 1
    lead = nhalf % 2
    nfull = nhalf // 2
    base = lead * half
    k0_last = pl.multiple_of(jnp.maximum(base + (nfull - 1) * SEL_CH, 0), half)
    off = pl.multiple_of(i * qb - k0_last, qb)
    kd_scr[...] = ktab_ref[pl.ds(k0_last, SEL_CH), :]
    kd_scr[pl.ds(off, qb), :] = kd_scr[pl.ds(off, qb), :] + kdiag_ref[...]

    def scores(slot, k0, size, kmask):
        for e in pair:
            k = jnp.concatenate([ksl_ref[e, 0, pl.ds(k0, size), :], kmask], axis=1)
            for rs in parts:
                sc = _dot_nt(qs_scr[e, rs, :], k)
                sc_scr[e, slot, rs, 0:size] = sc
                mx = m_scr[e, rs, :]
                for j in range(size // LANES):
                    mx = jnp.maximum(mx, sc[:, j * LANES:(j + 1) * LANES])
                m_scr[e, rs, :] = mx

    def values(slot, k0, size):
        for e in pair:
            v = jnp.concatenate([vsl_ref[e, 0, pl.ds(k0, size), :], ones[0:size]], axis=1)
            for rs in parts:
                mb = m_scr[e, rs, :]
                pr = jnp.concatenate([jnp.exp(sc_scr[e, slot, rs, j * LANES:(j + 1) * LANES] - mb)
                                      for j in range(size // LANES)], axis=1).astype(BF16)
                acc_scr[e, rs, :] += _dot(pr, v)

    @pl.when(lead == 1)
    def _():
        scores(0, 0, half, jnp.where(nfull == 0, kd_scr[0:half, :], ktab_ref[0:half, :]))

    def score_body(c, carry):
        k0 = pl.multiple_of(base + c * SEL_CH, half)
        scores(1 + c, k0, SEL_CH, jnp.where(c == nfull - 1, kd_scr[...], ktab_ref[pl.ds(k0, SEL_CH), :]))
        return carry

    lax.fori_loop(0, nfull, score_body, 0)
    for e in pair:
        m_scr[e] = jnp.broadcast_to(jnp.max(m_scr[e], axis=1, keepdims=True), (rows, LANES))

    @pl.when(lead == 1)
    def _():
        values(0, 0, half)

    def value_body(c, carry):
        values(1 + c, pl.multiple_of(base + c * SEL_CH, half), SEL_CH)
        return carry

    lax.fori_loop(0, nfull, value_body, 0)

    for e in pair:
        o_slc = acc_scr[e, :, 0:LANES] / acc_scr[e, :, LANES:2 * LANES]
        g_hi = jax.nn.sigmoid(g_ref[e, tr, :])
        g_lo = g_hi - g_hi.astype(BF16).astype(F32)
        gate = _dot(jnp.concatenate([g_hi.astype(BF16), g_lo.astype(BF16)], axis=1), gexp_ref[...])
        o_ref[e, tr, :] = (gate[:, 0:NSA_W] * _unpad_heads(cmp_scr[e])
                    + gate[:, NSA_W:2 * NSA_W] * _unpad_heads(o_slc)
                    + gate[:, 2 * NSA_W:3 * NSA_W] * _unpad_heads(win_scr[e]))


def _nsa(qpad, gts, kcmp, vcmp, ksl, vsl, kwn, vwn, consts, batch, seq):
    qb = SEL_BLOCK
    nq = seq // qb
    rows = NSA_HEADS * qb
    npair = NSA_PAIR if batch % NSA_PAIR == 0 else 1
    nbh = batch // npair
    tok = lambda a: a.reshape(npair, nbh * seq, a.shape[-1])
    per = lambda a: a.reshape(npair, nbh, a.shape[-2], KV_W)
    assert nq % NSA_STEP_BLOCKS == 0
    nst = nq // NSA_STEP_BLOCKS
    tok_spec = lambda w: pl.BlockSpec((npair, NSA_STEP_BLOCKS * qb, w), lambda bh, i: (0, bh * nst + i, 0))
    per_spec = lambda n: pl.BlockSpec((npair, 1, n, KV_W), lambda bh, i: (0, bh, 0, 0))
    full = lambda a: pl.BlockSpec(a.shape, lambda bh, i: (0,) * a.ndim)
    out = pl.pallas_call(
        _nsa_kernel,
        grid=(nbh, nst),
        in_specs=[
            tok_spec(QPAD_W), tok_spec(LANES),
            per_spec(kcmp.shape[1]), per_spec(vcmp.shape[1]),
            per_spec(seq), per_spec(seq), per_spec(seq), per_spec(seq),
        ] + [full(c) for c in consts],
        out_specs=tok_spec(NSA_W),
        out_shape=jax.ShapeDtypeStruct((npair, nbh * seq, NSA_W), F32),
        scratch_shapes=[
            pltpu.VMEM((npair, consts[-1].shape[0], NSA_KV_GROUPS * qb), F32),
            pltpu.VMEM((npair, rows, LANES), F32),
            pltpu.VMEM((npair, rows, 2 * LANES), F32),
            pltpu.VMEM((npair, rows, LANES), F32),
            pltpu.VMEM((SEL_CH, LANES), BF16),
            pltpu.VMEM((npair, rows, 2 * LANES), BF16),
            pltpu.VMEM((npair, seq // SEL_CH + 1, rows, SEL_CH), F32),
            pltpu.VMEM((npair, rows, LANES), F32),
        ],
        compiler_params=_cparams(("parallel", "arbitrary")),
        name="nsa",
    )(tok(qpad), tok(gts), per(kcmp), per(vcmp), per(ksl.reshape(batch, seq, KV_W)),
      per(vsl.reshape(batch, seq, KV_W)), per(kwn.reshape(batch, seq, KV_W)),
      per(vwn.reshape(batch, seq, KV_W)), *consts)
    return out.reshape(batch * seq, NSA_W)


def _nsa_constants(seq):
    qb = SEL_BLOCK
    nc_pad = seq // CMP_STRIDE
    nsb = seq // qb
    assert nsb <= qb
    nsb_pad = max(nsb, 8)
    cs = np.arange(nc_pad) * CMP_STRIDE
    ce = cs + CMP_LEN
    bs = np.arange(nsb_pad) * qb
    be = bs + qb
    ov = np.clip(np.minimum(ce[None, :], be[:, None]) - np.maximum(cs[None, :], bs[:, None]), 0, None)
    impt = ov / CMP_LEN
    pos = np.arange(seq)
    lane = np.arange(LANES)[None, :]
    ktab = np.where(lane == pos[:, None] // qb, NEG_BIG, 0.0)
    kdiag = np.where(lane == qb + np.arange(qb)[:, None], NEG_BIG, 0.0)
    last = WIN_KEYS // qb - 1
    x = np.arange(WIN_KEYS + last * qb)[:, None]
    kb, j = x // qb, x % qb
    kwtab = np.where((((kb == 0) | (kb > last)) & (lane == 0)) | ((kb == 1) & (lane == j))
                     | ((kb == last) & (lane == qb + j)), NEG_BIG, 0.0)
    r = (np.arange(NSA_HEADS * qb) % qb)[:, None]
    upper = (lane >= qb) & (lane - qb > r)
    qc = np.where(upper, 1.0, 0.0)
    qw = np.where(upper | ((lane < qb) & (lane <= r)), 1.0, 0.0)
    eye = np.eye(nsb_pad, LANES)
    col = np.arange(GEXP_W)[None, :]
    src = np.arange(LANES)[:, None]
    rep = (src == (col % NSA_W) // HEAD_DIM * 3 + col // NSA_W) & (src < 3 * NSA_HEADS)
    gexp = np.concatenate([rep, rep], axis=0)
    return (jnp.asarray(ktab, BF16), jnp.asarray(kdiag, BF16), jnp.asarray(kwtab, BF16),
            jnp.asarray(qc, F32), jnp.asarray(qw, BF16), jnp.asarray(eye, F32), jnp.asarray(gexp, BF16),
            jnp.asarray(impt, F32))


def _split3(x):
    hi = x.astype(BF16)
    r1 = x - hi.astype(F32)
    mid = r1.astype(BF16)
    lo = (r1 - mid.astype(F32)).astype(BF16)
    return hi, mid, lo


def _gla_kernel(q_ref, k_ref, v_ref, a_ref, r_ref, w6_ref, bup_ref, ng_ref, o_ref, st_ref):
    pair = range(q_ref.shape[0])
    blk = q_ref.shape[1]
    nsub = blk // GLA_SUB

    @pl.when(pl.program_id(1) == 0)
    def _():
        st_ref[...] = jnp.zeros(st_ref.shape, F32)

    ri = lax.broadcasted_iota(jnp.int32, (blk, blk), 0)
    ci = lax.broadcasted_iota(jnp.int32, (blk, blk), 1)
    ltri = jnp.where(ci <= ri, 1.0, 0.0).astype(BF16)
    ltri3 = jnp.concatenate([ltri, ltri, ltri], axis=1)
    lane_rep = lax.broadcasted_iota(jnp.int32, (1, LANES), 1) // GLA_RANK
    srow = lax.broadcasted_iota(jnp.int32, st_ref.shape[1:], 0) // GLA_DV
    scol = lax.broadcasted_iota(jnp.int32, st_ref.shape[1:], 1) // GLA_DK
    krow = lax.broadcasted_iota(jnp.int32, (blk, 1), 0)
    lane_qk = lax.broadcasted_iota(jnp.int32, (1, GLA_QK_W), 1) // GLA_DK
    lane_v = lax.broadcasted_iota(jnp.int32, (1, GLA_W), 1) // GLA_DV
    qrow = lax.broadcasted_iota(jnp.int32, (GLA_HEADS * GLA_SUB, 1), 0) % GLA_SUB
    kcol = lax.broadcasted_iota(jnp.int32, (1, blk), 1)
    gi = lax.broadcasted_iota(jnp.int32, (GLA_W, GLA_W), 0) // GLA_DV
    gj = lax.broadcasted_iota(jnp.int32, (GLA_W, GLA_W), 1) // GLA_DV
    gmean = jnp.where(gi == gj, 1.0 / GLA_DV, 0.0).astype(BF16)
    gmean2 = jnp.concatenate([gmean, gmean], axis=0)

    for e in pair:
        a_hi, a_mid, a_lo = _split3(a_ref[e])
        a6 = jnp.where(lane_rep < 3, a_hi, jnp.where(lane_rep < 5, a_mid, a_lo))
        x = _dot(a6, w6_ref[...]) + bup_ref[...]
        g = (jnp.minimum(x, 0.0) - jnp.log1p(jnp.exp(-jnp.abs(x)))) / GLA_TAU
        b = _dot(ltri3, jnp.concatenate(_split3(g), axis=0))
        blast = b[blk - 1:blk, :]
        q = q_ref[e] * (GLA_DK ** -0.5)
        k = k_ref[e]
        v = v_ref[e]
        vb = v.astype(BF16)

        st = st_ref[e]
        o = _dot_nt((q * jnp.exp(b)).astype(BF16), st.astype(BF16))
        khat = k * jnp.exp(blast - b)
        st_ref[e] = st * jnp.exp(blast) + jnp.where(srow == scol, _dot_tn(v, khat), 0.0)

        for c in range(nsub):
            lo = c * GLA_SUB
            ref_b = b[lo:lo + 1, :]
            qt = q[lo:lo + GLA_SUB] * jnp.exp(b[lo:lo + GLA_SUB] - ref_b)
            kt = k * jnp.exp(jnp.where(krow < lo + GLA_SUB, ref_b - b, 0.0))
            qs = jnp.concatenate([jnp.where(lane_qk == h, qt, 0.0) for h in range(GLA_HEADS)], axis=0)
            a = _dot_nt(qs.astype(BF16), kt.astype(BF16))
            a = jnp.where(kcol <= lo + qrow, a, 0.0)
            r = _dot(a.astype(BF16), vb)
            oi = sum(jnp.where(lane_v == h, r[h * GLA_SUB:(h + 1) * GLA_SUB], 0.0) for h in range(GLA_HEADS))
            o_ref[e, lo:lo + GLA_SUB, :] = o[lo:lo + GLA_SUB] + oi

    for e in pair:
        o = o_ref[e]
        oo = o * o
        oo_hi = oo.astype(BF16)
        oo_lo = (oo - oo_hi.astype(F32)).astype(BF16)
        ms = _dot(jnp.concatenate([oo_hi, oo_lo], axis=1), gmean2)
        rr = r_ref[e]
        o_ref[e] = o * lax.rsqrt(ms + NORM_EPS) * ng_ref[...] * (rr * jax.nn.sigmoid(rr))


def _prep_gla_wup(w_up):
    hi = w_up.astype(BF16)
    r1 = w_up - hi.astype(F32)
    mid = r1.astype(BF16)
    lo = (r1 - mid.astype(F32)).astype(BF16)
    w6 = jnp.concatenate([hi, mid, lo, hi, mid, hi], axis=0)
    return jnp.pad(w6, ((0, LANES - w6.shape[0]), (0, 0)))


def _gla(gq, gk, gv, ga, gr, w6, bup, ng, batch, seq):
    blk = min(GLA_BLK, seq)
    nb = seq // blk
    npair = GLA_PAIR if batch % GLA_PAIR == 0 else 1
    nbh = batch // npair
    tok = lambda a: a.reshape(npair, nbh * seq, a.shape[-1])
    spec = lambda w: pl.BlockSpec((npair, blk, w), lambda bh, j: (0, bh * nb + j, 0))
    full = lambda a: pl.BlockSpec(a.shape, lambda bh, j: (0,) * a.ndim)
    out = pl.pallas_call(
        _gla_kernel,
        grid=(nbh, nb),
        in_specs=[spec(GLA_QK_W), spec(GLA_QK_W), spec(GLA_W), spec(LANES), spec(GLA_W),
                  full(w6), full(bup), full(ng)],
        out_specs=spec(GLA_W),
        out_shape=jax.ShapeDtypeStruct((npair, nbh * seq, GLA_W), F32),
        scratch_shapes=[pltpu.VMEM((npair, GLA_W, GLA_QK_W), F32)],
        compiler_params=_cparams(("parallel", "arbitrary")),
        name="gla",
    )(tok(gq), tok(gk), tok(gv), tok(ga), tok(gr), w6, bup, ng)
    return out.reshape(batch * seq, GLA_W)


def _outffn_kernel(x_ref, on_ref, og_ref, os_ref, wo_ref, fg_ref, wg_ref, wu_ref, wd_ref,
                   fin_ref, o_ref, *, final):
    x = x_ref[...]
    x = x + _dot(on_ref[...].astype(BF16), wo_ref[0:NSA_W, :])
    x = x + _dot(og_ref[...].astype(BF16), wo_ref[NSA_W:NSA_W + GLA_W, :])
    x = x + _dot(os_ref[...].astype(BF16), wo_ref[NSA_W + GLA_W:, :])
    hn = (x * lax.rsqrt(jnp.mean(x * x, axis=-1, keepdims=True) + NORM_EPS) * fg_ref[...]).astype(BF16)
    gt = _dot(hn, wg_ref[...])
    up = _dot(hn, wu_ref[...])
    y = x + _dot((gt * jax.nn.sigmoid(gt) * up).astype(BF16), wd_ref[...])
    if final:
        y = y * lax.rsqrt(jnp.mean(y * y, axis=-1, keepdims=True) + NORM_EPS) * fin_ref[...]
    o_ref[...] = y


def _ffn_inproj_kernel(x_ref, on_ref, og_ref, os_ref, wo_ref, fg_ref, wg_ref, wu_ref, wd_ref, *refs):
    inproj_in, o_ref, rest = refs[:8], refs[8], refs[9:]
    x = x_ref[...]
    x = x + _dot(on_ref[...].astype(BF16), wo_ref[0:NSA_W, :])
    x = x + _dot(og_ref[...].astype(BF16), wo_ref[NSA_W:NSA_W + GLA_W, :])
    x = x + _dot(os_ref[...].astype(BF16), wo_ref[NSA_W + GLA_W:, :])
    hn = (x * lax.rsqrt(jnp.mean(x * x, axis=-1, keepdims=True) + NORM_EPS) * fg_ref[...]).astype(BF16)
    gt = _dot(hn, wg_ref[...])
    up = _dot(hn, wu_ref[...])
    y = x + _dot((gt * jax.nn.sigmoid(gt) * up).astype(BF16), wd_ref[...])
    o_ref[...] = y
    _inproj_body(y, *inproj_in, *rest)


def _ffn_inproj(x2, o_nsa, o_gla, o_sg, wo, fgain, wg, wu, wd, gain, w_p, cos_t, sin_t, sg_params, seq, tm):
    T, D = x2.shape
    nper = seq // tm
    row = lambda w: pl.BlockSpec((tm, w), lambda i: (i, 0))
    const = lambda a: pl.BlockSpec(a.shape, lambda i: (0,) * a.ndim, pipeline_mode=pl.Buffered(1))
    pos = pl.BlockSpec((tm, LANES), lambda i: (i % nper, 0))
    regroup = lambda n: CMP_STRIDE if n in ("kc", "vc") else 1
    out_shape = [jax.ShapeDtypeStruct((T, D), F32)] + [
        jax.ShapeDtypeStruct((T // regroup(n), w * regroup(n)), dt) for n, w, dt in _INPROJ_OUT]
    out_specs = [row(D)] + [pl.BlockSpec((tm // regroup(n), w * regroup(n)), lambda i: (i, 0))
                            for n, w, _ in _INPROJ_OUT]
    return pl.pallas_call(
        _ffn_inproj_kernel,
        grid=(T // tm,),
        in_specs=[row(D), row(NSA_W), row(GLA_W), row(SG_W), const(wo), const(fgain),
                  const(wg), const(wu), const(wd), const(gain), const(w_p), pos, pos]
                 + [const(a) for a in sg_params],
        out_specs=out_specs,
        out_shape=out_shape,
        scratch_shapes=[pltpu.VMEM((tm, KV_W), F32)],
        compiler_params=pltpu.CompilerParams(dimension_semantics=("parallel",),
                                             vmem_limit_bytes=FUSED_VMEM_LIMIT),
        name="ffn_inproj",
    )(x2, o_nsa, o_gla, o_sg, wo, fgain, wg, wu, wd, gain, w_p, cos_t, sin_t, *sg_params)


def _outffn(x2, o_nsa, o_gla, o_sg, wo, fgain, wg, wu, wd, fin, tm, final):
    T, D = x2.shape
    row = lambda w: pl.BlockSpec((tm, w), lambda i: (i, 0))
    const = lambda a: pl.BlockSpec(a.shape, lambda i: (0,) * a.ndim, pipeline_mode=pl.Buffered(1))
    return pl.pallas_call(
        functools.partial(_outffn_kernel, final=final),
        grid=(T // tm,),
        in_specs=[row(D), row(NSA_W), row(GLA_W), row(SG_W), const(wo), const(fgain),
                  const(wg), const(wu), const(wd), const(fin)],
        out_specs=row(D),
        out_shape=jax.ShapeDtypeStruct((T, D), F32),
        compiler_params=_cparams(("parallel",)),
        name="outproj_ffn",
    )(x2, o_nsa, o_gla, o_sg, wo, fgain, wg, wu, wd, fin)


def kernel(x, attn_norm, w_in, cmp_pos_k, cmp_w1_k, cmp_w2_k, cmp_pos_v, cmp_w1_v, cmp_w2_v,
           gla_w_up, gla_b_up, gla_norm, sg_ln_g, sg_ln_b, sg_w, sg_b, w_out,
           ffn_norm, w_gate, w_up, w_down, final_norm):
    batch, seq, d_model = x.shape
    depth = w_in.shape[0]
    T = batch * seq
    assert seq % GLA_BLK == 0 and seq >= WIN_KEYS and seq % SEL_CH == 0
    tm_in = tm_ffn = min(ROW_TILE, seq)

    cos_t, sin_t = _rope_tables(jnp.arange(seq))
    nrow = seq // CMP_STRIDE
    cos_c, sin_c = _rope_tables(jnp.arange(nrow) * CMP_STRIDE + CMP_LEN - 1)
    nsa_consts = _nsa_constants(seq)

    x2 = x.reshape(T, d_model)
    def inproj_args(l):
        sg_params = (sg_ln_g[l][None, :], sg_ln_b[l][None, :], sg_w[l],
                     jnp.repeat(sg_b[l].T, SG_CH, axis=1))
        return attn_norm[l][None, :], _prep_w_in(w_in[l]), cos_t, sin_t, sg_params

    proj = _inproj(x2, *inproj_args(0), seq, tm_in)
    for l in range(depth):
        (o_sg, qpad, kc, vc, ksl, vsl, kwn, vwn, gts, gq, gk, gv, ga, gr) = proj
        kcmp, vcmp = _compress(
            kc, vc, _prep_cmp_weights(cmp_w1_k[l], cmp_w2_k[l], cmp_pos_k[l]),
            _prep_cmp_weights(cmp_w1_v[l], cmp_w2_v[l], cmp_pos_v[l]), cos_c, sin_c, batch, seq)
        o_nsa = _nsa(qpad, gts, kcmp, vcmp, ksl, vsl, kwn, vwn, nsa_consts, batch, seq)
        o_gla = _gla(gq, gk, gv, ga, gr, _prep_gla_wup(gla_w_up[l]), gla_b_up[l][None, :],
                     jnp.tile(gla_norm[l], GLA_HEADS)[None, :], batch, seq)
        ffn_w = (w_out[l].astype(BF16), ffn_norm[l][None, :],
                 w_gate[l].astype(BF16), w_up[l].astype(BF16), w_down[l].astype(BF16))
        if l + 1 < depth:
            x2, *proj = _ffn_inproj(x2, o_nsa, o_gla, o_sg, *ffn_w, *inproj_args(l + 1), seq, tm_ffn)
        else:
            x2 = _outffn(x2, o_nsa, o_gla, o_sg, *ffn_w, final_norm[None, :], tm_ffn, final=True)
    return x2.reshape(batch, seq, d_model)
```
